```python
import math
import jax, jax.numpy as jnp
from jax import lax
import numpy as np

D_MODEL = 2048
BATCH = 1
SEQ = 8192
DEPTH = 2

HEAD_DIM = 128
N_HEADS_A = D_MODEL // HEAD_DIM
N_KV_A = 4
ROT_DIM = HEAD_DIM // 4
IDX_HEADS = 16
IDX_DIM = 64
IDX_ROT = IDX_DIM // 4
TOPK_MAX = 256
N_HEADS_B = D_MODEL // HEAD_DIM
N_KV_B = 4
BLOCK_Q = 128
PLE_DIM = 256
ROPE_THETA = 500000.0
LN_EPS = 1e-5
N_A_LAYERS = DEPTH // 2
N_B_LAYERS = DEPTH - N_A_LAYERS
ALPHA = (2.0 * DEPTH) ** 0.25
BETA = (8.0 * DEPTH) ** -0.25

WIDTH_A = N_HEADS_A * HEAD_DIM
KVW_A = N_KV_A * HEAD_DIM
SPLITS_A = tuple(int(c) for c in np.cumsum([WIDTH_A, KVW_A, KVW_A, WIDTH_A, IDX_HEADS * IDX_DIM, IDX_DIM]))
COLS_A = WIDTH_A + 2 * KVW_A + WIDTH_A + IDX_HEADS * IDX_DIM + IDX_DIM + IDX_HEADS
WIDTH_B = N_HEADS_B * HEAD_DIM
KVW_B = N_KV_B * HEAD_DIM

kernel_name = "yoco_dsa_stickbreaking_deepnorm_ple"


def layer_norm(x, g, b):
    xf = x.astype(jnp.float32)
    mu = jnp.mean(xf, axis=-1, keepdims=True)
    var = jnp.mean(jnp.square(xf - mu), axis=-1, keepdims=True)
    y = (xf - mu) * lax.rsqrt(var + LN_EPS) * g.astype(jnp.float32) + b.astype(jnp.float32)
    return y.astype(x.dtype)


def partial_rope(x, positions, rot_dim):
    half = rot_dim // 2
    inv = 1.0 / (ROPE_THETA ** (jnp.arange(half, dtype=jnp.float32) / half))
    ang = positions.astype(jnp.float32)[..., None] * inv
    cos = jnp.cos(ang)[:, :, None, :]
    sin = jnp.sin(ang)[:, :, None, :]
    xr = x[..., :rot_dim].astype(jnp.float32)
    x1, x2 = xr[..., :half], xr[..., half:]
    rot = jnp.concatenate([x1 * cos - x2 * sin, x1 * sin + x2 * cos], axis=-1)
    return jnp.concatenate([rot.astype(x.dtype), x[..., rot_dim:]], axis=-1)


def to_blocks(a):
    b, s = a.shape[0], a.shape[1]
    return jnp.swapaxes(a.reshape((b, s // BLOCK_Q, BLOCK_Q) + a.shape[2:]), 0, 1)


def from_blocks(a):
    a = jnp.swapaxes(a, 0, 1)
    return a.reshape((a.shape[0], a.shape[1] * a.shape[2]) + a.shape[3:])


def mixer_a(x, w_in, w_o, positions):
    b, s, _ = x.shape
    proj = x @ w_in
    q, k, v, gate, iq, ik, iw = jnp.split(proj, SPLITS_A, axis=-1)
    q = partial_rope(q.reshape(b, s, N_HEADS_A, HEAD_DIM), positions, ROT_DIM)
    k = partial_rope(k.reshape(b, s, N_KV_A, HEAD_DIM), positions, ROT_DIM)
    v = v.reshape(b, s, N_KV_A, HEAD_DIM)
    iq = partial_rope(iq.reshape(b, s, IDX_HEADS, IDX_DIM), positions, IDX_ROT)
    ik = partial_rope(ik.reshape(b, s, 1, IDX_DIM), positions, IDX_ROT)[:, :, 0, :]
    n_sel = min(TOPK_MAX, s // 4)
    groups = N_HEADS_A // N_KV_A
    key_pos = jnp.arange(s, dtype=jnp.int32)
    idx_scale = (IDX_DIM ** -0.5) * (IDX_HEADS ** -0.5)
    starts = jnp.arange(s // BLOCK_Q, dtype=jnp.int32) * BLOCK_Q

    def block_fn(args):
        qb, iqb, iwb, start = args
        t = start + jnp.arange(BLOCK_Q, dtype=jnp.int32)
        dots = jnp.einsum('bqhd,bsd->bqhs', iqb, ik).astype(jnp.float32)
        iscore = jnp.einsum('bqhs,bqh->bqs', jax.nn.relu(dots), iwb.astype(jnp.float32)) * idx_scale
        causal = key_pos[None, :] <= t[:, None]
        iscore = jnp.where(causal[None], iscore, -jnp.inf)
        _, idx = lax.top_k(iscore, n_sel)
        ksel = jax.vmap(lambda kk, ii: kk[ii])(k, idx)
        vsel = jax.vmap(lambda vv, ii: vv[ii])(v, idx)
        qg = qb.reshape(b, BLOCK_Q, N_KV_A, groups, HEAD_DIM)
        logits = jnp.einsum('bqngd,bqknd->bqngk', qg, ksel).astype(jnp.float32) * (HEAD_DIM ** -0.5)
        valid = idx <= t[None, :, None]
        logits = jnp.where(valid[:, :, None, None, :], logits, -jnp.inf)
        wts = jax.nn.softmax(logits, axis=-1).astype(v.dtype)
        o = jnp.einsum('bqngk,bqknd->bqngd', wts, vsel)
        return o.reshape(b, BLOCK_Q, WIDTH_A)

    o = from_blocks(lax.map(block_fn, (to_blocks(q), to_blocks(iq), to_blocks(iw), starts)))
    return (o * jax.nn.silu(gate)) @ w_o


def shared_kv(x, w_kv):
    b, s, _ = x.shape
    k, v = jnp.split(x @ w_kv, [KVW_B], axis=-1)
    return k.reshape(b, s, N_KV_B, HEAD_DIM), v.reshape(b, s, N_KV_B, HEAD_DIM)


def mixer_b(x, w_in, w_o, k, v):
    b, s, _ = x.shape
    q, gate = jnp.split(x @ w_in, [WIDTH_B], axis=-1)
    q = q.reshape(b, s, N_HEADS_B, HEAD_DIM)
    groups = N_HEADS_B // N_KV_B
    key_pos = jnp.arange(s, dtype=jnp.int32)
    starts = jnp.arange(s // BLOCK_Q, dtype=jnp.int32) * BLOCK_Q

    def block_fn(args):
        qb, start = args
        t = start + jnp.arange(BLOCK_Q, dtype=jnp.int32)
        qg = qb.reshape(b, BLOCK_Q, N_KV_B, groups, HEAD_DIM)
        z = jnp.einsum('bqngd,bsnd->bngqs', qg, k).astype(jnp.float32) * (HEAD_DIM ** -0.5)
        strict = key_pos[None, :] < t[:, None]
        log_om = jnp.where(strict, -jax.nn.softplus(z), 0.0)
        suffix = lax.cumsum(log_om, axis=log_om.ndim - 1, reverse=True) - log_om
        a = jnp.where(strict, jnp.exp(jax.nn.log_sigmoid(z) + suffix), 0.0)
        o = jnp.einsum('bngqs,bsnd->bqngd', a.astype(v.dtype), v)
        return o.reshape(b, BLOCK_Q, WIDTH_B)

    o = from_blocks(lax.map(block_fn, (to_blocks(q), starts)))
    return (o * jax.nn.silu(gate)) @ w_o


def per_layer_embed(x, p_i, w_p, w_g):
    return (p_i @ w_p) * jax.nn.sigmoid(x @ w_g)


def setup_inputs(seed: int = 0) -> dict:
    key = jax.random.key(seed)
    ks = jax.random.split(key, 12)
    nrm = jax.random.normal
    x = nrm(ks[0], (BATCH, SEQ, D_MODEL), jnp.float32)
    p = nrm(ks[1], (DEPTH, BATCH, SEQ, PLE_DIM), jnp.float32)
    positions = jnp.broadcast_to(jnp.arange(SEQ, dtype=jnp.int32)[None, :], (BATCH, SEQ))
    w_in_a = nrm(ks[2], (N_A_LAYERS, D_MODEL, COLS_A), jnp.float32) * D_MODEL ** -0.5
    w_o_a = nrm(ks[3], (N_A_LAYERS, WIDTH_A, D_MODEL), jnp.float32) * (WIDTH_A ** -0.5 * BETA)
    w_kv_b = nrm(ks[4], (D_MODEL, 2 * KVW_B), jnp.float32) * D_MODEL ** -0.5
    w_in_b = nrm(ks[5], (N_B_LAYERS, D_MODEL, 2 * WIDTH_B), jnp.float32) * D_MODEL ** -0.5
    w_o_b = nrm(ks[6], (N_B_LAYERS, WIDTH_B, D_MODEL), jnp.float32) * (WIDTH_B ** -0.5 * BETA)
    ln_g = 1.0 + 0.02 * nrm(ks[7], (DEPTH, D_MODEL), jnp.float32)
    ln_b = 0.02 * nrm(ks[8], (DEPTH, D_MODEL), jnp.float32)
    w_ple = nrm(ks[9], (DEPTH, PLE_DIM, D_MODEL), jnp.float32) * PLE_DIM ** -0.5
    w_ple_gate = nrm(ks[10], (DEPTH, D_MODEL, D_MODEL), jnp.float32) * D_MODEL ** -0.5
    return {"x": x, "p": p, "positions": positions, "w_in_a": w_in_a, "w_o_a": w_o_a,
            "w_kv_b": w_kv_b, "w_in_b": w_in_b, "w_o_b": w_o_b, "ln_g": ln_g, "ln_b": ln_b,
            "w_ple": w_ple, "w_ple_gate": w_ple_gate}


def reference(x, p, positions, w_in_a, w_o_a, w_kv_b, w_in_b, w_o_b, ln_g, ln_b, w_ple, w_ple_gate):
    k_sh, v_sh = None, None
    for i in range(DEPTH):
        if i < N_A_LAYERS:
            h = mixer_a(x, w_in_a[i], w_o_a[i], positions)
        else:
            if i == N_A_LAYERS:
                k_sh, v_sh = shared_kv(x, w_kv_b)
            j = i - N_A_LAYERS
            h = mixer_b(x, w_in_b[j], w_o_b[j], k_sh, v_sh)
        x = layer_norm(ALPHA * x + h, ln_g[i], ln_b[i])
        x = x + per_layer_embed(x, p[i], w_ple[i], w_ple_gate[i])
    return x
```

```python
import functools

import jax
import jax.numpy as jnp
from jax import lax
from jax.experimental import pallas as pl
from jax.experimental.pallas import tpu as pltpu

HEAD_DIM = 128
N_KV = 4
ROT_DIM = HEAD_DIM // 4
IDX_HEADS = 16
IDX_DIM = 64
IDX_ROT = IDX_DIM // 4
TOPK_MAX = 256
ROPE_THETA = 500000.0
LN_EPS = 1e-5

LANES = 128
V7X_VMEM_BYTES = 64 * 1024 * 1024
VMEM_LIMIT = V7X_VMEM_BYTES - 8 * 1024 * 1024

TQ = 256
NEG_BIG = -1e30
EXP_ZERO_BELOW = -110.0
MAX_BISECT = 256


def _cparams(sem):
    return pltpu.CompilerParams(dimension_semantics=sem, vmem_limit_bytes=VMEM_LIMIT)


def _resident(shape):
    nd = len(shape)
    return pl.BlockSpec(shape, lambda *_: (0,) * nd, pipeline_mode=pl.Buffered(1))


def _proj_kernel(*refs, rope_half):
    if rope_half:
        x_ref, w_ref, c_ref, s1_ref, s2_ref, o_ref = refs
    else:
        x_ref, w_ref, o_ref = refs
    x = x_ref[...].astype(jnp.bfloat16)
    y = jnp.dot(x, w_ref[...], preferred_element_type=jnp.float32)
    if rope_half:
        c, s1, s2 = c_ref[...], s1_ref[...], s2_ref[...]
        slabs = []
        for j in range(y.shape[1] // LANES):
            yj = y[:, j * LANES:(j + 1) * LANES]
            up = pltpu.roll(yj, LANES - rope_half, axis=1)
            dn = pltpu.roll(yj, rope_half, axis=1)
            slabs.append(yj * c + up * s1 + dn * s2)
        y = jnp.concatenate(slabs, axis=1) if len(slabs) > 1 else slabs[0]
    o_ref[...] = y.astype(o_ref.dtype)


def _col_tile(n):
    for t in (640, 512, 384, 256, 128):
        if n % t == 0:
            return t
    raise ValueError(f"projection width {n} is not a multiple of {LANES}")


def _project(x, w, out_dtype, rope=None, tm=512):
    s, kdim = x.shape
    n = w.shape[1]
    tn = _col_tile(n)
    in_specs = [pl.BlockSpec((tm, kdim), lambda i, j: (i, 0)),
                pl.BlockSpec((kdim, tn), lambda i, j: (0, j))]
    args = [x, w]
    half = 0
    if rope is not None:
        half, c, s1, s2 = rope
        in_specs += [pl.BlockSpec((tm, LANES), lambda i, j: (i, 0))] * 3
        args += [c, s1, s2]
    return pl.pallas_call(
        functools.partial(_proj_kernel, rope_half=half),
        grid=(s // tm, n // tn),
        in_specs=in_specs,
        out_specs=pl.BlockSpec((tm, tn), lambda i, j: (i, j)),
        out_shape=jax.ShapeDtypeStruct((s, n), out_dtype),
        compiler_params=_cparams(("parallel", "arbitrary")),
        name="proj_rope" if half else "proj",
    )(*args)


def _rope_tables(positions, head_dim, rot_dim):
    half = rot_dim // 2
    inv = 1.0 / (ROPE_THETA ** (jnp.arange(half, dtype=jnp.float32) / half))
    ang = positions.astype(jnp.float32)[:, None] * inv
    cos, sin = jnp.cos(ang), jnp.sin(ang)
    s = positions.shape[0]
    ones = jnp.ones((s, head_dim - rot_dim), jnp.float32)
    zeros_h = jnp.zeros((s, half), jnp.float32)
    zeros_r = jnp.zeros((s, head_dim - rot_dim), jnp.float32)
    c = jnp.concatenate([cos, cos, ones], axis=1)
    s_up = jnp.concatenate([-sin, zeros_h, zeros_r], axis=1)
    s_dn = jnp.concatenate([zeros_h, sin, zeros_r], axis=1)
    reps = LANES // head_dim
    return half, jnp.tile(c, (1, reps)), jnp.tile(s_up, (1, reps)), jnp.tile(s_dn, (1, reps))


def _mixer_a_kernel(q_ref, iq_ref, iw_ref, gate_ref, ikt_ref, k_ref, v_ref, o_ref,
                    score_ref, wrep_ref, lo_ref, hi_ref, cnt_ref, mn_ref, mx_ref,
                    m_ref, l_ref, acc_ref, *, n_sel, idx_scale, sm_scale):
    tq = q_ref.shape[0]
    tk = tq
    groups = q_ref.shape[1] // HEAD_DIM // N_KV
    i = pl.program_id(0)
    nkb = i + 1
    f32 = jnp.float32

    row = lax.broadcasted_iota(jnp.int32, (tq, tk), 0)
    col = lax.broadcasted_iota(jnp.int32, (tq, tk), 1)

    for h in range(IDX_HEADS):
        wrep_ref[h] = jnp.broadcast_to(iw_ref[:, h:h + 1] * idx_scale, (tq, LANES))

    mn_ref[...] = jnp.full((tq, tk), jnp.inf, f32)
    mx_ref[...] = jnp.full((tq, tk), -jnp.inf, f32)

    def score_body(kb, carry):
        koff = pl.multiple_of(kb * tk, tk)
        ikt = ikt_ref[:, pl.ds(koff, tk)]
        acc = jnp.zeros((tq, tk), f32)
        for h in range(IDX_HEADS):
            d = jnp.dot(iq_ref[:, h * IDX_DIM:(h + 1) * IDX_DIM], ikt, preferred_element_type=f32)
            w = wrep_ref[h]
            acc = acc + jnp.concatenate([w] * (tk // LANES), axis=1) * jnp.maximum(d, 0.0)
        causal = (col + kb * tk) <= (row + i * tq)
        score_ref[:, pl.ds(koff, tk)] = jnp.where(causal, acc, -jnp.inf)
        mn_ref[...] = jnp.minimum(mn_ref[...], jnp.where(causal, acc, jnp.inf))
        mx_ref[...] = jnp.maximum(mx_ref[...], jnp.where(causal, acc, -jnp.inf))
        return carry

    lax.fori_loop(0, nkb, score_body, 0)

    t1 = (lax.broadcasted_iota(jnp.int32, (tq, LANES), 0) + i * tq + 1).astype(f32)
    k_t = jnp.minimum(t1, float(n_sel))
    lo_ref[...] = jnp.broadcast_to(jnp.min(mn_ref[...], axis=1, keepdims=True), (tq, LANES))
    hi_ref[...] = jnp.broadcast_to(jnp.max(mx_ref[...], axis=1, keepdims=True), (tq, LANES))
    cnt_ref[...] = t1

    def unresolved():
        lo, hi = lo_ref[...], hi_ref[...]
        mid = 0.5 * lo + 0.5 * hi
        open_ = jnp.where(cnt_ref[...] != k_t, 1.0, 0.0) * jnp.where(mid > lo, 1.0, 0.0) * jnp.where(mid < hi, 1.0, 0.0)
        return (jnp.max(open_) > 0.0).astype(jnp.int32)

    def bisect_body(state):
        it, _ = state
        for rh in range(tq // LANES):
            rows = slice(rh * LANES, (rh + 1) * LANES)
            lo, hi = lo_ref[rows, :], hi_ref[rows, :]
            mid = 0.5 * lo + 0.5 * hi

            def count_body(kb, acc):
                koff = pl.multiple_of(kb * tk, tk)
                for c in range(tk // LANES):
                    s = score_ref[rows, pl.ds(koff + c * LANES, LANES)]
                    acc = acc + jnp.where(s >= mid, 1.0, 0.0)
                return acc

            acc = lax.fori_loop(0, nkb, count_body, jnp.zeros((LANES, LANES), f32))
            cnt = jnp.broadcast_to(jnp.sum(acc, axis=1, keepdims=True), (LANES, LANES))
            ge = cnt >= k_t[rows, :]
            lo_ref[rows, :] = jnp.where(ge, mid, lo)
            hi_ref[rows, :] = jnp.where(ge, hi, mid)
            cnt_ref[rows, :] = jnp.where(ge, cnt, cnt_ref[rows, :])
        return it + 1, unresolved()

    lax.while_loop(lambda st: jnp.logical_and(st[1] > 0, st[0] < MAX_BISECT), bisect_body,
                   (jnp.int32(0), unresolved()))

    thr = jnp.concatenate([lo_ref[...]] * (tk // LANES), axis=1)
    m_rows = groups * tq
    for n in range(N_KV):
        qn = jnp.concatenate(
            [q_ref[:, (n * groups + g) * HEAD_DIM:(n * groups + g + 1) * HEAD_DIM] for g in range(groups)], axis=0)
        m_ref[...] = jnp.full((m_rows, LANES), NEG_BIG, f32)
        l_ref[...] = jnp.zeros((m_rows, LANES), f32)
        acc_ref[...] = jnp.zeros((m_rows, HEAD_DIM), f32)

        def attn_body(kb, carry):
            koff = pl.multiple_of(kb * tk, tk)
            kblk = k_ref[pl.ds(koff, tk), n * HEAD_DIM:(n + 1) * HEAD_DIM]
            vblk = v_ref[pl.ds(koff, tk), n * HEAD_DIM:(n + 1) * HEAD_DIM]
            z = lax.dot_general(qn, kblk, (((1,), (1,)), ((), ())), preferred_element_type=f32) * sm_scale
            sel1 = score_ref[:, pl.ds(koff, tk)] >= thr
            sel = jnp.concatenate([sel1] * groups, axis=0)
            m_old = m_ref[...]
            blk_max = jnp.max(jnp.where(sel, z, NEG_BIG), axis=1, keepdims=True)
            m_new = jnp.maximum(m_old, jnp.broadcast_to(blk_max, (m_rows, LANES)))
            p = jnp.where(sel, jnp.exp(z - jnp.concatenate([m_new] * (tk // LANES), axis=1)), 0.0)
            alpha = jnp.exp(m_old - m_new)
            l_ref[...] = alpha * l_ref[...] + jnp.broadcast_to(jnp.sum(p, axis=1, keepdims=True), (m_rows, LANES))
            acc_ref[...] = alpha * acc_ref[...] + jnp.dot(p.astype(jnp.bfloat16), vblk, preferred_element_type=f32)
            m_ref[...] = m_new
            return carry

        lax.fori_loop(0, nkb, attn_body, 0)
        out = acc_ref[...] / l_ref[...]
        for g in range(groups):
            cols = slice((n * groups + g) * HEAD_DIM, (n * groups + g + 1) * HEAD_DIM)
            gate = gate_ref[:, cols]
            o_ref[:, cols] = (out[g * tq:(g + 1) * tq, :] * (gate * jax.nn.sigmoid(gate))).astype(o_ref.dtype)


def _mixer_a(q, iq, iw, gate, ikt, k, v, n_sel):
    s, width = q.shape
    groups = width // HEAD_DIM // N_KV
    f32 = jnp.float32
    kern = functools.partial(_mixer_a_kernel, n_sel=n_sel,
                             idx_scale=(IDX_DIM ** -0.5) * (IDX_HEADS ** -0.5), sm_scale=HEAD_DIM ** -0.5)
    return pl.pallas_call(
        kern,
        grid=(s // TQ,),
        in_specs=[pl.BlockSpec((TQ, width), lambda i: (i, 0)),
                  pl.BlockSpec((TQ, IDX_HEADS * IDX_DIM), lambda i: (i, 0)),
                  pl.BlockSpec((TQ, LANES), lambda i: (i, 0)),
                  pl.BlockSpec((TQ, width), lambda i: (i, 0)),
                  _resident(ikt.shape), _resident(k.shape), _resident(v.shape)],
        out_specs=pl.BlockSpec((TQ, width), lambda i: (i, 0)),
        out_shape=jax.ShapeDtypeStruct((s, width), jnp.bfloat16),
        scratch_shapes=[pltpu.VMEM((TQ, s), f32),
                        pltpu.VMEM((IDX_HEADS, TQ, LANES), f32),
                        pltpu.VMEM((TQ, LANES), f32),
                        pltpu.VMEM((TQ, LANES), f32),
                        pltpu.VMEM((TQ, LANES), f32),
                        pltpu.VMEM((TQ, TQ), f32),
                        pltpu.VMEM((TQ, TQ), f32),
                        pltpu.VMEM((groups * TQ, LANES), f32),
                        pltpu.VMEM((groups * TQ, LANES), f32),
                        pltpu.VMEM((groups * TQ, HEAD_DIM), f32)],
        compiler_params=_cparams(("parallel",)),
        name="mixer_a",
    )(q, iq, iw, gate, ikt, k, v)


def _mixer_b_kernel(q_ref, gate_ref, k_ref, v_ref, o_ref, c_ref, acc_ref, *, sm_scale):
    tq = q_ref.shape[0]
    tk = tq
    groups = q_ref.shape[1] // HEAD_DIM // N_KV
    m_rows = groups * tq
    i = pl.program_id(0)
    f32 = jnp.float32

    row = lax.broadcasted_iota(jnp.int32, (tq, tk), 0)
    col = lax.broadcasted_iota(jnp.int32, (tq, tk), 1)
    later = jnp.where(row > col, 1.0, 0.0).astype(jnp.bfloat16)

    for n in range(N_KV):
        qn = jnp.concatenate(
            [q_ref[:, (n * groups + g) * HEAD_DIM:(n * groups + g + 1) * HEAD_DIM] for g in range(groups)], axis=0)
        c_ref[...] = jnp.zeros((m_rows, LANES), f32)
        acc_ref[...] = jnp.zeros((m_rows, HEAD_DIM), f32)

        def body(state):
            kb, _ = state
            koff = pl.multiple_of(kb * tk, tk)
            kblk = k_ref[pl.ds(koff, tk), n * HEAD_DIM:(n + 1) * HEAD_DIM]
            vblk = v_ref[pl.ds(koff, tk), n * HEAD_DIM:(n + 1) * HEAD_DIM]
            z = lax.dot_general(qn, kblk, (((1,), (1,)), ((), ())), preferred_element_type=f32) * sm_scale
            sp = jnp.maximum(z, 0.0) + jnp.log1p(jnp.exp(-jnp.abs(z)))
            strict1 = (col + kb * tk) < (row + i * tq)
            strict = jnp.concatenate([strict1] * groups, axis=0)
            log_om = jnp.where(strict, -sp, 0.0)
            l_hi = log_om.astype(jnp.bfloat16)
            l_lo = (log_om - l_hi.astype(f32)).astype(jnp.bfloat16)
            suffix = (jnp.dot(l_hi, later, preferred_element_type=f32)
                      + jnp.dot(l_lo, later, preferred_element_type=f32))
            c_old = c_ref[...]
            total = suffix + jnp.concatenate([c_old] * (tk // LANES), axis=1)
            a = jnp.where(strict, jnp.exp(z - sp + total), 0.0)
            acc_ref[...] += jnp.dot(a.astype(jnp.bfloat16), vblk, preferred_element_type=f32)
            c_new = c_old + jnp.broadcast_to(jnp.sum(log_om, axis=1, keepdims=True), (m_rows, LANES))
            c_ref[...] = c_new
            return kb - 1, (jnp.max(c_new) >= EXP_ZERO_BELOW).astype(jnp.int32)

        lax.while_loop(lambda st: jnp.logical_and(st[0] >= 0, st[1] > 0), body, (i, jnp.int32(1)))
        out = acc_ref[...]
        for g in range(groups):
            cols = slice((n * groups + g) * HEAD_DIM, (n * groups + g + 1) * HEAD_DIM)
            gate = gate_ref[:, cols]
            o_ref[:, cols] = (out[g * tq:(g + 1) * tq, :] * (gate * jax.nn.sigmoid(gate))).astype(o_ref.dtype)


def _mixer_b(q, gate, k, v):
    s, width = q.shape
    groups = width // HEAD_DIM // N_KV
    return pl.pallas_call(
        functools.partial(_mixer_b_kernel, sm_scale=HEAD_DIM ** -0.5),
        grid=(s // TQ,),
        in_specs=[pl.BlockSpec((TQ, width), lambda i: (i, 0)),
                  pl.BlockSpec((TQ, width), lambda i: (i, 0)),
                  _resident(k.shape), _resident(v.shape)],
        out_specs=pl.BlockSpec((TQ, width), lambda i: (i, 0)),
        out_shape=jax.ShapeDtypeStruct((s, width), jnp.bfloat16),
        scratch_shapes=[pltpu.VMEM((groups * TQ, LANES), jnp.float32),
                        pltpu.VMEM((groups * TQ, HEAD_DIM), jnp.float32)],
        compiler_params=_cparams(("parallel",)),
        name="mixer_b",
    )(q, gate, k, v)


def _post_kernel(og_ref, x_ref, p_ref, wo_ref, wg_ref, wp_ref, lng_ref, lnb_ref, o_ref, *, alpha):
    f32 = jnp.float32
    h = jnp.dot(og_ref[...], wo_ref[...], preferred_element_type=f32)
    y = alpha * x_ref[...] + h
    mu = jnp.mean(y, axis=-1, keepdims=True)
    d = y - mu
    var = jnp.mean(d * d, axis=-1, keepdims=True)
    yn = d * lax.rsqrt(var + LN_EPS) * lng_ref[...] + lnb_ref[...]
    gl = jnp.dot(yn.astype(jnp.bfloat16), wg_ref[...], preferred_element_type=f32)
    pe = jnp.dot(p_ref[...].astype(jnp.bfloat16), wp_ref[...], preferred_element_type=f32)
    o_ref[...] = yn + pe * jax.nn.sigmoid(gl)


def _post(og, x, p, wo, wg, wp, lng, lnb, alpha, tm=256):
    s, d = x.shape
    return pl.pallas_call(
        functools.partial(_post_kernel, alpha=alpha),
        grid=(s // tm,),
        in_specs=[pl.BlockSpec((tm, og.shape[1]), lambda i: (i, 0)),
                  pl.BlockSpec((tm, d), lambda i: (i, 0)),
                  pl.BlockSpec((tm, p.shape[1]), lambda i: (i, 0)),
                  _resident(wo.shape), _resident(wg.shape), _resident(wp.shape),
                  _resident(lng.shape), _resident(lnb.shape)],
        out_specs=pl.BlockSpec((tm, d), lambda i: (i, 0)),
        out_shape=jax.ShapeDtypeStruct((s, d), jnp.float32),
        compiler_params=_cparams(("parallel",)),
        name="post",
    )(og, x, p, wo, wg, wp, lng, lnb)


def _pad_cols(w, n):
    return jnp.pad(w, ((0, 0), (0, n - w.shape[1])))


def kernel(x, p, positions, w_in_a, w_o_a, w_kv_b, w_in_b, w_o_b, ln_g, ln_b, w_ple, w_ple_gate):
    bsz, s, d = x.shape
    n_a, n_b = w_in_a.shape[0], w_in_b.shape[0]
    depth = n_a + n_b
    alpha = (2.0 * depth) ** 0.25
    width = w_o_a.shape[1]
    kvw = N_KV * HEAD_DIM
    idxw = IDX_HEADS * IDX_DIM
    n_sel = min(TOPK_MAX, s // 4)
    bf16 = jnp.bfloat16

    outs = []
    for b in range(bsz):
        xb = x[b]
        rope_qk = _rope_tables(positions[b], HEAD_DIM, ROT_DIM)
        rope_ix = _rope_tables(positions[b], IDX_DIM, IDX_ROT)
        k_sh = v_sh = None
        for i in range(depth):
            if i < n_a:
                w = w_in_a[i].astype(bf16)
                o0 = 0
                w_qk = w[:, o0:o0 + width + kvw]; o0 += width + kvw
                w_v = w[:, o0:o0 + kvw]; o0 += kvw
                w_g = w[:, o0:o0 + width]; o0 += width
                w_ix = _pad_cols(w[:, o0:o0 + idxw + IDX_DIM], idxw + LANES); o0 += idxw + IDX_DIM
                w_iw = _pad_cols(w[:, o0:o0 + IDX_HEADS], LANES)
                qk = _project(xb, w_qk, bf16, rope=rope_qk)
                v = _project(xb, w_v, bf16)
                gate = _project(xb, w_g, jnp.float32)
                ix = _project(xb, w_ix, bf16, rope=rope_ix)
                iw = _project(xb, w_iw, jnp.float32)
                ikt = ix[:, idxw:idxw + IDX_DIM].T
                og = _mixer_a(qk[:, :width], ix[:, :idxw], iw, gate, ikt, qk[:, width:], v, n_sel)
                wo = w_o_a[i].astype(bf16)
            else:
                j = i - n_a
                if k_sh is None:
                    kv = _project(xb, w_kv_b.astype(bf16), bf16)
                    k_sh, v_sh = kv[:, :kvw], kv[:, kvw:]
                w = w_in_b[j].astype(bf16)
                q = _project(xb, w[:, :width], bf16)
                gate = _project(xb, w[:, width:], jnp.float32)
                og = _mixer_b(q, gate, k_sh, v_sh)
                wo = w_o_b[j].astype(bf16)
            xb = _post(og, xb, p[i, b], wo, w_ple_gate[i].astype(bf16), w_ple[i].astype(bf16),
                       ln_g[i][None, :], ln_b[i][None, :], alpha)
        outs.append(xb)
    return jnp.stack(outs, axis=0)
```

```python
import functools

import jax
import jax.numpy as jnp
from jax import lax
from jax.experimental import pallas as pl
from jax.experimental.pallas import tpu as pltpu

HEAD_DIM = 128
N_KV = 4
ROT_DIM = HEAD_DIM // 4
IDX_HEADS = 16
IDX_DIM = 64
IDX_ROT = IDX_DIM // 4
TOPK_MAX = 256
ROPE_THETA = 500000.0
LN_EPS = 1e-5
LOG2_E = 1.4426950408889634

LANES = 128
V7X_VMEM_BYTES = 64 * 1024 * 1024
VMEM_LIMIT = V7X_VMEM_BYTES - 8 * 1024 * 1024

TQ = 256
NEG_BIG = -1e30
EXP_ZERO_BELOW = -110.0
MAX_BISECT = 256


def _cparams(sem):
    return pltpu.CompilerParams(dimension_semantics=sem, vmem_limit_bytes=VMEM_LIMIT)


def _resident(shape):
    nd = len(shape)
    return pl.BlockSpec(shape, lambda *_: (0,) * nd, pipeline_mode=pl.Buffered(1))


def _proj_kernel(*refs, rope_half):
    if rope_half:
        x_ref, w_ref, c_ref, s1_ref, s2_ref, o_ref = refs
    else:
        x_ref, w_ref, o_ref = refs
    x = x_ref[...].astype(jnp.bfloat16)
    y = jnp.dot(x, w_ref[...], preferred_element_type=jnp.float32)
    if rope_half:
        c, s1, s2 = c_ref[...], s1_ref[...], s2_ref[...]
        slabs = []
        for j in range(y.shape[1] // LANES):
            yj = y[:, j * LANES:(j + 1) * LANES]
            up = pltpu.roll(yj, LANES - rope_half, axis=1)
            dn = pltpu.roll(yj, rope_half, axis=1)
            slabs.append(yj * c + up * s1 + dn * s2)
        y = jnp.concatenate(slabs, axis=1) if len(slabs) > 1 else slabs[0]
    o_ref[...] = y.astype(o_ref.dtype)


def _col_tile(n):
    for t in (640, 512, 384, 256, 128):
        if n % t == 0:
            return t
    raise ValueError(f"projection width {n} is not a multiple of {LANES}")


def _project(x, w, out_dtype, rope=None, tm=512):
    s, kdim = x.shape
    n = w.shape[1]
    tn = _col_tile(n)
    in_specs = [pl.BlockSpec((tm, kdim), lambda i, j: (i, 0)),
                pl.BlockSpec((kdim, tn), lambda i, j: (0, j))]
    args = [x, w]
    half = 0
    if rope is not None:
        half, c, s1, s2 = rope
        in_specs += [pl.BlockSpec((tm, LANES), lambda i, j: (i, 0))] * 3
        args += [c, s1, s2]
    return pl.pallas_call(
        functools.partial(_proj_kernel, rope_half=half),
        grid=(s // tm, n // tn),
        in_specs=in_specs,
        out_specs=pl.BlockSpec((tm, tn), lambda i, j: (i, j)),
        out_shape=jax.ShapeDtypeStruct((s, n), out_dtype),
        compiler_params=_cparams(("parallel", "arbitrary")),
        name="proj_rope" if half else "proj",
    )(*args)


def _rope_tables(positions, head_dim, rot_dim, scale=1.0):
    half = rot_dim // 2
    inv = 1.0 / (ROPE_THETA ** (jnp.arange(half, dtype=jnp.float32) / half))
    ang = positions.astype(jnp.float32)[:, None] * inv
    cos, sin = jnp.cos(ang), jnp.sin(ang)
    s = positions.shape[0]
    ones = jnp.ones((s, head_dim - rot_dim), jnp.float32)
    zeros_h = jnp.zeros((s, half), jnp.float32)
    zeros_r = jnp.zeros((s, head_dim - rot_dim), jnp.float32)
    c = jnp.concatenate([cos, cos, ones], axis=1)
    s_up = jnp.concatenate([-sin, zeros_h, zeros_r], axis=1)
    s_dn = jnp.concatenate([zeros_h, sin, zeros_r], axis=1)
    reps = LANES // head_dim
    return (half,) + tuple(jnp.tile(t * scale, (1, reps)) for t in (c, s_up, s_dn))


def _mixer_a_kernel(q_ref, iq_ref, iw_ref, gate_ref, ikt_ref, k_ref, v_ref, o_ref,
                    score_ref, wrep_ref, lo_ref, hi_ref, cnt_ref, mn_ref, mx_ref,
                    qn_ref, m_ref, acc_ref, *, n_sel, idx_scale):
    tq = q_ref.shape[0]
    tk = tq
    groups = q_ref.shape[1] // HEAD_DIM // N_KV
    i = pl.program_id(0)
    nkb = i + 1
    f32 = jnp.float32

    row = lax.broadcasted_iota(jnp.int32, (tq, tk), 0)
    col = lax.broadcasted_iota(jnp.int32, (tq, tk), 1)

    for h in range(IDX_HEADS):
        wrep_ref[h] = jnp.broadcast_to(iw_ref[:, h:h + 1] * idx_scale, (tq, LANES))

    mn_ref[...] = jnp.full((tq, tk), jnp.inf, f32)
    mx_ref[...] = jnp.full((tq, tk), -jnp.inf, f32)

    def score_body(kb, carry):
        koff = pl.multiple_of(kb * tk, tk)
        ikt = ikt_ref[:, pl.ds(koff, tk)]
        acc = jnp.zeros((tq, tk), f32)
        for h in range(IDX_HEADS):
            d = jnp.dot(iq_ref[:, h * IDX_DIM:(h + 1) * IDX_DIM], ikt, preferred_element_type=f32)
            w = wrep_ref[h]
            acc = acc + jnp.concatenate([w] * (tk // LANES), axis=1) * jnp.maximum(d, 0.0)
        causal = (col + kb * tk) <= (row + i * tq)
        score_ref[:, pl.ds(koff, tk)] = jnp.where(causal, acc, -jnp.inf)
        mn_ref[...] = jnp.minimum(mn_ref[...], jnp.where(causal, acc, jnp.inf))
        mx_ref[...] = jnp.maximum(mx_ref[...], jnp.where(causal, acc, -jnp.inf))
        return carry

    lax.fori_loop(0, nkb, score_body, 0)

    t1 = (lax.broadcasted_iota(jnp.int32, (tq, LANES), 0) + i * tq + 1).astype(f32)
    k_t = jnp.minimum(t1, float(n_sel))
    lo_ref[...] = jnp.broadcast_to(jnp.min(mn_ref[...], axis=1, keepdims=True), (tq, LANES))
    hi_ref[...] = jnp.broadcast_to(jnp.max(mx_ref[...], axis=1, keepdims=True), (tq, LANES))
    cnt_ref[...] = t1

    def unresolved():
        lo, hi = lo_ref[...], hi_ref[...]
        mid = 0.5 * lo + 0.5 * hi
        open_ = jnp.where(cnt_ref[...] != k_t, 1.0, 0.0) * jnp.where(mid > lo, 1.0, 0.0) * jnp.where(mid < hi, 1.0, 0.0)
        return (jnp.max(open_) > 0.0).astype(jnp.int32)

    def bisect_body(state):
        it, _ = state
        for rh in range(tq // LANES):
            rows = slice(rh * LANES, (rh + 1) * LANES)
            lo, hi = lo_ref[rows, :], hi_ref[rows, :]
            mid = 0.5 * lo + 0.5 * hi

            def count_body(kb, acc):
                koff = pl.multiple_of(kb * tk, tk)
                for c in range(tk // LANES):
                    s = score_ref[rows, pl.ds(koff + c * LANES, LANES)]
                    acc = acc + jnp.where(s >= mid, 1.0, 0.0)
                return acc

            acc = lax.fori_loop(0, nkb, count_body, jnp.zeros((LANES, LANES), f32))
            cnt = jnp.broadcast_to(jnp.sum(acc, axis=1, keepdims=True), (LANES, LANES))
            ge = cnt >= k_t[rows, :]
            lo_ref[rows, :] = jnp.where(ge, mid, lo)
            hi_ref[rows, :] = jnp.where(ge, hi, mid)
            cnt_ref[rows, :] = jnp.where(ge, cnt, cnt_ref[rows, :])
        return it + 1, unresolved()

    lax.while_loop(lambda st: jnp.logical_and(st[1] > 0, st[0] < MAX_BISECT), bisect_body,
                   (jnp.int32(0), unresolved()))

    thr = jnp.concatenate([lo_ref[...]] * (tk // LANES), axis=1)
    m_rows = groups * tq
    for n in range(N_KV):
        qn_ref[n] = jnp.concatenate(
            [q_ref[:, (n * groups + g) * HEAD_DIM:(n * groups + g + 1) * HEAD_DIM] for g in range(groups)], axis=0)
    m_ref[...] = jnp.full((N_KV, m_rows, LANES), NEG_BIG, f32)
    acc_ref[...] = jnp.zeros((N_KV, m_rows, 2 * HEAD_DIM), f32)

    def attn_body(kb, carry):
        koff = pl.multiple_of(kb * tk, tk)
        bias = jnp.where(score_ref[:, pl.ds(koff, tk)] >= thr, 0.0, NEG_BIG)
        bias = jnp.concatenate([bias] * groups, axis=0)
        for n in range(N_KV):
            kblk = k_ref[pl.ds(koff, tk), n * HEAD_DIM:(n + 1) * HEAD_DIM]
            vblk = v_ref[pl.ds(koff, tk), n * HEAD_DIM:(n + 1) * HEAD_DIM]
            vext = jnp.concatenate([vblk, jnp.ones_like(vblk)], axis=1)
            z = lax.dot_general(qn_ref[n], kblk, (((1,), (1,)), ((), ())), preferred_element_type=f32) + bias
            m_old = m_ref[n]
            m_new = jnp.maximum(m_old, jnp.max(z, axis=1, keepdims=True))
            p = jnp.exp2(z - jnp.concatenate([m_new] * (tk // LANES), axis=1))
            alpha = jnp.exp2(m_old - m_new)
            acc_ref[n] = (jnp.concatenate([alpha, alpha], axis=1) * acc_ref[n]
                          + jnp.dot(p.astype(jnp.bfloat16), vext, preferred_element_type=f32))
            m_ref[n] = m_new
        return carry

    lax.fori_loop(0, nkb, attn_body, 0)
    for n in range(N_KV):
        acc = acc_ref[n]
        out = acc[:, :HEAD_DIM] / acc[:, HEAD_DIM:]
        for g in range(groups):
            cols = slice((n * groups + g) * HEAD_DIM, (n * groups + g + 1) * HEAD_DIM)
            gate = gate_ref[:, cols]
            o_ref[:, cols] = (out[g * tq:(g + 1) * tq, :] * (gate * jax.nn.sigmoid(gate))).astype(o_ref.dtype)


def _mixer_a(q, iq, iw, gate, ikt, k, v, n_sel):
    s, width = q.shape
    groups = width // HEAD_DIM // N_KV
    f32 = jnp.float32
    kern = functools.partial(_mixer_a_kernel, n_sel=n_sel,
                             idx_scale=(IDX_DIM ** -0.5) * (IDX_HEADS ** -0.5))
    return pl.pallas_call(
        kern,
        grid=(s // TQ,),
        in_specs=[pl.BlockSpec((TQ, width), lambda i: (i, 0)),
                  pl.BlockSpec((TQ, IDX_HEADS * IDX_DIM), lambda i: (i, 0)),
                  pl.BlockSpec((TQ, LANES), lambda i: (i, 0)),
                  pl.BlockSpec((TQ, width), lambda i: (i, 0)),
                  _resident(ikt.shape), _resident(k.shape), _resident(v.shape)],
        out_specs=pl.BlockSpec((TQ, width), lambda i: (i, 0)),
        out_shape=jax.ShapeDtypeStruct((s, width), jnp.bfloat16),
        scratch_shapes=[pltpu.VMEM((TQ, s), f32),
                        pltpu.VMEM((IDX_HEADS, TQ, LANES), f32),
                        pltpu.VMEM((TQ, LANES), f32),
                        pltpu.VMEM((TQ, LANES), f32),
                        pltpu.VMEM((TQ, LANES), f32),
                        pltpu.VMEM((TQ, TQ), f32),
                        pltpu.VMEM((TQ, TQ), f32),
                        pltpu.VMEM((N_KV, groups * TQ, HEAD_DIM), jnp.bfloat16),
                        pltpu.VMEM((N_KV, groups * TQ, LANES), f32),
                        pltpu.VMEM((N_KV, groups * TQ, 2 * HEAD_DIM), f32)],
        compiler_params=_cparams(("parallel",)),
        name="mixer_a",
    )(q, iq, iw, gate, ikt, k, v)


def _mixer_b_kernel(q_ref, gate_ref, k_ref, v_ref, o_ref, c_ref, acc_ref, *, sm_scale):
    tq = q_ref.shape[0]
    tk = tq
    groups = q_ref.shape[1] // HEAD_DIM // N_KV
    m_rows = groups * tq
    i = pl.program_id(0)
    f32 = jnp.float32

    row = lax.broadcasted_iota(jnp.int32, (tq, tk), 0)
    col = lax.broadcasted_iota(jnp.int32, (tq, tk), 1)
    later = jnp.where(row > col, 1.0, 0.0).astype(jnp.bfloat16)

    for n in range(N_KV):
        qn = jnp.concatenate(
            [q_ref[:, (n * groups + g) * HEAD_DIM:(n * groups + g + 1) * HEAD_DIM] for g in range(groups)], axis=0)
        c_ref[...] = jnp.zeros((m_rows, LANES), f32)
        acc_ref[...] = jnp.zeros((m_rows, HEAD_DIM), f32)

        def body(state):
            kb, _ = state
            koff = pl.multiple_of(kb * tk, tk)
            kblk = k_ref[pl.ds(koff, tk), n * HEAD_DIM:(n + 1) * HEAD_DIM]
            vblk = v_ref[pl.ds(koff, tk), n * HEAD_DIM:(n + 1) * HEAD_DIM]
            z = lax.dot_general(qn, kblk, (((1,), (1,)), ((), ())), preferred_element_type=f32) * sm_scale
            sp = jnp.maximum(z, 0.0) + jnp.log1p(jnp.exp(-jnp.abs(z)))
            strict1 = (col + kb * tk) < (row + i * tq)
            strict = jnp.concatenate([strict1] * groups, axis=0)
            log_om = jnp.where(strict, -sp, 0.0)
            l_hi = log_om.astype(jnp.bfloat16)
            l_lo = (log_om - l_hi.astype(f32)).astype(jnp.bfloat16)
            suffix = (jnp.dot(l_hi, later, preferred_element_type=f32)
                      + jnp.dot(l_lo, later, preferred_element_type=f32))
            c_old = c_ref[...]
            total = suffix + jnp.concatenate([c_old] * (tk // LANES), axis=1)
            a = jnp.where(strict, jnp.exp(z - sp + total), 0.0)
            acc_ref[...] += jnp.dot(a.astype(jnp.bfloat16), vblk, preferred_element_type=f32)
            c_new = c_old + jnp.broadcast_to(jnp.sum(log_om, axis=1, keepdims=True), (m_rows, LANES))
            c_ref[...] = c_new
            return kb - 1, (jnp.max(c_new) >= EXP_ZERO_BELOW).astype(jnp.int32)

        lax.while_loop(lambda st: jnp.logical_and(st[0] >= 0, st[1] > 0), body, (i, jnp.int32(1)))
        out = acc_ref[...]
        for g in range(groups):
            cols = slice((n * groups + g) * HEAD_DIM, (n * groups + g + 1) * HEAD_DIM)
            gate = gate_ref[:, cols]
            o_ref[:, cols] = (out[g * tq:(g + 1) * tq, :] * (gate * jax.nn.sigmoid(gate))).astype(o_ref.dtype)


def _mixer_b(q, gate, kv):
    s, width = q.shape
    groups = width // HEAD_DIM // N_KV
    kvw = kv.shape[1] // 2
    half_spec = lambda c: pl.BlockSpec((s, kvw), lambda i: (0, c), pipeline_mode=pl.Buffered(1))
    return pl.pallas_call(
        functools.partial(_mixer_b_kernel, sm_scale=HEAD_DIM ** -0.5),
        grid=(s // TQ,),
        in_specs=[pl.BlockSpec((TQ, width), lambda i: (i, 0)),
                  pl.BlockSpec((TQ, width), lambda i: (i, 0)),
                  half_spec(0), half_spec(1)],
        out_specs=pl.BlockSpec((TQ, width), lambda i: (i, 0)),
        out_shape=jax.ShapeDtypeStruct((s, width), jnp.bfloat16),
        scratch_shapes=[pltpu.VMEM((groups * TQ, LANES), jnp.float32),
                        pltpu.VMEM((groups * TQ, HEAD_DIM), jnp.float32)],
        compiler_params=_cparams(("parallel",)),
        name="mixer_b",
    )(q, gate, kv, kv)


def _post_kernel(og_ref, x_ref, p_ref, wo_ref, wg_ref, wp_ref, lng_ref, lnb_ref, o_ref, *, alpha):
    f32 = jnp.float32
    h = jnp.dot(og_ref[...], wo_ref[...], preferred_element_type=f32)
    y = alpha * x_ref[...] + h
    mu = jnp.mean(y, axis=-1, keepdims=True)
    d = y - mu
    var = jnp.mean(d * d, axis=-1, keepdims=True)
    yn = d * lax.rsqrt(var + LN_EPS) * lng_ref[...] + lnb_ref[...]
    gl = jnp.dot(yn.astype(jnp.bfloat16), wg_ref[...], preferred_element_type=f32)
    pe = jnp.dot(p_ref[...].astype(jnp.bfloat16), wp_ref[...], preferred_element_type=f32)
    o_ref[...] = yn + pe * jax.nn.sigmoid(gl)


def _post(og, x, p, wo, wg, wp, lng, lnb, alpha, tm=256):
    s, d = x.shape
    return pl.pallas_call(
        functools.partial(_post_kernel, alpha=alpha),
        grid=(s // tm,),
        in_specs=[pl.BlockSpec((tm, og.shape[1]), lambda i: (i, 0)),
                  pl.BlockSpec((tm, d), lambda i: (i, 0)),
                  pl.BlockSpec((tm, p.shape[1]), lambda i: (i, 0)),
                  _resident(wo.shape), _resident(wg.shape), _resident(wp.shape),
                  _resident(lng.shape), _resident(lnb.shape)],
        out_specs=pl.BlockSpec((tm, d), lambda i: (i, 0)),
        out_shape=jax.ShapeDtypeStruct((s, d), jnp.float32),
        compiler_params=_cparams(("parallel",)),
        name="post",
    )(og, x, p, wo, wg, wp, lng, lnb)


def _pad_cols(w, n):
    return jnp.pad(w, ((0, 0), (0, n - w.shape[1])))


def kernel(x, p, positions, w_in_a, w_o_a, w_kv_b, w_in_b, w_o_b, ln_g, ln_b, w_ple, w_ple_gate):
    bsz, s, d = x.shape
    n_a, n_b = w_in_a.shape[0], w_in_b.shape[0]
    depth = n_a + n_b
    alpha = (2.0 * depth) ** 0.25
    width = w_o_a.shape[1]
    kvw = N_KV * HEAD_DIM
    idxw = IDX_HEADS * IDX_DIM
    n_sel = min(TOPK_MAX, s // 4)
    bf16 = jnp.bfloat16

    outs = []
    for b in range(bsz):
        xb = x[b]
        rope_k = _rope_tables(positions[b], HEAD_DIM, ROT_DIM)
        rope_q = _rope_tables(positions[b], HEAD_DIM, ROT_DIM, scale=LOG2_E * HEAD_DIM ** -0.5)
        rope_ix = _rope_tables(positions[b], IDX_DIM, IDX_ROT)
        k_sh = None
        for i in range(depth):
            if i < n_a:
                w = w_in_a[i].astype(bf16)
                o0 = 0
                w_q = w[:, o0:o0 + width]; o0 += width
                w_k = w[:, o0:o0 + kvw]; o0 += kvw
                w_v = w[:, o0:o0 + kvw]; o0 += kvw
                w_g = w[:, o0:o0 + width]; o0 += width
                w_ix = _pad_cols(w[:, o0:o0 + idxw + IDX_DIM], idxw + LANES); o0 += idxw + IDX_DIM
                w_iw = _pad_cols(w[:, o0:o0 + IDX_HEADS], LANES)
                q = _project(xb, w_q, bf16, rope=rope_q)
                k = _project(xb, w_k, bf16, rope=rope_k)
                v = _project(xb, w_v, bf16)
                gate = _project(xb, w_g, jnp.float32)
                ix = _project(xb, w_ix, bf16, rope=rope_ix)
                iw = _project(xb, w_iw, jnp.float32)
                ikt = ix[:, idxw:idxw + IDX_DIM].T
                og = _mixer_a(q, ix, iw, gate, ikt, k, v, n_sel)
                wo = w_o_a[i].astype(bf16)
            else:
                j = i - n_a
                if k_sh is None:
                    k_sh = _project(xb, w_kv_b.astype(bf16), bf16)
                w = w_in_b[j].astype(bf16)
                q = _project(xb, w[:, :width], bf16)
                gate = _project(xb, w[:, width:], jnp.float32)
                og = _mixer_b(q, gate, k_sh)
                wo = w_o_b[j].astype(bf16)
            xb = _post(og, xb, p[i, b], wo, w_ple_gate[i].astype(bf16), w_ple[i].astype(bf16),
                       ln_g[i][None, :], ln_b[i][None, :], alpha)
        outs.append(xb)
    return jnp.stack(outs, axis=0)
```

```python
import functools
from typing import NamedTuple

import jax
import jax.numpy as jnp
from jax import lax
from jax.experimental import pallas as pl
from jax.experimental.pallas import tpu as pltpu

HEAD_DIM = 128
N_KV = 4
ROT_DIM = HEAD_DIM // 4
IDX_HEADS = 16
IDX_DIM = 64
IDX_ROT = IDX_DIM // 4
TOPK_MAX = 256
ROPE_THETA = 500000.0
LN_EPS = 1e-5
LOG2_E = 1.4426950408889634

LANES = 128
V7X_VMEM_BYTES = 64 * 1024 * 1024
VMEM_LIMIT = V7X_VMEM_BYTES - 8 * 1024 * 1024

TQ = 256
TK2 = 2 * TQ
TM_PROJ = 256
SEG_MAX = 1024
NEG_BIG = -1e30
EXP_ZERO_ABOVE = 110.0
MAX_BISECT = 256


def _cparams(sem):
    return pltpu.CompilerParams(dimension_semantics=sem, vmem_limit_bytes=VMEM_LIMIT)


def _resident(shape):
    nd = len(shape)
    return pl.BlockSpec(shape, lambda *_: (0,) * nd, pipeline_mode=pl.Buffered(1))


class Seg(NamedTuple):
    col: int
    width: int
    out: int
    out_col: int
    rope: int
    scale: float


def _rope_slab(y, half, c, s_up, s_dn):
    up = pltpu.roll(y, LANES - half, axis=1)
    dn = pltpu.roll(y, half, axis=1)
    return y * c + up * s_up + dn * s_dn


def _fused_proj_kernel(x_ref, w_ref, *refs, segs, rope_halves):
    n_tab = 3 * len(rope_halves)
    tabs, outs = refs[:n_tab], refs[n_tab:]
    x = x_ref[...].astype(jnp.bfloat16)
    for sg in segs:
        y = jnp.dot(x, w_ref[:, sg.col:sg.col + sg.width], preferred_element_type=jnp.float32)
        if sg.scale != 1.0:
            y = y * sg.scale
        if sg.rope >= 0:
            c, s_up, s_dn = (t[...] for t in tabs[3 * sg.rope:3 * sg.rope + 3])
            slabs = [_rope_slab(y[:, j * LANES:(j + 1) * LANES], rope_halves[sg.rope], c, s_up, s_dn)
                     for j in range(sg.width // LANES)]
            y = jnp.concatenate(slabs, axis=1) if len(slabs) > 1 else slabs[0]
        o_ref = outs[sg.out]
        o_ref[:, sg.out_col:sg.out_col + sg.width] = y.astype(o_ref.dtype)


def _fused_project(x, w, plan, out_dtypes, ropes):
    s, kdim = x.shape
    segs, col, out_cols = [], 0, [0] * len(out_dtypes)
    for width, out, rope, scale in plan:
        for c0 in range(0, width, SEG_MAX):
            wd = min(SEG_MAX, width - c0)
            segs.append(Seg(col + c0, wd, out, out_cols[out] + c0, rope, scale))
        col += width
        out_cols[out] += width
    assert col == w.shape[1]
    tm = TM_PROJ
    tabs = [t for r in ropes for t in r[1:]]
    return pl.pallas_call(
        functools.partial(_fused_proj_kernel, segs=tuple(segs), rope_halves=tuple(r[0] for r in ropes)),
        grid=(s // tm,),
        in_specs=[pl.BlockSpec((tm, kdim), lambda i: (i, 0)), _resident(w.shape)]
                 + [pl.BlockSpec((tm, LANES), lambda i: (i, 0))] * len(tabs),
        out_specs=[pl.BlockSpec((tm, n), lambda i: (i, 0)) for n in out_cols],
        out_shape=[jax.ShapeDtypeStruct((s, n), dt) for n, dt in zip(out_cols, out_dtypes)],
        compiler_params=_cparams(("parallel",)),
        name="fused_proj",
    )(x, w, *tabs)


def _rope_tables(positions, head_dim, rot_dim):
    half = rot_dim // 2
    inv = 1.0 / (ROPE_THETA ** (jnp.arange(half, dtype=jnp.float32) / half))
    ang = positions.astype(jnp.float32)[:, None] * inv
    cos, sin = jnp.cos(ang), jnp.sin(ang)
    s = positions.shape[0]
    ones = jnp.ones((s, head_dim - rot_dim), jnp.float32)
    zeros_h = jnp.zeros((s, half), jnp.float32)
    zeros_r = jnp.zeros((s, head_dim - rot_dim), jnp.float32)
    c = jnp.concatenate([cos, cos, ones], axis=1)
    s_up = jnp.concatenate([-sin, zeros_h, zeros_r], axis=1)
    s_dn = jnp.concatenate([zeros_h, sin, zeros_r], axis=1)
    reps = LANES // head_dim
    return (half,) + tuple(jnp.tile(t, (1, reps)) for t in (c, s_up, s_dn))


def _stack_heads(q_ref, n, groups):
    return jnp.concatenate(
        [q_ref[:, (n * groups + g) * HEAD_DIM:(n * groups + g + 1) * HEAD_DIM] for g in range(groups)], axis=0)


def _store_gated(o_ref, gate_ref, out, n, groups):
    tq = o_ref.shape[0]
    for g in range(groups):
        cols = slice((n * groups + g) * HEAD_DIM, (n * groups + g + 1) * HEAD_DIM)
        gate = gate_ref[:, cols]
        o_ref[:, cols] = (out[g * tq:(g + 1) * tq, :] * (gate * jax.nn.sigmoid(gate))).astype(o_ref.dtype)


def _mixer_a_kernel(q_ref, iq_ref, iw_ref, gate_ref, ikt_ref, k_ref, v_ref, o_ref,
                    score_ref, wrep_ref, lo_ref, hi_ref, cnt_ref, mn_ref, mx_ref,
                    qn_ref, m_ref, acc_ref, *, n_sel, idx_scale):
    tq = q_ref.shape[0]
    tk = tq
    groups = q_ref.shape[1] // HEAD_DIM // N_KV
    i = pl.program_id(0)
    nkb = i + 1
    nkb2 = (i + 2) // 2
    f32 = jnp.float32

    row = lax.broadcasted_iota(jnp.int32, (tq, tk), 0)
    col = lax.broadcasted_iota(jnp.int32, (tq, tk), 1)

    for h in range(IDX_HEADS):
        wrep_ref[h] = jnp.broadcast_to(iw_ref[:, h:h + 1] * idx_scale, (tq, LANES))

    mn_ref[...] = jnp.full((tq, tk), jnp.inf, f32)
    mx_ref[...] = jnp.full((tq, tk), -jnp.inf, f32)

    def score_body(kb, carry):
        koff = pl.multiple_of(kb * tk, tk)
        ikt = ikt_ref[:, pl.ds(koff, tk)]
        acc = jnp.zeros((tq, tk), f32)
        for h in range(IDX_HEADS):
            d = jnp.dot(iq_ref[:, h * IDX_DIM:(h + 1) * IDX_DIM], ikt, preferred_element_type=f32)
            w = wrep_ref[h]
            acc = acc + jnp.concatenate([w] * (tk // LANES), axis=1) * jnp.maximum(d, 0.0)
        causal = (col + kb * tk) <= (row + i * tq)
        score_ref[:, pl.ds(koff, tk)] = jnp.where(causal, acc, -jnp.inf)
        mn_ref[...] = jnp.minimum(mn_ref[...], jnp.where(causal, acc, jnp.inf))
        mx_ref[...] = jnp.maximum(mx_ref[...], jnp.where(causal, acc, -jnp.inf))
        return carry

    lax.fori_loop(0, nkb, score_body, 0)
    score_ref[:, pl.ds(pl.multiple_of(nkb * tk, tk), tk)] = jnp.full((tq, tk), -jnp.inf, f32)

    t1 = (lax.broadcasted_iota(jnp.int32, (tq, LANES), 0) + i * tq + 1).astype(f32)
    k_t = jnp.minimum(t1, float(n_sel))
    lo_ref[...] = jnp.broadcast_to(jnp.min(mn_ref[...], axis=1, keepdims=True), (tq, LANES))
    hi_ref[...] = jnp.broadcast_to(jnp.max(mx_ref[...], axis=1, keepdims=True), (tq, LANES))
    cnt_ref[...] = t1

    def unresolved():
        lo, hi = lo_ref[...], hi_ref[...]
        mid = 0.5 * lo + 0.5 * hi
        open_ = jnp.where(cnt_ref[...] != k_t, 1.0, 0.0) * jnp.where(mid > lo, 1.0, 0.0) * jnp.where(mid < hi, 1.0, 0.0)
        return (jnp.max(open_) > 0.0).astype(jnp.int32)

    def bisect_body(state):
        it, _ = state
        for rh in range(tq // LANES):
            rows = slice(rh * LANES, (rh + 1) * LANES)
            lo, hi = lo_ref[rows, :], hi_ref[rows, :]
            mid = 0.5 * lo + 0.5 * hi

            def count_body(kb2, acc):
                koff = pl.multiple_of(kb2 * TK2, TK2)
                for c in range(TK2 // LANES):
                    s = score_ref[rows, pl.ds(koff + c * LANES, LANES)]
                    acc = acc + jnp.where(s >= mid, 1.0, 0.0)
                return acc

            acc = lax.fori_loop(0, nkb2, count_body, jnp.zeros((LANES, LANES), f32))
            cnt = jnp.broadcast_to(jnp.sum(acc, axis=1, keepdims=True), (LANES, LANES))
            ge = cnt >= k_t[rows, :]
            lo_ref[rows, :] = jnp.where(ge, mid, lo)
            hi_ref[rows, :] = jnp.where(ge, hi, mid)
            cnt_ref[rows, :] = jnp.where(ge, cnt, cnt_ref[rows, :])
        return it + 1, unresolved()

    lax.while_loop(lambda st: jnp.logical_and(st[1] > 0, st[0] < MAX_BISECT), bisect_body,
                   (jnp.int32(0), unresolved()))

    thr = jnp.concatenate([lo_ref[...]] * (TK2 // LANES), axis=1)
    m_rows = groups * tq
    for n in range(N_KV):
        qn_ref[n] = _stack_heads(q_ref, n, groups)
    m_ref[...] = jnp.full((N_KV, m_rows, LANES), NEG_BIG, f32)
    acc_ref[...] = jnp.zeros((N_KV, m_rows, 2 * HEAD_DIM), f32)

    def attn_body(kb2, carry):
        koff = pl.multiple_of(kb2 * TK2, TK2)
        bias = jnp.where(score_ref[:, pl.ds(koff, TK2)] >= thr, 0.0, NEG_BIG)
        bias = jnp.concatenate([bias] * groups, axis=0)
        for n in range(N_KV):
            kblk = k_ref[pl.ds(koff, TK2), n * HEAD_DIM:(n + 1) * HEAD_DIM]
            vblk = v_ref[pl.ds(koff, TK2), n * HEAD_DIM:(n + 1) * HEAD_DIM]
            vext = jnp.concatenate([vblk, jnp.ones_like(vblk)], axis=1)
            z = lax.dot_general(qn_ref[n], kblk, (((1,), (1,)), ((), ())), preferred_element_type=f32) + bias
            m_old = m_ref[n]
            m_new = jnp.maximum(m_old, jnp.max(z, axis=1, keepdims=True))
            p = jnp.exp2(z - jnp.concatenate([m_new] * (TK2 // LANES), axis=1))
            alpha = jnp.exp2(m_old - m_new)
            acc_ref[n] = (jnp.concatenate([alpha, alpha], axis=1) * acc_ref[n]
                          + jnp.dot(p.astype(jnp.bfloat16), vext, preferred_element_type=f32))
            m_ref[n] = m_new
        return carry

    lax.fori_loop(0, nkb2, attn_body, 0)
    for n in range(N_KV):
        acc = acc_ref[n]
        _store_gated(o_ref, gate_ref, acc[:, :HEAD_DIM] / acc[:, HEAD_DIM:], n, groups)


def _mixer_a(q, ix, iw, gate, ikt, k, v, n_sel):
    s, width = q.shape
    groups = width // HEAD_DIM // N_KV
    f32 = jnp.float32
    kern = functools.partial(_mixer_a_kernel, n_sel=n_sel, idx_scale=(IDX_DIM ** -0.5) * (IDX_HEADS ** -0.5))
    return pl.pallas_call(
        kern,
        grid=(s // TQ,),
        in_specs=[pl.BlockSpec((TQ, width), lambda i: (i, 0)),
                  pl.BlockSpec((TQ, IDX_HEADS * IDX_DIM), lambda i: (i, 0)),
                  pl.BlockSpec((TQ, LANES), lambda i: (i, 0)),
                  pl.BlockSpec((TQ, width), lambda i: (i, 0)),
                  _resident(ikt.shape), _resident(k.shape), _resident(v.shape)],
        out_specs=pl.BlockSpec((TQ, width), lambda i: (i, 0)),
        out_shape=jax.ShapeDtypeStruct((s, width), jnp.bfloat16),
        scratch_shapes=[pltpu.VMEM((TQ, s + TQ), f32),
                        pltpu.VMEM((IDX_HEADS, TQ, LANES), f32),
                        pltpu.VMEM((TQ, LANES), f32),
                        pltpu.VMEM((TQ, LANES), f32),
                        pltpu.VMEM((TQ, LANES), f32),
                        pltpu.VMEM((TQ, TQ), f32),
                        pltpu.VMEM((TQ, TQ), f32),
                        pltpu.VMEM((N_KV, groups * TQ, HEAD_DIM), jnp.bfloat16),
                        pltpu.VMEM((N_KV, groups * TQ, LANES), f32),
                        pltpu.VMEM((N_KV, groups * TQ, 2 * HEAD_DIM), f32)],
        compiler_params=_cparams(("parallel",)),
        name="mixer_a",
    )(q, ix, iw, gate, ikt, k, v)


def _mixer_b_kernel(q_ref, gate_ref, k_ref, v_ref, o_ref, c_ref, acc_ref):
    tq = q_ref.shape[0]
    tk = tq
    groups = q_ref.shape[1] // HEAD_DIM // N_KV
    m_rows = groups * tq
    i = pl.program_id(0)
    f32, bf16 = jnp.float32, jnp.bfloat16

    row = lax.broadcasted_iota(jnp.int32, (tq, tk), 0)
    col = lax.broadcasted_iota(jnp.int32, (tq, tk), 1)
    later = jnp.where(row > col, 1.0, 0.0).astype(bf16)
    later2 = jnp.concatenate([later, later], axis=0)
    strict = jnp.concatenate([col < row] * groups, axis=0)

    for n in range(N_KV):
        qn = _stack_heads(q_ref, n, groups)

        def step(kb, diagonal):
            koff = pl.multiple_of(kb * tk, tk)
            kblk = k_ref[pl.ds(koff, tk), n * HEAD_DIM:(n + 1) * HEAD_DIM]
            vblk = v_ref[pl.ds(koff, tk), n * HEAD_DIM:(n + 1) * HEAD_DIM]
            z = lax.dot_general(qn, kblk, (((1,), (1,)), ((), ())), preferred_element_type=f32)
            t = jnp.log(1.0 + jnp.exp2(jnp.abs(z) * (-LOG2_E)))
            sp = jnp.maximum(z, 0.0) + t
            if diagonal:
                sp = jnp.where(strict, sp, 0.0)
            sp_hi = sp.astype(bf16)
            sp_lo = (sp - sp_hi.astype(f32)).astype(bf16)
            suffix = jnp.dot(jnp.concatenate([sp_hi, sp_lo], axis=1), later2, preferred_element_type=f32)
            c_old = c_ref[...]
            total = suffix + jnp.concatenate([c_old] * (tk // LANES), axis=1)
            a = jnp.exp((jnp.minimum(z, 0.0) - t) - total)
            if diagonal:
                a = jnp.where(strict, a, 0.0)
            acc_ref[...] += jnp.dot(a.astype(bf16), vblk, preferred_element_type=f32)
            c_new = c_old + jnp.broadcast_to(jnp.sum(sp, axis=1, keepdims=True), (m_rows, LANES))
            c_ref[...] = c_new
            return (jnp.min(c_new) <= EXP_ZERO_ABOVE).astype(jnp.int32)

        c_ref[...] = jnp.zeros((m_rows, LANES), f32)
        acc_ref[...] = jnp.zeros((m_rows, HEAD_DIM), f32)
        go = step(i, True)
        lax.while_loop(lambda st: jnp.logical_and(st[0] >= 0, st[1] > 0),
                       lambda st: (st[0] - 1, step(st[0], False)), (i - 1, go))
        _store_gated(o_ref, gate_ref, acc_ref[...], n, groups)


def _mixer_b(q, gate, kv):
    s, width = q.shape
    groups = width // HEAD_DIM // N_KV
    kvw = kv.shape[1] // 2
    half_spec = lambda c: pl.BlockSpec((s, kvw), lambda i: (0, c), pipeline_mode=pl.Buffered(1))
    return pl.pallas_call(
        _mixer_b_kernel,
        grid=(s // TQ,),
        in_specs=[pl.BlockSpec((TQ, width), lambda i: (i, 0)),
                  pl.BlockSpec((TQ, width), lambda i: (i, 0)),
                  half_spec(0), half_spec(1)],
        out_specs=pl.BlockSpec((TQ, width), lambda i: (i, 0)),
        out_shape=jax.ShapeDtypeStruct((s, width), jnp.bfloat16),
        scratch_shapes=[pltpu.VMEM((groups * TQ, LANES), jnp.float32),
                        pltpu.VMEM((groups * TQ, HEAD_DIM), jnp.float32)],
        compiler_params=_cparams(("parallel",)),
        name="mixer_b",
    )(q, gate, kv, kv)


def _post_kernel(og_ref, x_ref, p_ref, wo_ref, wg_ref, wp_ref, lng_ref, lnb_ref, o_ref, *, alpha):
    f32 = jnp.float32
    h = jnp.dot(og_ref[...], wo_ref[...], preferred_element_type=f32)
    y = alpha * x_ref[...] + h
    mu = jnp.mean(y, axis=-1, keepdims=True)
    d = y - mu
    var = jnp.mean(d * d, axis=-1, keepdims=True)
    yn = d * lax.rsqrt(var + LN_EPS) * lng_ref[...] + lnb_ref[...]
    gl = jnp.dot(yn.astype(jnp.bfloat16), wg_ref[...], preferred_element_type=f32)
    pe = jnp.dot(p_ref[...].astype(jnp.bfloat16), wp_ref[...], preferred_element_type=f32)
    o_ref[...] = yn + pe * jax.nn.sigmoid(gl)


def _post(og, x, p, wo, wg, wp, lng, lnb, alpha):
    s, d = x.shape
    tm = TM_PROJ
    return pl.pallas_call(
        functools.partial(_post_kernel, alpha=alpha),
        grid=(s // tm,),
        in_specs=[pl.BlockSpec((tm, og.shape[1]), lambda i: (i, 0)),
                  pl.BlockSpec((tm, d), lambda i: (i, 0)),
                  pl.BlockSpec((tm, p.shape[1]), lambda i: (i, 0)),
                  _resident(wo.shape), _resident(wg.shape), _resident(wp.shape),
                  _resident(lng.shape), _resident(lnb.shape)],
        out_specs=pl.BlockSpec((tm, d), lambda i: (i, 0)),
        out_shape=jax.ShapeDtypeStruct((s, d), jnp.float32),
        compiler_params=_cparams(("parallel",)),
        name="post",
    )(og, x, p, wo, wg, wp, lng, lnb)


def _pad_cols(w, n):
    return jnp.pad(w, ((0, 0), (0, n - w.shape[1])))


def kernel(x, p, positions, w_in_a, w_o_a, w_kv_b, w_in_b, w_o_b, ln_g, ln_b, w_ple, w_ple_gate):
    bsz, s, d = x.shape
    n_a, n_b = w_in_a.shape[0], w_in_b.shape[0]
    depth = n_a + n_b
    alpha = (2.0 * depth) ** 0.25
    width = w_o_a.shape[1]
    kvw = N_KV * HEAD_DIM
    idxw = IDX_HEADS * IDX_DIM
    n_sel = min(TOPK_MAX, s // 4)
    bf16, f32 = jnp.bfloat16, jnp.float32
    sm_scale = HEAD_DIM ** -0.5

    outs = []
    for b in range(bsz):
        xb = x[b]
        ropes = [_rope_tables(positions[b], HEAD_DIM, ROT_DIM), _rope_tables(positions[b], IDX_DIM, IDX_ROT)]
        kv_shared = None
        for i in range(depth):
            if i < n_a:
                main = width + 2 * kvw + width + idxw
                w = jnp.concatenate([w_in_a[i][:, :main], _pad_cols(w_in_a[i][:, main:main + IDX_DIM], LANES),
                                     _pad_cols(w_in_a[i][:, main + IDX_DIM:], LANES)], axis=1).astype(bf16)
                plan = [(width, 0, 0, LOG2_E * sm_scale), (kvw, 1, 0, 1.0), (kvw, 2, -1, 1.0), (width, 3, -1, 1.0),
                        (idxw + LANES, 4, 1, 1.0), (LANES, 5, -1, 1.0)]
                q, k, v, gate, ix, iw = _fused_project(xb, w, plan, [bf16, bf16, bf16, f32, bf16, f32], ropes)
                ikt = ix[:, idxw:idxw + IDX_DIM].T
                og = _mixer_a(q, ix, iw, gate, ikt, k, v, n_sel)
                wo = w_o_a[i].astype(bf16)
            else:
                j = i - n_a
                if kv_shared is None:
                    w = jnp.concatenate([w_kv_b, w_in_b[j]], axis=1).astype(bf16)
                    plan = [(2 * kvw, 0, -1, 1.0), (width, 1, -1, sm_scale), (width, 2, -1, 1.0)]
                    kv_shared, q, gate = _fused_project(xb, w, plan, [bf16, bf16, f32], [])
                else:
                    plan = [(width, 0, -1, sm_scale), (width, 1, -1, 1.0)]
                    q, gate = _fused_project(xb, w_in_b[j].astype(bf16), plan, [bf16, f32], [])
                og = _mixer_b(q, gate, kv_shared)
                wo = w_o_b[j].astype(bf16)
            xb = _post(og, xb, p[i, b], wo, w_ple_gate[i].astype(bf16), w_ple[i].astype(bf16),
                       ln_g[i][None, :], ln_b[i][None, :], alpha)
        outs.append(xb)
    return jnp.stack(outs, axis=0)
```

```python
import functools
from typing import NamedTuple

import jax
import jax.numpy as jnp
from jax import lax
from jax.experimental import pallas as pl
from jax.experimental.pallas import tpu as pltpu

HEAD_DIM = 128
N_KV = 4
ROT_DIM = HEAD_DIM // 4
IDX_HEADS = 16
IDX_DIM = 64
IDX_ROT = IDX_DIM // 4
TOPK_MAX = 256
ROPE_THETA = 500000.0
LN_EPS = 1e-5
LOG2_E = 1.4426950408889634

LANES = 128
V7X_VMEM_BYTES = 64 * 1024 * 1024
VMEM_LIMIT = V7X_VMEM_BYTES - 8 * 1024 * 1024

TQ = 256
TK2 = 2 * TQ
TM_PROJ = 256
SEG_MAX = 1024
NEG_BIG = -1e30
EXP_ZERO_ABOVE = 110.0
MAX_BISECT = 256
BISECT_STEPS_PER_CHECK = 3


def _cparams(sem):
    return pltpu.CompilerParams(dimension_semantics=sem, vmem_limit_bytes=VMEM_LIMIT)


def _resident(shape):
    nd = len(shape)
    return pl.BlockSpec(shape, lambda *_: (0,) * nd, pipeline_mode=pl.Buffered(1))


class Seg(NamedTuple):
    w: int
    col: int
    width: int
    out: int
    out_col: int
    rope: int
    scale: float


def _rope_slab(y, half, c, s_up, s_dn):
    up = pltpu.roll(y, LANES - half, axis=1)
    dn = pltpu.roll(y, half, axis=1)
    return y * c + up * s_up + dn * s_dn


def _fused_proj_kernel(x_ref, *refs, segs, rope_halves, w_transposed):
    n_w, n_tab = len(w_transposed), 3 * len(rope_halves)
    ws, tabs, outs = refs[:n_w], refs[n_w:n_w + n_tab], refs[n_w + n_tab:]
    x = x_ref[...].astype(jnp.bfloat16)
    for sg in segs:
        if w_transposed[sg.w]:
            y = lax.dot_general(x, ws[sg.w][sg.col:sg.col + sg.width, :], (((1,), (1,)), ((), ())),
                                preferred_element_type=jnp.float32)
        else:
            y = jnp.dot(x, ws[sg.w][:, sg.col:sg.col + sg.width], preferred_element_type=jnp.float32)
        if sg.scale != 1.0:
            y = y * sg.scale
        if sg.rope >= 0:
            c, s_up, s_dn = (t[...] for t in tabs[3 * sg.rope:3 * sg.rope + 3])
            slabs = [_rope_slab(y[:, j * LANES:(j + 1) * LANES], rope_halves[sg.rope], c, s_up, s_dn)
                     for j in range(sg.width // LANES)]
            y = jnp.concatenate(slabs, axis=1) if len(slabs) > 1 else slabs[0]
        o_ref = outs[sg.out]
        o_ref[:, sg.out_col:sg.out_col + sg.width] = y.astype(o_ref.dtype)


def _fused_project(x, ws, w_transposed, plan, out_dtypes, ropes):
    s, kdim = x.shape
    segs, cols, out_cols = [], [0] * len(ws), [0] * len(out_dtypes)
    for wi, width, out, rope, scale in plan:
        for c0 in range(0, width, SEG_MAX):
            wd = min(SEG_MAX, width - c0)
            segs.append(Seg(wi, cols[wi] + c0, wd, out, out_cols[out] + c0, rope, scale))
        cols[wi] += width
        out_cols[out] += width
    assert all(c == w.shape[0 if t else 1] for c, w, t in zip(cols, ws, w_transposed))
    tm = TM_PROJ
    tabs = [t for r in ropes for t in r[1:]]
    return pl.pallas_call(
        functools.partial(_fused_proj_kernel, segs=tuple(segs), rope_halves=tuple(r[0] for r in ropes),
                          w_transposed=tuple(w_transposed)),
        grid=(s // tm,),
        in_specs=[pl.BlockSpec((tm, kdim), lambda i: (i, 0))] + [_resident(w.shape) for w in ws]
                 + [pl.BlockSpec((tm, LANES), lambda i: (i, 0))] * len(tabs),
        out_specs=[pl.BlockSpec((tm, n), lambda i: (i, 0)) for n in out_cols],
        out_shape=[jax.ShapeDtypeStruct((s, n), dt) for n, dt in zip(out_cols, out_dtypes)],
        compiler_params=_cparams(("parallel",)),
        name="fused_proj",
    )(x, *ws, *tabs)


def _rope_tables(positions, head_dim, rot_dim):
    half = rot_dim // 2
    inv = 1.0 / (ROPE_THETA ** (jnp.arange(half, dtype=jnp.float32) / half))
    ang = positions.astype(jnp.float32)[:, None] * inv
    cos, sin = jnp.cos(ang), jnp.sin(ang)
    s = positions.shape[0]
    ones = jnp.ones((s, head_dim - rot_dim), jnp.float32)
    zeros_h = jnp.zeros((s, half), jnp.float32)
    zeros_r = jnp.zeros((s, head_dim - rot_dim), jnp.float32)
    c = jnp.concatenate([cos, cos, ones], axis=1)
    s_up = jnp.concatenate([-sin, zeros_h, zeros_r], axis=1)
    s_dn = jnp.concatenate([zeros_h, sin, zeros_r], axis=1)
    reps = LANES // head_dim
    return (half,) + tuple(jnp.tile(t, (1, reps)) for t in (c, s_up, s_dn))


def _stack_heads(q_ref, n, groups):
    return jnp.concatenate(
        [q_ref[:, (n * groups + g) * HEAD_DIM:(n * groups + g + 1) * HEAD_DIM] for g in range(groups)], axis=0)


def _store_gated(o_ref, gate_ref, out, n, groups):
    tq = o_ref.shape[0]
    for g in range(groups):
        cols = slice((n * groups + g) * HEAD_DIM, (n * groups + g + 1) * HEAD_DIM)
        gate = gate_ref[:, cols]
        o_ref[:, cols] = (out[g * tq:(g + 1) * tq, :] * (gate * jax.nn.sigmoid(gate))).astype(o_ref.dtype)


def _mixer_a_kernel(q_ref, iq_ref, iw_ref, gate_ref, ikt_ref, k_ref, v_ref, o_ref,
                    score_ref, wrep_ref, lo_ref, hi_ref, cnt_ref, mid_ref, mn_ref, mx_ref,
                    qn_ref, z_ref, m_ref, acc_ref, *, n_sel, idx_scale):
    tq = q_ref.shape[0]
    tk = tq
    groups = q_ref.shape[1] // HEAD_DIM // N_KV
    i = pl.program_id(0)
    nkb = i + 1
    nkb2 = (i + 2) // 2
    f32, bf16 = jnp.float32, jnp.bfloat16

    row = lax.broadcasted_iota(jnp.int32, (tq, tk), 0)
    col = lax.broadcasted_iota(jnp.int32, (tq, tk), 1)

    for h in range(IDX_HEADS):
        wrep_ref[h] = jnp.broadcast_to(iw_ref[:, h:h + 1] * idx_scale, (tq, LANES))

    mn_ref[...] = jnp.full((tq, tk), jnp.inf, f32)
    mx_ref[...] = jnp.full((tq, tk), -jnp.inf, f32)

    def score_body(kb, carry):
        koff = pl.multiple_of(kb * tk, tk)
        ikt = ikt_ref[:, pl.ds(koff, tk)]
        acc = jnp.zeros((tq, tk), f32)
        for h in range(IDX_HEADS):
            d = jnp.dot(iq_ref[:, h * IDX_DIM:(h + 1) * IDX_DIM], ikt, preferred_element_type=f32)
            w = wrep_ref[h]
            acc = acc + jnp.concatenate([w] * (tk // LANES), axis=1) * jnp.maximum(d, 0.0)
        causal = (col + kb * tk) <= (row + i * tq)
        score_ref[:, pl.ds(koff, tk)] = jnp.where(causal, acc, -jnp.inf)
        mn_ref[...] = jnp.minimum(mn_ref[...], jnp.where(causal, acc, jnp.inf))
        mx_ref[...] = jnp.maximum(mx_ref[...], jnp.where(causal, acc, -jnp.inf))
        return carry

    lax.fori_loop(0, nkb, score_body, 0)
    score_ref[:, pl.ds(pl.multiple_of(nkb * tk, tk), tk)] = jnp.full((tq, tk), -jnp.inf, f32)

    t1 = (lax.broadcasted_iota(jnp.int32, (tq, LANES), 0) + i * tq + 1).astype(f32)
    k_t = jnp.minimum(t1, float(n_sel))
    row_min = jnp.broadcast_to(jnp.min(mn_ref[...], axis=1, keepdims=True), (tq, LANES))
    if tk >= n_sel:
        group_lo = jnp.broadcast_to(jnp.min(mx_ref[...], axis=1, keepdims=True), (tq, LANES))
        lo_ref[...] = jnp.where(t1 >= float(tk), group_lo, row_min)
    else:
        lo_ref[...] = row_min
    hi_ref[...] = jnp.broadcast_to(jnp.max(mx_ref[...], axis=1, keepdims=True), (tq, LANES))
    cnt_ref[...] = t1

    def unresolved():
        lo, hi = lo_ref[...], hi_ref[...]
        mid = 0.5 * lo + 0.5 * hi
        open_ = jnp.where(cnt_ref[...] != k_t, 1.0, 0.0) * jnp.where(mid > lo, 1.0, 0.0) * jnp.where(mid < hi, 1.0, 0.0)
        return (jnp.max(open_) > 0.0).astype(jnp.int32)

    n_half = tq // LANES

    def bisect_step():
        lo, hi = lo_ref[...], hi_ref[...]
        mid_ref[...] = 0.5 * lo + 0.5 * hi

        def count_body(kb2, accs):
            koff = pl.multiple_of(kb2 * TK2, TK2)
            new = []
            for rh in range(n_half):
                rows = slice(rh * LANES, (rh + 1) * LANES)
                acc = accs[rh]
                for c in range(TK2 // LANES):
                    s = score_ref[rows, pl.ds(koff + c * LANES, LANES)]
                    acc = acc + jnp.where(s >= mid_ref[rows, :], 1.0, 0.0)
                new.append(acc)
            return tuple(new)

        accs = lax.fori_loop(0, nkb2, count_body, (jnp.zeros((LANES, LANES), f32),) * n_half)
        cnt = jnp.broadcast_to(jnp.sum(jnp.concatenate(accs, axis=0), axis=1, keepdims=True), (tq, LANES))
        mid = mid_ref[...]
        ge = cnt >= k_t
        lo_ref[...] = jnp.where(ge, mid, lo)
        hi_ref[...] = jnp.where(ge, hi, mid)
        cnt_ref[...] = jnp.where(ge, cnt, cnt_ref[...])

    def bisect_body(state):
        for _ in range(BISECT_STEPS_PER_CHECK):
            bisect_step()
        return state[0] + BISECT_STEPS_PER_CHECK, unresolved()

    lax.while_loop(lambda st: jnp.logical_and(st[1] > 0, st[0] < MAX_BISECT), bisect_body,
                   (jnp.int32(0), unresolved()))

    thr = jnp.concatenate([lo_ref[...]] * (TK2 // LANES), axis=1)
    m_rows = groups * tq
    for n in range(N_KV):
        qn_ref[n] = _stack_heads(q_ref, n, groups)
    m_ref[...] = jnp.full((N_KV, m_rows, LANES), NEG_BIG, f32)
    acc_ref[...] = jnp.zeros((N_KV, m_rows, 2 * HEAD_DIM), f32)

    def logits(kb2, n):
        kblk = k_ref[pl.ds(pl.multiple_of(kb2 * TK2, TK2), TK2), n * HEAD_DIM:(n + 1) * HEAD_DIM]
        return lax.dot_general(qn_ref[n], kblk, (((1,), (1,)), ((), ())), preferred_element_type=f32).astype(bf16)

    for n in range(N_KV):
        z_ref[n] = logits(0, n)

    def attn_body(kb2, carry):
        koff = pl.multiple_of(kb2 * TK2, TK2)
        ahead = jnp.minimum(kb2 + 1, nkb2 - 1)
        bias = jnp.where(score_ref[:, pl.ds(koff, TK2)] >= thr, 0.0, NEG_BIG).astype(bf16)
        bias = jnp.concatenate([bias] * groups, axis=0)
        for n in range(N_KV):
            vblk = v_ref[pl.ds(koff, TK2), n * HEAD_DIM:(n + 1) * HEAD_DIM]
            vext = jnp.concatenate([vblk, jnp.ones_like(vblk)], axis=1)
            z = z_ref[n] + bias
            z_ref[n] = logits(ahead, n)
            m_old = m_ref[n]
            m_new = jnp.maximum(m_old, jnp.max(z, axis=1, keepdims=True).astype(f32))
            p = jnp.exp2(z - jnp.concatenate([m_new.astype(bf16)] * (TK2 // LANES), axis=1))
            alpha = jnp.exp2(m_old - m_new)
            acc_ref[n] = (jnp.concatenate([alpha, alpha], axis=1) * acc_ref[n]
                          + jnp.dot(p, vext, preferred_element_type=f32))
            m_ref[n] = m_new
        return carry

    lax.fori_loop(0, nkb2, attn_body, 0)
    for n in range(N_KV):
        acc = acc_ref[n]
        _store_gated(o_ref, gate_ref, acc[:, :HEAD_DIM] / acc[:, HEAD_DIM:], n, groups)


def _mixer_a(q, ix, iw, gate, ikt, k, v, n_sel):
    s, width = q.shape
    groups = width // HEAD_DIM // N_KV
    f32 = jnp.float32
    kern = functools.partial(_mixer_a_kernel, n_sel=n_sel, idx_scale=(IDX_DIM ** -0.5) * (IDX_HEADS ** -0.5))
    return pl.pallas_call(
        kern,
        grid=(s // TQ,),
        in_specs=[pl.BlockSpec((TQ, width), lambda i: (i, 0)),
                  pl.BlockSpec((TQ, IDX_HEADS * IDX_DIM), lambda i: (i, 0)),
                  pl.BlockSpec((TQ, LANES), lambda i: (i, 0)),
                  pl.BlockSpec((TQ, width), lambda i: (i, 0)),
                  _resident(ikt.shape), _resident(k.shape), _resident(v.shape)],
        out_specs=pl.BlockSpec((TQ, width), lambda i: (i, 0)),
        out_shape=jax.ShapeDtypeStruct((s, width), jnp.bfloat16),
        scratch_shapes=[pltpu.VMEM((TQ, s + TQ), f32),
                        pltpu.VMEM((IDX_HEADS, TQ, LANES), f32),
                        pltpu.VMEM((TQ, LANES), f32),
                        pltpu.VMEM((TQ, LANES), f32),
                        pltpu.VMEM((TQ, LANES), f32),
                        pltpu.VMEM((TQ, LANES), f32),
                        pltpu.VMEM((TQ, TQ), f32),
                        pltpu.VMEM((TQ, TQ), f32),
                        pltpu.VMEM((N_KV, groups * TQ, HEAD_DIM), jnp.bfloat16),
                        pltpu.VMEM((N_KV, groups * TQ, TK2), jnp.bfloat16),
                        pltpu.VMEM((N_KV, groups * TQ, LANES), f32),
                        pltpu.VMEM((N_KV, groups * TQ, 2 * HEAD_DIM), f32)],
        compiler_params=_cparams(("parallel",)),
        name="mixer_a",
    )(q, ix, iw, gate, ikt, k, v)


def _mixer_b_kernel(q_ref, gate_ref, k_ref, v_ref, o_ref, c_ref, acc_ref):
    tq = q_ref.shape[0]
    tk = tq
    groups = q_ref.shape[1] // HEAD_DIM // N_KV
    m_rows = groups * tq
    i = pl.program_id(0)
    f32, bf16 = jnp.float32, jnp.bfloat16

    row = lax.broadcasted_iota(jnp.int32, (tq, tk), 0)
    col = lax.broadcasted_iota(jnp.int32, (tq, tk), 1)
    later = jnp.where(row > col, 1.0, 0.0).astype(bf16)
    later2 = jnp.concatenate([later, later], axis=0)
    strict = jnp.concatenate([col < row] * groups, axis=0)

    for n in range(N_KV):
        qn = _stack_heads(q_ref, n, groups)

        def step(kb, diagonal):
            koff = pl.multiple_of(kb * tk, tk)
            kblk = k_ref[pl.ds(koff, tk), n * HEAD_DIM:(n + 1) * HEAD_DIM]
            vblk = v_ref[pl.ds(koff, tk), n * HEAD_DIM:(n + 1) * HEAD_DIM]
            z = lax.dot_general(qn, kblk, (((1,), (1,)), ((), ())), preferred_element_type=f32)
            t = jnp.log(1.0 + jnp.exp2(jnp.abs(z) * (-LOG2_E)))
            sp = jnp.maximum(z, 0.0) + t
            if diagonal:
                sp = jnp.where(strict, sp, 0.0)
            sp_hi = sp.astype(bf16)
            sp_lo = (sp - sp_hi.astype(f32)).astype(bf16)
            suffix = jnp.dot(jnp.concatenate([sp_hi, sp_lo], axis=1), later2, preferred_element_type=f32)
            c_old = c_ref[...]
            total = suffix + jnp.concatenate([c_old] * (tk // LANES), axis=1)
            a = jnp.exp((jnp.minimum(z, 0.0) - t) - total)
            if diagonal:
                a = jnp.where(strict, a, 0.0)
            acc_ref[...] += jnp.dot(a.astype(bf16), vblk, preferred_element_type=f32)
            c_new = c_old + jnp.broadcast_to(jnp.sum(sp, axis=1, keepdims=True), (m_rows, LANES))
            c_ref[...] = c_new
            return (jnp.min(c_new) <= EXP_ZERO_ABOVE).astype(jnp.int32)

        c_ref[...] = jnp.zeros((m_rows, LANES), f32)
        acc_ref[...] = jnp.zeros((m_rows, HEAD_DIM), f32)
        go = step(i, True)
        lax.while_loop(lambda st: jnp.logical_and(st[0] >= 0, st[1] > 0),
                       lambda st: (st[0] - 1, step(st[0], False)), (i - 1, go))
        _store_gated(o_ref, gate_ref, acc_ref[...], n, groups)


def _mixer_b(q, gate, kv):
    s, width = q.shape
    groups = width // HEAD_DIM // N_KV
    kvw = kv.shape[1] // 2
    half_spec = lambda c: pl.BlockSpec((s, kvw), lambda i: (0, c), pipeline_mode=pl.Buffered(1))
    return pl.pallas_call(
        _mixer_b_kernel,
        grid=(s // TQ,),
        in_specs=[pl.BlockSpec((TQ, width), lambda i: (i, 0)),
                  pl.BlockSpec((TQ, width), lambda i: (i, 0)),
                  half_spec(0), half_spec(1)],
        out_specs=pl.BlockSpec((TQ, width), lambda i: (i, 0)),
        out_shape=jax.ShapeDtypeStruct((s, width), jnp.bfloat16),
        scratch_shapes=[pltpu.VMEM((groups * TQ, LANES), jnp.float32),
                        pltpu.VMEM((groups * TQ, HEAD_DIM), jnp.float32)],
        compiler_params=_cparams(("parallel",)),
        name="mixer_b",
    )(q, gate, kv, kv)


def _post_kernel(og_ref, x_ref, p_ref, wo_ref, wg_ref, wp_ref, lng_ref, lnb_ref, o_ref, *, alpha):
    f32 = jnp.float32
    h = jnp.dot(og_ref[...], wo_ref[...], preferred_element_type=f32)
    y = alpha * x_ref[...] + h
    mu = jnp.mean(y, axis=-1, keepdims=True)
    d = y - mu
    var = jnp.mean(d * d, axis=-1, keepdims=True)
    yn = d * lax.rsqrt(var + LN_EPS) * lng_ref[...] + lnb_ref[...]
    gl = jnp.dot(yn.astype(jnp.bfloat16), wg_ref[...], preferred_element_type=f32)
    pe = jnp.dot(p_ref[...].astype(jnp.bfloat16), wp_ref[...], preferred_element_type=f32)
    o_ref[...] = yn + pe * jax.nn.sigmoid(gl)


def _post(og, x, p, layer, batch, wo, wg, wp, lng, lnb, alpha):
    s, d = x.shape
    tm = TM_PROJ
    return pl.pallas_call(
        functools.partial(_post_kernel, alpha=alpha),
        grid=(s // tm,),
        in_specs=[pl.BlockSpec((tm, og.shape[1]), lambda i: (i, 0)),
                  pl.BlockSpec((tm, d), lambda i: (i, 0)),
                  pl.BlockSpec((None, None, tm, p.shape[3]), lambda i: (layer, batch, i, 0)),
                  _resident(wo.shape), _resident(wg.shape), _resident(wp.shape),
                  _resident(lng.shape), _resident(lnb.shape)],
        out_specs=pl.BlockSpec((tm, d), lambda i: (i, 0)),
        out_shape=jax.ShapeDtypeStruct((s, d), jnp.float32),
        compiler_params=_cparams(("parallel",)),
        name="post",
    )(og, x, p, wo, wg, wp, lng, lnb)


def _pad_rows(w, n):
    return jnp.pad(w, ((0, n - w.shape[0]), (0, 0)))


def kernel(x, p, positions, w_in_a, w_o_a, w_kv_b, w_in_b, w_o_b, ln_g, ln_b, w_ple, w_ple_gate):
    bsz, s, d = x.shape
    n_a, n_b = w_in_a.shape[0], w_in_b.shape[0]
    depth = n_a + n_b
    alpha = (2.0 * depth) ** 0.25
    width = w_o_a.shape[1]
    kvw = N_KV * HEAD_DIM
    idxw = IDX_HEADS * IDX_DIM
    n_sel = min(TOPK_MAX, s // 4)
    bf16, f32 = jnp.bfloat16, jnp.float32
    sm_scale = HEAD_DIM ** -0.5

    outs = []
    for b in range(bsz):
        xb = x[b]
        ropes = [_rope_tables(positions[b], HEAD_DIM, ROT_DIM), _rope_tables(positions[b], IDX_DIM, IDX_ROT)]
        kv_shared = None
        for i in range(depth):
            if i < n_a:
                main = width + 2 * kvw + width + idxw
                wt = jnp.swapaxes(w_in_a[i], 0, 1)
                wt = jnp.concatenate([wt[:main], _pad_rows(wt[main:main + IDX_DIM], LANES),
                                      _pad_rows(wt[main + IDX_DIM:], LANES)], axis=0).astype(bf16)
                plan = [(0, width, 0, 0, LOG2_E * sm_scale), (0, kvw, 1, 0, 1.0), (0, kvw, 2, -1, 1.0),
                        (0, width, 3, -1, 1.0), (0, idxw + LANES, 4, 1, 1.0), (0, LANES, 5, -1, 1.0)]
                q, k, v, gate, ix, iw = _fused_project(xb, [wt], [True], plan, [bf16, bf16, bf16, f32, bf16, f32],
                                                       ropes)
                ikt = ix[:, idxw:idxw + IDX_DIM].T
                og = _mixer_a(q, ix, iw, gate, ikt, k, v, n_sel)
                wo = w_o_a[i].astype(bf16)
            else:
                j = i - n_a
                if kv_shared is None:
                    plan = [(0, 2 * kvw, 0, -1, 1.0), (1, width, 1, -1, sm_scale), (1, width, 2, -1, 1.0)]
                    kv_shared, q, gate = _fused_project(xb, [w_kv_b.astype(bf16), w_in_b[j].astype(bf16)],
                                                        [False, False], plan, [bf16, bf16, f32], [])
                else:
                    plan = [(0, width, 0, -1, sm_scale), (0, width, 1, -1, 1.0)]
                    q, gate = _fused_project(xb, [w_in_b[j].astype(bf16)], [False], plan, [bf16, f32], [])
                og = _mixer_b(q, gate, kv_shared)
                wo = w_o_b[j].astype(bf16)
            xb = _post(og, xb, p, i, b, wo, w_ple_gate[i].astype(bf16), w_ple[i].astype(bf16),
                       ln_g[i][None, :], ln_b[i][None, :], alpha)
        outs.append(xb)
    return jnp.stack(outs, axis=0)
```

```python
import functools
from typing import NamedTuple

import jax
import jax.numpy as jnp
from jax import lax
from jax.experimental import pallas as pl
from jax.experimental.pallas import tpu as pltpu

HEAD_DIM = 128
N_KV = 4
ROT_DIM = HEAD_DIM // 4
IDX_HEADS = 16
IDX_DIM = 64
IDX_ROT = IDX_DIM // 4
TOPK_MAX = 256
ROPE_THETA = 500000.0
LN_EPS = 1e-5
LOG2_E = 1.4426950408889634

LANES = 128
V7X_VMEM_BYTES = 64 * 1024 * 1024
VMEM_LIMIT = V7X_VMEM_BYTES - 8 * 1024 * 1024

TQ = 256
TK2 = 2 * TQ
TM_PROJ = 256
SEG_MAX = 1024
NEG_BIG = -1e30
EXP_ZERO_ABOVE = 110.0
MAX_BISECT = 256
BISECT_STEPS_PER_CHECK = 3


def _cparams(sem):
    return pltpu.CompilerParams(dimension_semantics=sem, vmem_limit_bytes=VMEM_LIMIT)


def _resident(shape):
    nd = len(shape)
    return pl.BlockSpec(shape, lambda *_: (0,) * nd, pipeline_mode=pl.Buffered(1))


class Seg(NamedTuple):
    w: int
    col: int
    width: int
    stored: int
    out: int
    out_col: int
    rope: int
    scale: float


def _rope_slab(y, half, c, s_up, s_dn):
    up = pltpu.roll(y, LANES - half, axis=1)
    dn = pltpu.roll(y, half, axis=1)
    return y * c + up * s_up + dn * s_dn


def _fused_proj_kernel(x_ref, *refs, segs, rope_halves, w_transposed):
    n_w, n_tab = len(w_transposed), 3 * len(rope_halves)
    ws, tabs, outs = refs[:n_w], refs[n_w:n_w + n_tab], refs[n_w + n_tab:]
    x = x_ref[...].astype(jnp.bfloat16)
    for sg in segs:
        if w_transposed[sg.w]:
            y = lax.dot_general(x, ws[sg.w][sg.col:sg.col + sg.width, :], (((1,), (1,)), ((), ())),
                                preferred_element_type=jnp.float32)
        else:
            y = jnp.dot(x, ws[sg.w][:, sg.col:sg.col + sg.width], preferred_element_type=jnp.float32)
        if sg.scale != 1.0:
            y = y * sg.scale
        if sg.stored > sg.width:
            y = jnp.concatenate([y, jnp.zeros((y.shape[0], sg.stored - sg.width), y.dtype)], axis=1)
        if sg.rope >= 0:
            c, s_up, s_dn = (t[...] for t in tabs[3 * sg.rope:3 * sg.rope + 3])
            slabs = [_rope_slab(y[:, j * LANES:(j + 1) * LANES], rope_halves[sg.rope], c, s_up, s_dn)
                     for j in range(sg.stored // LANES)]
            y = jnp.concatenate(slabs, axis=1) if len(slabs) > 1 else slabs[0]
        o_ref = outs[sg.out]
        o_ref[:, sg.out_col:sg.out_col + sg.stored] = y.astype(o_ref.dtype)


def _fused_project(x, ws, w_transposed, plan, out_dtypes, ropes):
    s, kdim = x.shape
    segs, cols, out_cols = [], [0] * len(ws), [0] * len(out_dtypes)
    for wi, width, stored, out, rope, scale in plan:
        assert stored == width or width <= SEG_MAX
        for c0 in range(0, width, SEG_MAX):
            wd = min(SEG_MAX, width - c0)
            segs.append(Seg(wi, cols[wi] + c0, wd, wd if stored == width else stored, out, out_cols[out] + c0,
                            rope, scale))
        cols[wi] += width
        out_cols[out] += stored
    assert all(c == w.shape[0 if t else 1] for c, w, t in zip(cols, ws, w_transposed))
    tm = TM_PROJ
    tabs = [t for r in ropes for t in r[1:]]
    return pl.pallas_call(
        functools.partial(_fused_proj_kernel, segs=tuple(segs), rope_halves=tuple(r[0] for r in ropes),
                          w_transposed=tuple(w_transposed)),
        grid=(s // tm,),
        in_specs=[pl.BlockSpec((tm, kdim), lambda i: (i, 0))] + [_resident(w.shape) for w in ws]
                 + [pl.BlockSpec((tm, LANES), lambda i: (i, 0))] * len(tabs),
        out_specs=[pl.BlockSpec((tm, n), lambda i: (i, 0)) for n in out_cols],
        out_shape=[jax.ShapeDtypeStruct((s, n), dt) for n, dt in zip(out_cols, out_dtypes)],
        compiler_params=_cparams(("parallel",)),
        name="fused_proj",
    )(x, *ws, *tabs)


def _rope_tables(positions, head_dim, rot_dim):
    half = rot_dim // 2
    inv = 1.0 / (ROPE_THETA ** (jnp.arange(half, dtype=jnp.float32) / half))
    ang = positions.astype(jnp.float32)[:, None] * inv
    cos, sin = jnp.cos(ang), jnp.sin(ang)
    s = positions.shape[0]
    ones = jnp.ones((s, head_dim - rot_dim), jnp.float32)
    zeros_h = jnp.zeros((s, half), jnp.float32)
    zeros_r = jnp.zeros((s, head_dim - rot_dim), jnp.float32)
    c = jnp.concatenate([cos, cos, ones], axis=1)
    s_up = jnp.concatenate([-sin, zeros_h, zeros_r], axis=1)
    s_dn = jnp.concatenate([zeros_h, sin, zeros_r], axis=1)
    reps = LANES // head_dim
    return (half,) + tuple(jnp.tile(t, (1, reps)) for t in (c, s_up, s_dn))


def _stack_heads(q_ref, n, groups):
    return jnp.concatenate(
        [q_ref[:, (n * groups + g) * HEAD_DIM:(n * groups + g + 1) * HEAD_DIM] for g in range(groups)], axis=0)


def _store_gated(o_ref, gate_ref, out, n, groups):
    tq = o_ref.shape[0]
    for g in range(groups):
        cols = slice((n * groups + g) * HEAD_DIM, (n * groups + g + 1) * HEAD_DIM)
        gate = gate_ref[:, cols]
        o_ref[:, cols] = (out[g * tq:(g + 1) * tq, :] * (gate * jax.nn.sigmoid(gate))).astype(o_ref.dtype)


def _mixer_a_kernel(q_ref, iq_ref, iw_ref, gate_ref, ikt_ref, k_ref, v_ref, o_ref,
                    score_ref, wrep_ref, lo_ref, hi_ref, cnt_ref, mn_ref, mx_ref,
                    qn_ref, z_ref, m_ref, acc_ref, *, n_sel, idx_scale):
    tq = q_ref.shape[0]
    tk = tq
    groups = q_ref.shape[1] // HEAD_DIM // N_KV
    i = pl.program_id(0)
    nkb2 = (i + 2) // 2
    f32, bf16 = jnp.float32, jnp.bfloat16

    row2 = lax.broadcasted_iota(jnp.int32, (tq, TK2), 0)
    col2 = lax.broadcasted_iota(jnp.int32, (tq, TK2), 1)

    for h in range(IDX_HEADS):
        wrep_ref[h] = jnp.broadcast_to(iw_ref[:, h:h + 1] * idx_scale, (tq, LANES))

    mn_ref[...] = jnp.full((tq, tk), jnp.inf, f32)
    mx_ref[...] = jnp.full((tq, tk), -jnp.inf, f32)

    def score_body(kb2, carry):
        koff = pl.multiple_of(kb2 * TK2, TK2)
        ikt = ikt_ref[:, pl.ds(koff, TK2)]
        acc = jnp.zeros((tq, TK2), f32)
        for h in range(IDX_HEADS):
            d = jnp.dot(iq_ref[:, h * IDX_DIM:(h + 1) * IDX_DIM], ikt, preferred_element_type=f32)
            w = wrep_ref[h]
            acc = acc + jnp.concatenate([w] * (TK2 // LANES), axis=1) * jnp.maximum(d, 0.0)
        causal = (col2 + kb2 * TK2) <= (row2 + i * tq)
        score_ref[:, pl.ds(koff, TK2)] = jnp.where(causal, acc, -jnp.inf)
        lo_part = jnp.where(causal, acc, jnp.inf)
        hi_part = jnp.where(causal, acc, -jnp.inf)
        mn_ref[...] = jnp.minimum(mn_ref[...], jnp.minimum(lo_part[:, :tk], lo_part[:, tk:]))
        mx_ref[...] = jnp.maximum(mx_ref[...], jnp.maximum(hi_part[:, :tk], hi_part[:, tk:]))
        return carry

    lax.fori_loop(0, nkb2, score_body, 0)

    t1 = (lax.broadcasted_iota(jnp.int32, (tq, LANES), 0) + i * tq + 1).astype(f32)
    k_t = jnp.minimum(t1, float(n_sel))
    row_min = jnp.broadcast_to(jnp.min(mn_ref[...], axis=1, keepdims=True), (tq, LANES))
    if tk >= n_sel:
        group_lo = jnp.broadcast_to(jnp.min(mx_ref[...], axis=1, keepdims=True), (tq, LANES))
        lo_ref[...] = jnp.where(t1 >= float(tk), group_lo, row_min)
    else:
        lo_ref[...] = row_min
    hi_ref[...] = jnp.broadcast_to(jnp.max(mx_ref[...], axis=1, keepdims=True), (tq, LANES))
    cnt_ref[...] = t1

    def unresolved():
        lo, hi = lo_ref[...], hi_ref[...]
        mid = 0.5 * lo + 0.5 * hi
        open_ = jnp.where(cnt_ref[...] != k_t, 1.0, 0.0) * jnp.where(mid > lo, 1.0, 0.0) * jnp.where(mid < hi, 1.0, 0.0)
        return (jnp.max(open_) > 0.0).astype(jnp.int32)

    n_half = tq // LANES

    def bisect_step():
        lo, hi = lo_ref[...], hi_ref[...]
        mid = 0.5 * lo + 0.5 * hi
        accs = []
        for rh in range(n_half):
            rows = slice(rh * LANES, (rh + 1) * LANES)
            mid_h = 0.5 * lo_ref[rows, :] + 0.5 * hi_ref[rows, :]

            def count_body(kb2, acc):
                koff = pl.multiple_of(kb2 * TK2, TK2)
                for c in range(TK2 // LANES):
                    s = score_ref[rows, pl.ds(koff + c * LANES, LANES)]
                    acc = acc + jnp.where(s >= mid_h, 1.0, 0.0)
                return acc

            accs.append(lax.fori_loop(0, nkb2, count_body, jnp.zeros((LANES, LANES), f32)))
        cnt = jnp.broadcast_to(jnp.sum(jnp.concatenate(accs, axis=0), axis=1, keepdims=True), (tq, LANES))
        ge = cnt >= k_t
        lo_ref[...] = jnp.where(ge, mid, lo)
        hi_ref[...] = jnp.where(ge, hi, mid)
        cnt_ref[...] = jnp.where(ge, cnt, cnt_ref[...])

    def bisect_body(state):
        for _ in range(BISECT_STEPS_PER_CHECK):
            bisect_step()
        return state[0] + BISECT_STEPS_PER_CHECK, unresolved()

    lax.while_loop(lambda st: jnp.logical_and(st[1] > 0, st[0] < MAX_BISECT), bisect_body,
                   (jnp.int32(0), unresolved()))

    thr = jnp.concatenate([lo_ref[...]] * (TK2 // LANES), axis=1)
    m_rows = groups * tq
    for n in range(N_KV):
        qn_ref[n] = _stack_heads(q_ref, n, groups)
    m_ref[...] = jnp.full((N_KV, m_rows, LANES), NEG_BIG, f32)
    acc_ref[...] = jnp.zeros((N_KV, m_rows, 2 * HEAD_DIM), f32)

    def logits(kb2, n):
        kblk = k_ref[pl.ds(pl.multiple_of(kb2 * TK2, TK2), TK2), n * HEAD_DIM:(n + 1) * HEAD_DIM]
        return lax.dot_general(qn_ref[n], kblk, (((1,), (1,)), ((), ())), preferred_element_type=f32).astype(bf16)

    for n in range(N_KV):
        z_ref[n] = logits(0, n)

    def attn_body(kb2, carry):
        koff = pl.multiple_of(kb2 * TK2, TK2)
        ahead = jnp.minimum(kb2 + 1, nkb2 - 1)
        bias = jnp.where(score_ref[:, pl.ds(koff, TK2)] >= thr, 0.0, NEG_BIG).astype(bf16)
        bias = jnp.concatenate([bias] * groups, axis=0)
        for n in range(N_KV):
            vblk = v_ref[pl.ds(koff, TK2), n * HEAD_DIM:(n + 1) * HEAD_DIM]
            vext = jnp.concatenate([vblk, jnp.ones_like(vblk)], axis=1)
            z = z_ref[n] + bias
            z_ref[n] = logits(ahead, n)
            m_old = m_ref[n]
            m_new = jnp.maximum(m_old, jnp.max(z, axis=1, keepdims=True).astype(f32))
            p = jnp.exp2(z - jnp.concatenate([m_new.astype(bf16)] * (TK2 // LANES), axis=1))
            alpha = jnp.exp2(m_old - m_new)
            acc_ref[n] = (jnp.concatenate([alpha, alpha], axis=1) * acc_ref[n]
                          + jnp.dot(p, vext, preferred_element_type=f32))
            m_ref[n] = m_new
        return carry

    lax.fori_loop(0, nkb2, attn_body, 0)
    for n in range(N_KV):
        acc = acc_ref[n]
        _store_gated(o_ref, gate_ref, acc[:, :HEAD_DIM] / acc[:, HEAD_DIM:], n, groups)


def _mixer_a(q, ix, iw, gate, ikt, k, v, n_sel):
    s, width = q.shape
    groups = width // HEAD_DIM // N_KV
    f32 = jnp.float32
    kern = functools.partial(_mixer_a_kernel, n_sel=n_sel, idx_scale=(IDX_DIM ** -0.5) * (IDX_HEADS ** -0.5))
    return pl.pallas_call(
        kern,
        grid=(s // TQ,),
        in_specs=[pl.BlockSpec((TQ, width), lambda i: (i, 0)),
                  pl.BlockSpec((TQ, IDX_HEADS * IDX_DIM), lambda i: (i, 0)),
                  pl.BlockSpec((TQ, IDX_HEADS), lambda i: (i, 0)),
                  pl.BlockSpec((TQ, width), lambda i: (i, 0)),
                  _resident(ikt.shape), _resident(k.shape), _resident(v.shape)],
        out_specs=pl.BlockSpec((TQ, width), lambda i: (i, 0)),
        out_shape=jax.ShapeDtypeStruct((s, width), jnp.bfloat16),
        scratch_shapes=[pltpu.VMEM((TQ, s + LANES), f32),
                        pltpu.VMEM((IDX_HEADS, TQ, LANES), f32),
                        pltpu.VMEM((TQ, LANES), f32),
                        pltpu.VMEM((TQ, LANES), f32),
                        pltpu.VMEM((TQ, LANES), f32),
                        pltpu.VMEM((TQ, TQ), f32),
                        pltpu.VMEM((TQ, TQ), f32),
                        pltpu.VMEM((N_KV, groups * TQ, HEAD_DIM), jnp.bfloat16),
                        pltpu.VMEM((N_KV, groups * TQ, TK2), jnp.bfloat16),
                        pltpu.VMEM((N_KV, groups * TQ, LANES), f32),
                        pltpu.VMEM((N_KV, groups * TQ, 2 * HEAD_DIM), f32)],
        compiler_params=_cparams(("parallel",)),
        name="mixer_a",
    )(q, ix, iw, gate, ikt, k, v)


def _mixer_b_kernel(q_ref, gate_ref, k_ref, v_ref, o_ref, c_ref, acc_ref):
    tq = q_ref.shape[0]
    tk = tq
    groups = q_ref.shape[1] // HEAD_DIM // N_KV
    m_rows = groups * tq
    i = pl.program_id(0)
    f32, bf16 = jnp.float32, jnp.bfloat16

    row = lax.broadcasted_iota(jnp.int32, (tq, tk), 0)
    col = lax.broadcasted_iota(jnp.int32, (tq, tk), 1)
    later = jnp.where(row > col, 1.0, 0.0).astype(bf16)
    later2 = jnp.concatenate([later, later], axis=0)
    strict = jnp.concatenate([col < row] * groups, axis=0)

    for n in range(N_KV):
        qn = _stack_heads(q_ref, n, groups)

        def step(kb, diagonal):
            koff = pl.multiple_of(kb * tk, tk)
            kblk = k_ref[pl.ds(koff, tk), n * HEAD_DIM:(n + 1) * HEAD_DIM]
            vblk = v_ref[pl.ds(koff, tk), n * HEAD_DIM:(n + 1) * HEAD_DIM]
            z = lax.dot_general(qn, kblk, (((1,), (1,)), ((), ())), preferred_element_type=f32)
            t = jnp.log(1.0 + jnp.exp2(jnp.abs(z) * (-LOG2_E)))
            sp = jnp.maximum(z, 0.0) + t
            if diagonal:
                sp = jnp.where(strict, sp, 0.0)
            sp_hi = sp.astype(bf16)
            sp_lo = (sp - sp_hi.astype(f32)).astype(bf16)
            suffix = jnp.dot(jnp.concatenate([sp_hi, sp_lo], axis=1), later2, preferred_element_type=f32)
            c_old = c_ref[...]
            total = suffix + jnp.concatenate([c_old] * (tk // LANES), axis=1)
            a = jnp.exp((jnp.minimum(z, 0.0) - t) - total)
            if diagonal:
                a = jnp.where(strict, a, 0.0)
            acc_ref[...] += jnp.dot(a.astype(bf16), vblk, preferred_element_type=f32)
            c_new = c_old + jnp.broadcast_to(jnp.sum(sp, axis=1, keepdims=True), (m_rows, LANES))
            c_ref[...] = c_new
            return (jnp.min(c_new) <= EXP_ZERO_ABOVE).astype(jnp.int32)

        c_ref[...] = jnp.zeros((m_rows, LANES), f32)
        acc_ref[...] = jnp.zeros((m_rows, HEAD_DIM), f32)
        go = step(i, True)
        lax.while_loop(lambda st: jnp.logical_and(st[0] >= 0, st[1] > 0),
                       lambda st: (st[0] - 1, step(st[0], False)), (i - 1, go))
        _store_gated(o_ref, gate_ref, acc_ref[...], n, groups)


def _mixer_b(q, gate, kv):
    s, width = q.shape
    groups = width // HEAD_DIM // N_KV
    kvw = kv.shape[1] // 2
    half_spec = lambda c: pl.BlockSpec((s, kvw), lambda i: (0, c), pipeline_mode=pl.Buffered(1))
    return pl.pallas_call(
        _mixer_b_kernel,
        grid=(s // TQ,),
        in_specs=[pl.BlockSpec((TQ, width), lambda i: (i, 0)),
                  pl.BlockSpec((TQ, width), lambda i: (i, 0)),
                  half_spec(0), half_spec(1)],
        out_specs=pl.BlockSpec((TQ, width), lambda i: (i, 0)),
        out_shape=jax.ShapeDtypeStruct((s, width), jnp.bfloat16),
        scratch_shapes=[pltpu.VMEM((groups * TQ, LANES), jnp.float32),
                        pltpu.VMEM((groups * TQ, HEAD_DIM), jnp.float32)],
        compiler_params=_cparams(("parallel",)),
        name="mixer_b",
    )(q, gate, kv, kv)


def _post_kernel(og_ref, x_ref, p_ref, wo_ref, wg_ref, wp_ref, lng_ref, lnb_ref, o_ref, *, alpha):
    f32 = jnp.float32
    h = jnp.dot(og_ref[...], wo_ref[...], preferred_element_type=f32)
    y = alpha * x_ref[...] + h
    mu = jnp.mean(y, axis=-1, keepdims=True)
    d = y - mu
    var = jnp.mean(d * d, axis=-1, keepdims=True)
    yn = d * lax.rsqrt(var + LN_EPS) * lng_ref[...] + lnb_ref[...]
    gl = jnp.dot(yn.astype(jnp.bfloat16), wg_ref[...], preferred_element_type=f32)
    pe = jnp.dot(p_ref[...].astype(jnp.bfloat16), wp_ref[...], preferred_element_type=f32)
    o_ref[...] = yn + pe * jax.nn.sigmoid(gl)


def _resident_slab(arr, idx):
    return pl.BlockSpec((None,) + arr.shape[1:], lambda *_: (idx,) + (0,) * (arr.ndim - 1),
                        pipeline_mode=pl.Buffered(1))


def _post(og, x, p, layer, batch, wo, wo_idx, wg, wp, lng, lnb, alpha):
    s, d = x.shape
    tm = TM_PROJ
    return pl.pallas_call(
        functools.partial(_post_kernel, alpha=alpha),
        grid=(s // tm,),
        in_specs=[pl.BlockSpec((tm, og.shape[1]), lambda i: (i, 0)),
                  pl.BlockSpec((tm, d), lambda i: (i, 0)),
                  pl.BlockSpec((None, None, tm, p.shape[3]), lambda i: (layer, batch, i, 0)),
                  _resident_slab(wo, wo_idx), _resident_slab(wg, layer), _resident_slab(wp, layer),
                  _resident_slab(lng, layer), _resident_slab(lnb, layer)],
        out_specs=pl.BlockSpec((tm, d), lambda i: (i, 0)),
        out_shape=jax.ShapeDtypeStruct((s, d), jnp.float32),
        compiler_params=_cparams(("parallel",)),
        name="post",
    )(og, x, p, wo, wg, wp, lng, lnb)


def kernel(x, p, positions, w_in_a, w_o_a, w_kv_b, w_in_b, w_o_b, ln_g, ln_b, w_ple, w_ple_gate):
    bsz, s, d = x.shape
    n_a, n_b = w_in_a.shape[0], w_in_b.shape[0]
    depth = n_a + n_b
    alpha = (2.0 * depth) ** 0.25
    width = w_o_a.shape[1]
    kvw = N_KV * HEAD_DIM
    idxw = IDX_HEADS * IDX_DIM
    n_sel = min(TOPK_MAX, s // 4)
    bf16, f32 = jnp.bfloat16, jnp.float32
    sm_scale = HEAD_DIM ** -0.5
    assert s % TK2 == 0
    wo_a, wo_b = w_o_a.astype(bf16), w_o_b.astype(bf16)
    wg_all, wp_all = w_ple_gate.astype(bf16), w_ple.astype(bf16)
    lng, lnb = ln_g[:, None, :], ln_b[:, None, :]

    outs = []
    for b in range(bsz):
        xb = x[b]
        ropes = [_rope_tables(positions[b], HEAD_DIM, ROT_DIM), _rope_tables(positions[b], IDX_DIM, IDX_ROT)]
        kv_shared = None
        for i in range(depth):
            if i < n_a:
                wt = jnp.swapaxes(w_in_a[i], 0, 1).astype(bf16)
                plan = [(0, width, width, 0, 0, LOG2_E * sm_scale), (0, kvw, kvw, 1, 0, 1.0),
                        (0, kvw, kvw, 2, -1, 1.0), (0, width, width, 3, -1, 1.0), (0, idxw, idxw, 4, 1, 1.0),
                        (0, IDX_DIM, LANES, 4, 1, 1.0), (0, IDX_HEADS, IDX_HEADS, 5, -1, 1.0)]
                q, k, v, gate, ix, iw = _fused_project(xb, [wt], [True], plan, [bf16, bf16, bf16, f32, bf16, f32],
                                                       ropes)
                ikt = ix[:, idxw:idxw + IDX_DIM].T
                og = _mixer_a(q, ix, iw, gate, ikt, k, v, n_sel)
                wo, wo_idx = wo_a, i
            else:
                j = i - n_a
                if kv_shared is None:
                    plan = [(0, 2 * kvw, 2 * kvw, 0, -1, 1.0), (1, width, width, 1, -1, sm_scale),
                            (1, width, width, 2, -1, 1.0)]
                    kv_shared, q, gate = _fused_project(xb, [w_kv_b.astype(bf16), w_in_b[j].astype(bf16)],
                                                        [False, False], plan, [bf16, bf16, f32], [])
                else:
                    plan = [(0, width, width, 0, -1, sm_scale), (0, width, width, 1, -1, 1.0)]
                    q, gate = _fused_project(xb, [w_in_b[j].astype(bf16)], [False], plan, [bf16, f32], [])
                og = _mixer_b(q, gate, kv_shared)
                wo, wo_idx = wo_b, j
            xb = _post(og, xb, p, i, b, wo, wo_idx, wg_all, wp_all, lng, lnb, alpha)
        outs.append(xb)
    return jnp.stack(outs, axis=0)
```

```python
import functools
from typing import NamedTuple

import jax
import jax.numpy as jnp
from jax import lax
from jax.experimental import pallas as pl
from jax.experimental.pallas import tpu as pltpu

HEAD_DIM = 128
N_KV = 4
ROT_DIM = HEAD_DIM // 4
IDX_HEADS = 16
IDX_DIM = 64
IDX_ROT = IDX_DIM // 4
TOPK_MAX = 256
ROPE_THETA = 500000.0
LN_EPS = 1e-5
LOG2_E = 1.4426950408889634

LANES = 128
BF16_SUBLANES = 16
V7X_VMEM_BYTES = 64 * 1024 * 1024
VMEM_LIMIT = V7X_VMEM_BYTES - 8 * 1024 * 1024

TQ = 256
TK2 = 2 * TQ
TM_PROJ = 256
SEG_MAX = 1024
NEG_BIG = -1e30
EXP_ZERO_ABOVE = 110.0
MAX_BISECT = 256
BISECT_STEPS_PER_CHECK = 3


def _cparams(sem):
    return pltpu.CompilerParams(dimension_semantics=sem, vmem_limit_bytes=VMEM_LIMIT)


def _resident(shape):
    nd = len(shape)
    return pl.BlockSpec(shape, lambda *_: (0,) * nd, pipeline_mode=pl.Buffered(1))


class Seg(NamedTuple):
    w: int
    col: int
    width: int
    stored: int
    out: int
    out_col: int
    rope: int
    scale: float


def _rope_slab(y, half, c, s_up, s_dn):
    up = pltpu.roll(y, LANES - half, axis=1)
    dn = pltpu.roll(y, half, axis=1)
    return y * c + up * s_up + dn * s_dn


class Side(NamedTuple):
    arr: jax.Array
    layer: int


def _side_ok(arr, n_steps):
    rows = arr.shape[-2]
    return rows % (n_steps * BF16_SUBLANES) == 0


def _side_io(sides, n_steps):
    in_specs, out_specs, out_shapes = [], [], []
    for sd in sides:
        r, c = sd.arr.shape[-2:]
        rows = r // n_steps
        if sd.layer < 0:
            in_specs.append(pl.BlockSpec((rows, c), lambda i: (i, 0)))
        else:
            in_specs.append(pl.BlockSpec((None, rows, c), lambda i, l=sd.layer: (l, i, 0)))
        out_specs.append(pl.BlockSpec((rows, c), lambda i: (i, 0)))
        out_shapes.append(jax.ShapeDtypeStruct((r, c), jnp.bfloat16))
    return in_specs, out_specs, out_shapes


def _cast_sides(side_in, side_out):
    for src, dst in zip(side_in, side_out):
        dst[...] = src[...].astype(dst.dtype)


def _fused_proj_kernel(x_ref, *refs, segs, rope_halves, w_transposed, n_side):
    n_w, n_tab = len(w_transposed), 3 * len(rope_halves)
    ws, tabs = refs[:n_w], refs[n_w:n_w + n_tab]
    side_in = refs[n_w + n_tab:n_w + n_tab + n_side]
    outs = refs[n_w + n_tab + n_side:len(refs) - n_side]
    _cast_sides(side_in, refs[len(refs) - n_side:])
    x = x_ref[...].astype(jnp.bfloat16)
    for sg in segs:
        if w_transposed[sg.w]:
            y = lax.dot_general(x, ws[sg.w][sg.col:sg.col + sg.width, :], (((1,), (1,)), ((), ())),
                                preferred_element_type=jnp.float32)
        else:
            y = jnp.dot(x, ws[sg.w][:, sg.col:sg.col + sg.width], preferred_element_type=jnp.float32)
        if sg.scale != 1.0:
            y = y * sg.scale
        if sg.stored > sg.width:
            y = jnp.concatenate([y, jnp.zeros((y.shape[0], sg.stored - sg.width), y.dtype)], axis=1)
        if sg.rope >= 0:
            c, s_up, s_dn = (t[...] for t in tabs[3 * sg.rope:3 * sg.rope + 3])
            slabs = [_rope_slab(y[:, j * LANES:(j + 1) * LANES], rope_halves[sg.rope], c, s_up, s_dn)
                     for j in range(sg.stored // LANES)]
            y = jnp.concatenate(slabs, axis=1) if len(slabs) > 1 else slabs[0]
        o_ref = outs[sg.out]
        o_ref[:, sg.out_col:sg.out_col + sg.stored] = y.astype(o_ref.dtype)


def _fused_project(x, ws, w_transposed, plan, out_dtypes, ropes, sides=()):
    s, kdim = x.shape
    segs, cols, out_cols = [], [0] * len(ws), [0] * len(out_dtypes)
    for wi, width, stored, out, rope, scale in plan:
        assert stored == width or width <= SEG_MAX
        for c0 in range(0, width, SEG_MAX):
            wd = min(SEG_MAX, width - c0)
            segs.append(Seg(wi, cols[wi] + c0, wd, wd if stored == width else stored, out, out_cols[out] + c0,
                            rope, scale))
        cols[wi] += width
        out_cols[out] += stored
    assert all(c == w.shape[0 if t else 1] for c, w, t in zip(cols, ws, w_transposed))
    tm = TM_PROJ
    tabs = [t for r in ropes for t in r[1:]]
    side_in, side_out, side_shapes = _side_io(sides, s // tm)
    res = pl.pallas_call(
        functools.partial(_fused_proj_kernel, segs=tuple(segs), rope_halves=tuple(r[0] for r in ropes),
                          w_transposed=tuple(w_transposed), n_side=len(sides)),
        grid=(s // tm,),
        in_specs=[pl.BlockSpec((tm, kdim), lambda i: (i, 0))] + [_resident(w.shape) for w in ws]
                 + [pl.BlockSpec((tm, LANES), lambda i: (i, 0))] * len(tabs) + side_in,
        out_specs=[pl.BlockSpec((tm, n), lambda i: (i, 0)) for n in out_cols] + side_out,
        out_shape=[jax.ShapeDtypeStruct((s, n), dt) for n, dt in zip(out_cols, out_dtypes)] + side_shapes,
        compiler_params=_cparams(("parallel",)),
        name="fused_proj",
    )(x, *ws, *tabs, *(sd.arr for sd in sides))
    return res[:len(out_dtypes)], res[len(out_dtypes):]


def _rope_tables(positions, head_dim, rot_dim):
    half = rot_dim // 2
    inv = 1.0 / (ROPE_THETA ** (jnp.arange(half, dtype=jnp.float32) / half))
    ang = positions.astype(jnp.float32)[:, None] * inv
    cos, sin = jnp.cos(ang), jnp.sin(ang)
    s = positions.shape[0]
    ones = jnp.ones((s, head_dim - rot_dim), jnp.float32)
    zeros_h = jnp.zeros((s, half), jnp.float32)
    zeros_r = jnp.zeros((s, head_dim - rot_dim), jnp.float32)
    c = jnp.concatenate([cos, cos, ones], axis=1)
    s_up = jnp.concatenate([-sin, zeros_h, zeros_r], axis=1)
    s_dn = jnp.concatenate([zeros_h, sin, zeros_r], axis=1)
    reps = LANES // head_dim
    return (half,) + tuple(jnp.tile(t, (1, reps)) for t in (c, s_up, s_dn))


def _stack_heads(q_ref, n, groups):
    return jnp.concatenate(
        [q_ref[:, (n * groups + g) * HEAD_DIM:(n * groups + g + 1) * HEAD_DIM] for g in range(groups)], axis=0)


def _store_gated(o_ref, gate_ref, out, n, groups):
    tq = o_ref.shape[0]
    for g in range(groups):
        cols = slice((n * groups + g) * HEAD_DIM, (n * groups + g + 1) * HEAD_DIM)
        gate = gate_ref[:, cols]
        o_ref[:, cols] = (out[g * tq:(g + 1) * tq, :] * (gate * jax.nn.sigmoid(gate))).astype(o_ref.dtype)


def _mixer_a_kernel(q_ref, iq_ref, iw_ref, gate_ref, ikt_ref, k_ref, v_ref, o_ref,
                    score_ref, wrep_ref, lo_ref, hi_ref, mn_ref, mx_ref,
                    qn_ref, z_ref, m_ref, acc_ref, *, n_sel, idx_scale):
    tq = q_ref.shape[0]
    tk = tq
    groups = q_ref.shape[1] // HEAD_DIM // N_KV
    i = pl.program_id(0)
    nkb2 = (i + 2) // 2
    f32, bf16 = jnp.float32, jnp.bfloat16

    row2 = lax.broadcasted_iota(jnp.int32, (tq, TK2), 0)
    col2 = lax.broadcasted_iota(jnp.int32, (tq, TK2), 1)

    for h in range(IDX_HEADS):
        wrep_ref[h] = jnp.broadcast_to(iw_ref[:, h:h + 1] * idx_scale, (tq, LANES))

    mn_ref[...] = jnp.full((tq, tk), jnp.inf, f32)
    mx_ref[...] = jnp.full((tq, tk), -jnp.inf, f32)

    def score_body(kb2, carry):
        koff = pl.multiple_of(kb2 * TK2, TK2)
        ikt = ikt_ref[:, pl.ds(koff, TK2)]
        acc = jnp.zeros((tq, TK2), f32)
        for h in range(IDX_HEADS):
            d = jnp.dot(iq_ref[:, h * IDX_DIM:(h + 1) * IDX_DIM], ikt, preferred_element_type=f32)
            w = wrep_ref[h]
            acc = acc + jnp.concatenate([w] * (TK2 // LANES), axis=1) * jnp.maximum(d, 0.0)
        causal = (col2 + kb2 * TK2) <= (row2 + i * tq)
        score_ref[:, pl.ds(koff, TK2)] = jnp.where(causal, acc, -jnp.inf)
        lo_part = jnp.where(causal, acc, jnp.inf)
        hi_part = jnp.where(causal, acc, -jnp.inf)
        mn_ref[...] = jnp.minimum(mn_ref[...], jnp.minimum(lo_part[:, :tk], lo_part[:, tk:]))
        mx_ref[...] = jnp.maximum(mx_ref[...], jnp.maximum(hi_part[:, :tk], hi_part[:, tk:]))
        return carry

    lax.fori_loop(0, nkb2, score_body, 0)

    t1 = (lax.broadcasted_iota(jnp.int32, (tq, LANES), 0) + i * tq + 1).astype(f32)
    k_t = jnp.minimum(t1, float(n_sel))
    row_min = jnp.broadcast_to(jnp.min(mn_ref[...], axis=1, keepdims=True), (tq, LANES))
    if tk >= n_sel:
        group_lo = jnp.broadcast_to(jnp.min(mx_ref[...], axis=1, keepdims=True), (tq, LANES))
        lo_ref[...] = jnp.where(t1 >= float(tk), group_lo, row_min)
    else:
        lo_ref[...] = row_min
    row_max = jnp.broadcast_to(jnp.max(mx_ref[...], axis=1, keepdims=True), (tq, LANES))
    hi_ref[...] = jnp.where(t1 <= float(n_sel), row_min, row_max)

    def unresolved():
        lo, hi = lo_ref[...], hi_ref[...]
        mid = 0.5 * lo + 0.5 * hi
        open_ = jnp.where(mid > lo, 1.0, 0.0) * jnp.where(mid < hi, 1.0, 0.0)
        return (jnp.max(open_) > 0.0).astype(jnp.int32)

    n_half = tq // LANES

    def bisect_step():
        lo, hi = lo_ref[...], hi_ref[...]
        mid = 0.5 * lo + 0.5 * hi
        accs = []
        for rh in range(n_half):
            rows = slice(rh * LANES, (rh + 1) * LANES)
            mid_h = 0.5 * lo_ref[rows, :] + 0.5 * hi_ref[rows, :]

            def count_body(kb2, acc):
                koff = pl.multiple_of(kb2 * TK2, TK2)
                for c in range(TK2 // LANES):
                    s = score_ref[rows, pl.ds(koff + c * LANES, LANES)]
                    acc = acc + jnp.where(s >= mid_h, 1.0, 0.0)
                return acc

            accs.append(lax.fori_loop(0, nkb2, count_body, jnp.zeros((LANES, LANES), f32)))
        cnt = jnp.broadcast_to(jnp.sum(jnp.concatenate(accs, axis=0), axis=1, keepdims=True), (tq, LANES))
        ge = cnt >= k_t
        lo_ref[...] = jnp.where(ge, mid, lo)
        hi_ref[...] = jnp.where(cnt == k_t, mid, jnp.where(ge, hi, mid))

    def bisect_body(state):
        for _ in range(BISECT_STEPS_PER_CHECK):
            bisect_step()
        return state[0] + BISECT_STEPS_PER_CHECK, unresolved()

    lax.while_loop(lambda st: jnp.logical_and(st[1] > 0, st[0] < MAX_BISECT), bisect_body,
                   (jnp.int32(0), unresolved()))

    thr = jnp.concatenate([lo_ref[...]] * (TK2 // LANES), axis=1)
    m_rows = groups * tq
    for n in range(N_KV):
        qn_ref[n] = _stack_heads(q_ref, n, groups)
    m_ref[...] = jnp.full((N_KV, m_rows, LANES), NEG_BIG, f32)
    acc_ref[...] = jnp.zeros((N_KV, m_rows, 2 * HEAD_DIM), f32)

    def logits(kb2, n):
        kblk = k_ref[pl.ds(pl.multiple_of(kb2 * TK2, TK2), TK2), n * HEAD_DIM:(n + 1) * HEAD_DIM]
        return lax.dot_general(qn_ref[n], kblk, (((1,), (1,)), ((), ())), preferred_element_type=f32).astype(bf16)

    for n in range(N_KV):
        z_ref[n] = logits(0, n)

    def attn_body(kb2, carry):
        koff = pl.multiple_of(kb2 * TK2, TK2)
        ahead = jnp.minimum(kb2 + 1, nkb2 - 1)
        bias = jnp.where(score_ref[:, pl.ds(koff, TK2)] >= thr, 0.0, NEG_BIG).astype(bf16)
        bias = jnp.concatenate([bias] * groups, axis=0)
        for n in range(N_KV):
            vblk = v_ref[pl.ds(koff, TK2), n * HEAD_DIM:(n + 1) * HEAD_DIM]
            vext = jnp.concatenate([vblk, jnp.ones_like(vblk)], axis=1)
            z = z_ref[n] + bias
            z_ref[n] = logits(ahead, n)
            m_old = m_ref[n]
            m_new = jnp.maximum(m_old, jnp.max(z, axis=1, keepdims=True).astype(f32))
            p = jnp.exp2(z - jnp.concatenate([m_new.astype(bf16)] * (TK2 // LANES), axis=1))
            alpha = jnp.exp2(m_old - m_new)
            acc_ref[n] = (jnp.concatenate([alpha, alpha], axis=1) * acc_ref[n]
                          + jnp.dot(p, vext, preferred_element_type=f32))
            m_ref[n] = m_new
        return carry

    lax.fori_loop(0, nkb2, attn_body, 0)
    for n in range(N_KV):
        acc = acc_ref[n]
        _store_gated(o_ref, gate_ref, acc[:, :HEAD_DIM] / acc[:, HEAD_DIM:], n, groups)


def _mixer_a(q, ix, iw, gate, ikt, k, v, n_sel):
    s, width = q.shape
    groups = width // HEAD_DIM // N_KV
    f32 = jnp.float32
    kern = functools.partial(_mixer_a_kernel, n_sel=n_sel, idx_scale=(IDX_DIM ** -0.5) * (IDX_HEADS ** -0.5))
    return pl.pallas_call(
        kern,
        grid=(s // TQ,),
        in_specs=[pl.BlockSpec((TQ, width), lambda i: (i, 0)),
                  pl.BlockSpec((TQ, IDX_HEADS * IDX_DIM), lambda i: (i, 0)),
                  pl.BlockSpec((TQ, IDX_HEADS), lambda i: (i, 0)),
                  pl.BlockSpec((TQ, width), lambda i: (i, 0)),
                  _resident(ikt.shape), _resident(k.shape), _resident(v.shape)],
        out_specs=pl.BlockSpec((TQ, width), lambda i: (i, 0)),
        out_shape=jax.ShapeDtypeStruct((s, width), jnp.bfloat16),
        scratch_shapes=[pltpu.VMEM((TQ, s + LANES), f32),
                        pltpu.VMEM((IDX_HEADS, TQ, LANES), f32),
                        pltpu.VMEM((TQ, LANES), f32),
                        pltpu.VMEM((TQ, LANES), f32),
                        pltpu.VMEM((TQ, TQ), f32),
                        pltpu.VMEM((TQ, TQ), f32),
                        pltpu.VMEM((N_KV, groups * TQ, HEAD_DIM), jnp.bfloat16),
                        pltpu.VMEM((N_KV, groups * TQ, TK2), jnp.bfloat16),
                        pltpu.VMEM((N_KV, groups * TQ, LANES), f32),
                        pltpu.VMEM((N_KV, groups * TQ, 2 * HEAD_DIM), f32)],
        compiler_params=_cparams(("parallel",)),
        name="mixer_a",
    )(q, ix, iw, gate, ikt, k, v)


def _mixer_b_kernel(q_ref, gate_ref, k_ref, v_ref, o_ref, c_ref, acc_ref):
    tq = q_ref.shape[0]
    tk = tq
    groups = q_ref.shape[1] // HEAD_DIM // N_KV
    m_rows = groups * tq
    i = pl.program_id(0)
    f32, bf16 = jnp.float32, jnp.bfloat16

    row = lax.broadcasted_iota(jnp.int32, (tq, tk), 0)
    col = lax.broadcasted_iota(jnp.int32, (tq, tk), 1)
    later = jnp.where(row > col, 1.0, 0.0).astype(bf16)
    later2 = jnp.concatenate([later, later], axis=0)
    strict = jnp.concatenate([col < row] * groups, axis=0)

    for n in range(N_KV):
        qn = _stack_heads(q_ref, n, groups)

        def step(kb, diagonal):
            koff = pl.multiple_of(kb * tk, tk)
            kblk = k_ref[pl.ds(koff, tk), n * HEAD_DIM:(n + 1) * HEAD_DIM]
            vblk = v_ref[pl.ds(koff, tk), n * HEAD_DIM:(n + 1) * HEAD_DIM]
            z = lax.dot_general(qn, kblk, (((1,), (1,)), ((), ())), preferred_element_type=f32)
            t = jnp.log(1.0 + jnp.exp2(jnp.abs(z) * (-LOG2_E)))
            sp = jnp.maximum(z, 0.0) + t
            if diagonal:
                sp = jnp.where(strict, sp, 0.0)
            sp_hi = sp.astype(bf16)
            sp_lo = (sp - sp_hi.astype(f32)).astype(bf16)
            suffix = jnp.dot(jnp.concatenate([sp_hi, sp_lo], axis=1), later2, preferred_element_type=f32)
            c_old = c_ref[...]
            total = suffix + jnp.concatenate([c_old] * (tk // LANES), axis=1)
            a = jnp.exp((jnp.minimum(z, 0.0) - t) - total)
            if diagonal:
                a = jnp.where(strict, a, 0.0)
            acc_ref[...] += jnp.dot(a.astype(bf16), vblk, preferred_element_type=f32)
            c_new = c_old + jnp.broadcast_to(jnp.sum(sp, axis=1, keepdims=True), (m_rows, LANES))
            c_ref[...] = c_new
            return (jnp.min(c_new) <= EXP_ZERO_ABOVE).astype(jnp.int32)

        c_ref[...] = jnp.zeros((m_rows, LANES), f32)
        acc_ref[...] = jnp.zeros((m_rows, HEAD_DIM), f32)
        go = step(i, True)
        lax.while_loop(lambda st: jnp.logical_and(st[0] >= 0, st[1] > 0),
                       lambda st: (st[0] - 1, step(st[0], False)), (i - 1, go))
        _store_gated(o_ref, gate_ref, acc_ref[...], n, groups)


def _mixer_b(q, gate, kv):
    s, width = q.shape
    groups = width // HEAD_DIM // N_KV
    kvw = kv.shape[1] // 2
    half_spec = lambda c: pl.BlockSpec((s, kvw), lambda i: (0, c), pipeline_mode=pl.Buffered(1))
    return pl.pallas_call(
        _mixer_b_kernel,
        grid=(s // TQ,),
        in_specs=[pl.BlockSpec((TQ, width), lambda i: (i, 0)),
                  pl.BlockSpec((TQ, width), lambda i: (i, 0)),
                  half_spec(0), half_spec(1)],
        out_specs=pl.BlockSpec((TQ, width), lambda i: (i, 0)),
        out_shape=jax.ShapeDtypeStruct((s, width), jnp.bfloat16),
        scratch_shapes=[pltpu.VMEM((groups * TQ, LANES), jnp.float32),
                        pltpu.VMEM((groups * TQ, HEAD_DIM), jnp.float32)],
        compiler_params=_cparams(("parallel",)),
        name="mixer_b",
    )(q, gate, kv, kv)


def _post_kernel(og_ref, x_ref, p_ref, wo_ref, wg_ref, wp_ref, lng_ref, lnb_ref, *refs, alpha, n_side):
    f32 = jnp.float32
    o_ref = refs[n_side]
    _cast_sides(refs[:n_side], refs[n_side + 1:])
    h = jnp.dot(og_ref[...], wo_ref[...], preferred_element_type=f32)
    y = alpha * x_ref[...] + h
    mu = jnp.mean(y, axis=-1, keepdims=True)
    d = y - mu
    var = jnp.mean(d * d, axis=-1, keepdims=True)
    yn = d * lax.rsqrt(var + LN_EPS) * lng_ref[...] + lnb_ref[...]
    gl = jnp.dot(yn.astype(jnp.bfloat16), wg_ref[...], preferred_element_type=f32)
    pe = jnp.dot(p_ref[...].astype(jnp.bfloat16), wp_ref[...], preferred_element_type=f32)
    o_ref[...] = yn + pe * jax.nn.sigmoid(gl)


def _resident_slab(arr, idx):
    return pl.BlockSpec((None,) + arr.shape[1:], lambda *_: (idx,) + (0,) * (arr.ndim - 1),
                        pipeline_mode=pl.Buffered(1))


def _post(og, x, p, layer, batch, wo, wg, wp, lng, lnb, alpha, sides=()):
    s, d = x.shape
    tm = TM_PROJ
    side_in, side_out, side_shapes = _side_io(sides, s // tm)
    res = pl.pallas_call(
        functools.partial(_post_kernel, alpha=alpha, n_side=len(sides)),
        grid=(s // tm,),
        in_specs=[pl.BlockSpec((tm, og.shape[1]), lambda i: (i, 0)),
                  pl.BlockSpec((tm, d), lambda i: (i, 0)),
                  pl.BlockSpec((None, None, tm, p.shape[3]), lambda i: (layer, batch, i, 0)),
                  _resident(wo.shape), _resident(wg.shape), _resident_slab(wp, layer),
                  _resident_slab(lng, layer), _resident_slab(lnb, layer)] + side_in,
        out_specs=[pl.BlockSpec((tm, d), lambda i: (i, 0))] + side_out,
        out_shape=[jax.ShapeDtypeStruct((s, d), jnp.float32)] + side_shapes,
        compiler_params=_cparams(("parallel",)),
        name="post",
    )(og, x, p, wo, wg, wp, lng, lnb, *(sd.arr for sd in sides))
    return res[0], res[1:]


def _now_bf16(sd):
    return (sd.arr if sd.layer < 0 else sd.arr[sd.layer]).astype(jnp.bfloat16)


def kernel(x, p, positions, w_in_a, w_o_a, w_kv_b, w_in_b, w_o_b, ln_g, ln_b, w_ple, w_ple_gate):
    bsz, s, d = x.shape
    n_a, n_b = w_in_a.shape[0], w_in_b.shape[0]
    depth = n_a + n_b
    alpha = (2.0 * depth) ** 0.25
    width = w_o_a.shape[1]
    kvw = N_KV * HEAD_DIM
    idxw = IDX_HEADS * IDX_DIM
    n_sel = min(TOPK_MAX, s // 4)
    bf16, f32 = jnp.bfloat16, jnp.float32
    sm_scale = HEAD_DIM ** -0.5
    assert s % TK2 == 0
    n_steps = s // TM_PROJ
    wp_all = w_ple.astype(bf16)
    lng, lnb = ln_g[:, None, :], ln_b[:, None, :]

    def proj_sources(layer):
        if layer < n_a:
            return [Side(jnp.swapaxes(w_in_a[layer], 0, 1), -1)]
        first_b = [Side(w_kv_b, -1)] if layer == n_a else []
        return first_b + [Side(w_in_b, layer - n_a)]

    def riders(srcs):
        return srcs if all(_side_ok(sd.arr, n_steps) for sd in srcs) else []

    proj_w, post_w = {}, {}
    outs = []
    for b in range(bsz):
        xb = x[b]
        ropes = [_rope_tables(positions[b], HEAD_DIM, ROT_DIM), _rope_tables(positions[b], IDX_DIM, IDX_ROT)]
        kv_shared = None
        for i in range(depth):
            if i not in proj_w:
                proj_w[i] = [_now_bf16(sd) for sd in proj_sources(i)]
            post_src = [Side(w_o_a, i) if i < n_a else Side(w_o_b, i - n_a), Side(w_ple_gate, i)]
            post_ride = riders(post_src) if i not in post_w else []
            if i < n_a:
                plan = [(0, width, width, 0, 0, LOG2_E * sm_scale), (0, kvw, kvw, 1, 0, 1.0),
                        (0, kvw, kvw, 2, -1, 1.0), (0, width, width, 3, -1, 1.0), (0, idxw, idxw, 4, 1, 1.0),
                        (0, IDX_DIM, LANES, 4, 1, 1.0), (0, IDX_HEADS, IDX_HEADS, 5, -1, 1.0)]
                (q, k, v, gate, ix, iw), cast = _fused_project(
                    xb, proj_w[i], [True], plan, [bf16, bf16, bf16, f32, bf16, f32], ropes, post_ride)
                ikt = ix[:, idxw:idxw + IDX_DIM].T
                og = _mixer_a(q, ix, iw, gate, ikt, k, v, n_sel)
            elif kv_shared is None:
                plan = [(0, 2 * kvw, 2 * kvw, 0, -1, 1.0), (1, width, width, 1, -1, sm_scale),
                        (1, width, width, 2, -1, 1.0)]
                (kv_shared, q, gate), cast = _fused_project(xb, proj_w[i], [False, False], plan, [bf16, bf16, f32],
                                                            [], post_ride)
                og = _mixer_b(q, gate, kv_shared)
            else:
                plan = [(0, width, width, 0, -1, sm_scale), (0, width, width, 1, -1, 1.0)]
                (q, gate), cast = _fused_project(xb, proj_w[i], [False], plan, [bf16, f32], [], post_ride)
                og = _mixer_b(q, gate, kv_shared)
            if i not in post_w:
                post_w[i] = tuple(cast) if post_ride else tuple(_now_bf16(sd) for sd in post_src)
            next_ride = riders(proj_sources(i + 1)) if i + 1 < depth and i + 1 not in proj_w else []
            xb, cast = _post(og, xb, p, i, b, post_w[i][0], post_w[i][1], wp_all, lng, lnb, alpha, next_ride)
            if next_ride:
                proj_w[i + 1] = list(cast)
        outs.append(xb)
    return jnp.stack(outs, axis=0)
```

```python
import functools
from typing import NamedTuple

import jax
import jax.numpy as jnp
from jax import lax
from jax.experimental import pallas as pl
from jax.experimental.pallas import tpu as pltpu

HEAD_DIM = 128
N_KV = 4
ROT_DIM = HEAD_DIM // 4
IDX_HEADS = 16
IDX_DIM = 64
IDX_ROT = IDX_DIM // 4
TOPK_MAX = 256
ROPE_THETA = 500000.0
LN_EPS = 1e-5
LOG2_E = 1.4426950408889634

LANES = 128
BF16_SUBLANES = 16
V7X_VMEM_BYTES = 64 * 1024 * 1024
VMEM_LIMIT = V7X_VMEM_BYTES - 8 * 1024 * 1024

TQ = 256
TK2 = 2 * TQ
TM_PROJ = 256
SEG_MAX = 1024
NEG_BIG = -1e30
EXP_ZERO_ABOVE = 110.0
MAX_BISECT = 256
BISECT_STEPS_PER_CHECK = 3


def _cparams(sem):
    return pltpu.CompilerParams(dimension_semantics=sem, vmem_limit_bytes=VMEM_LIMIT)


def _resident(shape):
    nd = len(shape)
    return pl.BlockSpec(shape, lambda *_: (0,) * nd, pipeline_mode=pl.Buffered(1))


class Seg(NamedTuple):
    w: int
    col: int
    width: int
    stored: int
    out: int
    out_col: int
    rope: int
    scale: float


def _rope_slab(y, half, c, s_up, s_dn):
    up = pltpu.roll(y, LANES - half, axis=1)
    dn = pltpu.roll(y, half, axis=1)
    return y * c + up * s_up + dn * s_dn


class Side(NamedTuple):
    arr: jax.Array
    layer: int


def _side_ok(arr, n_steps):
    rows = arr.shape[-2]
    return rows % (n_steps * BF16_SUBLANES) == 0


def _side_io(sides, n_steps):
    in_specs, out_specs, out_shapes = [], [], []
    for sd in sides:
        r, c = sd.arr.shape[-2:]
        rows = r // n_steps
        if sd.layer < 0:
            in_specs.append(pl.BlockSpec((rows, c), lambda i: (i, 0)))
        else:
            in_specs.append(pl.BlockSpec((None, rows, c), lambda i, l=sd.layer: (l, i, 0)))
        out_specs.append(pl.BlockSpec((rows, c), lambda i: (i, 0)))
        out_shapes.append(jax.ShapeDtypeStruct((r, c), jnp.bfloat16))
    return in_specs, out_specs, out_shapes


def _cast_sides(side_in, side_out):
    for src, dst in zip(side_in, side_out):
        dst[...] = src[...].astype(dst.dtype)


def _rope_slab_tables(pos, freq, head_dim, half):
    ang = pos * freq
    c, sn = jnp.cos(ang), jnp.sin(ang)
    lane = lax.broadcasted_iota(jnp.int32, ang.shape, 1) & (head_dim - 1)
    s_up = jnp.where(lane < half, -sn, 0.0)
    s_dn = jnp.where(lane < half, 0.0, jnp.where(lane < 2 * half, sn, 0.0))
    return c, s_up, s_dn


def _fused_proj_kernel(x_ref, *refs, segs, rope_dims, w_transposed, n_side):
    n_w, n_rope = len(w_transposed), 2 if rope_dims else 0
    ws = refs[:n_w]
    side_in = refs[n_w + n_rope:n_w + n_rope + n_side]
    outs = refs[n_w + n_rope + n_side:len(refs) - n_side]
    _cast_sides(side_in, refs[len(refs) - n_side:])
    if rope_dims:
        pos_ref, freq_ref = refs[n_w:n_w + 2]
        tabs = [_rope_slab_tables(pos_ref[...], freq_ref[r:r + 1, :], hd, half)
                for r, (hd, half) in enumerate(rope_dims)]
    x = x_ref[...].astype(jnp.bfloat16)
    for sg in segs:
        if w_transposed[sg.w]:
            y = lax.dot_general(x, ws[sg.w][sg.col:sg.col + sg.width, :], (((1,), (1,)), ((), ())),
                                preferred_element_type=jnp.float32)
        else:
            y = jnp.dot(x, ws[sg.w][:, sg.col:sg.col + sg.width], preferred_element_type=jnp.float32)
        if sg.scale != 1.0:
            y = y * sg.scale
        if sg.stored > sg.width:
            y = jnp.concatenate([y, jnp.zeros((y.shape[0], sg.stored - sg.width), y.dtype)], axis=1)
        if sg.rope >= 0:
            c, s_up, s_dn = tabs[sg.rope]
            slabs = [_rope_slab(y[:, j * LANES:(j + 1) * LANES], rope_dims[sg.rope][1], c, s_up, s_dn)
                     for j in range(sg.stored // LANES)]
            y = jnp.concatenate(slabs, axis=1) if len(slabs) > 1 else slabs[0]
        o_ref = outs[sg.out]
        o_ref[:, sg.out_col:sg.out_col + sg.stored] = y.astype(o_ref.dtype)


def _fused_project(x, ws, w_transposed, plan, out_dtypes, ropes, sides=()):
    s, kdim = x.shape
    segs, cols, out_cols = [], [0] * len(ws), [0] * len(out_dtypes)
    for wi, width, stored, out, rope, scale in plan:
        assert stored == width or width <= SEG_MAX
        for c0 in range(0, width, SEG_MAX):
            wd = min(SEG_MAX, width - c0)
            segs.append(Seg(wi, cols[wi] + c0, wd, wd if stored == width else stored, out, out_cols[out] + c0,
                            rope, scale))
        cols[wi] += width
        out_cols[out] += stored
    assert all(c == w.shape[0 if t else 1] for c, w, t in zip(cols, ws, w_transposed))
    tm = TM_PROJ
    rope_in, rope_specs = [], []
    if ropes is not None:
        pos, freqs, _ = ropes
        rope_in = [pos, freqs]
        rope_specs = [pl.BlockSpec((tm, 1), lambda i: (i, 0)), _resident(freqs.shape)]
    side_in, side_out, side_shapes = _side_io(sides, s // tm)
    res = pl.pallas_call(
        functools.partial(_fused_proj_kernel, segs=tuple(segs), rope_dims=ropes[2] if ropes is not None else (),
                          w_transposed=tuple(w_transposed), n_side=len(sides)),
        grid=(s // tm,),
        in_specs=[pl.BlockSpec((tm, kdim), lambda i: (i, 0))] + [_resident(w.shape) for w in ws]
                 + rope_specs + side_in,
        out_specs=[pl.BlockSpec((tm, n), lambda i: (i, 0)) for n in out_cols] + side_out,
        out_shape=[jax.ShapeDtypeStruct((s, n), dt) for n, dt in zip(out_cols, out_dtypes)] + side_shapes,
        compiler_params=_cparams(("parallel",)),
        name="fused_proj",
    )(x, *ws, *rope_in, *(sd.arr for sd in sides))
    return res[:len(out_dtypes)], res[len(out_dtypes):]


def _rope_freq_row(head_dim, rot_dim):
    half = rot_dim // 2
    inv = 1.0 / (ROPE_THETA ** (jnp.arange(half, dtype=jnp.float32) / half))
    row = jnp.concatenate([inv, inv, jnp.zeros((head_dim - rot_dim,), jnp.float32)])
    return jnp.tile(row, LANES // head_dim)


def _stack_heads(q_ref, n, groups):
    return jnp.concatenate(
        [q_ref[:, (n * groups + g) * HEAD_DIM:(n * groups + g + 1) * HEAD_DIM] for g in range(groups)], axis=0)


def _store_gated(o_ref, gate_ref, out, n, groups):
    tq = o_ref.shape[0]
    for g in range(groups):
        cols = slice((n * groups + g) * HEAD_DIM, (n * groups + g + 1) * HEAD_DIM)
        gate = gate_ref[:, cols]
        o_ref[:, cols] = (out[g * tq:(g + 1) * tq, :] * (gate * jax.nn.sigmoid(gate))).astype(o_ref.dtype)


def _mixer_a_kernel(q_ref, iq_ref, iw_ref, gate_ref, ik_ref, k_ref, v_ref, o_ref,
                    score_ref, wrep_ref, lo_ref, hi_ref, mn_ref, mx_ref,
                    qn_ref, z_ref, m_ref, acc_ref, *, n_sel, idx_scale):
    tq = q_ref.shape[0]
    tk = tq
    groups = q_ref.shape[1] // HEAD_DIM // N_KV
    i = pl.program_id(0)
    nkb2 = (i + 2) // 2
    f32, bf16 = jnp.float32, jnp.bfloat16

    row2 = lax.broadcasted_iota(jnp.int32, (tq, TK2), 0)
    col2 = lax.broadcasted_iota(jnp.int32, (tq, TK2), 1)

    for h in range(IDX_HEADS):
        wrep_ref[h] = jnp.broadcast_to(iw_ref[:, h:h + 1] * idx_scale, (tq, LANES))

    mn_ref[...] = jnp.full((tq, tk), jnp.inf, f32)
    mx_ref[...] = jnp.full((tq, tk), -jnp.inf, f32)

    def score_body(kb2, carry):
        koff = pl.multiple_of(kb2 * TK2, TK2)
        ik = ik_ref[pl.ds(koff, TK2), :IDX_DIM]
        acc = jnp.zeros((tq, TK2), f32)
        for h in range(IDX_HEADS):
            d = lax.dot_general(iq_ref[:, h * IDX_DIM:(h + 1) * IDX_DIM], ik, (((1,), (1,)), ((), ())),
                                preferred_element_type=f32)
            w = wrep_ref[h]
            acc = acc + jnp.concatenate([w] * (TK2 // LANES), axis=1) * jnp.maximum(d, 0.0)
        causal = (col2 + kb2 * TK2) <= (row2 + i * tq)
        score_ref[:, pl.ds(koff, TK2)] = jnp.where(causal, acc, -jnp.inf)
        lo_part = jnp.where(causal, acc, jnp.inf)
        hi_part = jnp.where(causal, acc, -jnp.inf)
        mn_ref[...] = jnp.minimum(mn_ref[...], jnp.minimum(lo_part[:, :tk], lo_part[:, tk:]))
        mx_ref[...] = jnp.maximum(mx_ref[...], jnp.maximum(hi_part[:, :tk], hi_part[:, tk:]))
        return carry

    lax.fori_loop(0, nkb2, score_body, 0)

    t1 = (lax.broadcasted_iota(jnp.int32, (tq, LANES), 0) + i * tq + 1).astype(f32)
    k_t = jnp.minimum(t1, float(n_sel))
    row_min = jnp.broadcast_to(jnp.min(mn_ref[...], axis=1, keepdims=True), (tq, LANES))
    if tk >= n_sel:
        group_lo = jnp.broadcast_to(jnp.min(mx_ref[...], axis=1, keepdims=True), (tq, LANES))
        lo_ref[...] = jnp.where(t1 >= float(tk), group_lo, row_min)
    else:
        lo_ref[...] = row_min
    row_max = jnp.broadcast_to(jnp.max(mx_ref[...], axis=1, keepdims=True), (tq, LANES))
    hi_ref[...] = jnp.where(t1 <= float(n_sel), row_min, row_max)

    def unresolved():
        lo, hi = lo_ref[...], hi_ref[...]
        mid = 0.5 * lo + 0.5 * hi
        open_ = jnp.where(mid > lo, 1.0, 0.0) * jnp.where(mid < hi, 1.0, 0.0)
        return (jnp.max(open_) > 0.0).astype(jnp.int32)

    n_half = tq // LANES

    def bisect_step():
        lo, hi = lo_ref[...], hi_ref[...]
        mid = 0.5 * lo + 0.5 * hi
        accs = []
        for rh in range(n_half):
            rows = slice(rh * LANES, (rh + 1) * LANES)
            mid_h = 0.5 * lo_ref[rows, :] + 0.5 * hi_ref[rows, :]

            def count_body(kb2, acc):
                koff = pl.multiple_of(kb2 * TK2, TK2)
                for c in range(TK2 // LANES):
                    s = score_ref[rows, pl.ds(koff + c * LANES, LANES)]
                    acc = acc + jnp.where(s >= mid_h, 1.0, 0.0)
                return acc

            accs.append(lax.fori_loop(0, nkb2, count_body, jnp.zeros((LANES, LANES), f32)))
        cnt = jnp.broadcast_to(jnp.sum(jnp.concatenate(accs, axis=0), axis=1, keepdims=True), (tq, LANES))
        ge = cnt >= k_t
        lo_ref[...] = jnp.where(ge, mid, lo)
        hi_ref[...] = jnp.where(cnt == k_t, mid, jnp.where(ge, hi, mid))

    def bisect_body(state):
        for _ in range(BISECT_STEPS_PER_CHECK):
            bisect_step()
        return state[0] + BISECT_STEPS_PER_CHECK, unresolved()

    lax.while_loop(lambda st: jnp.logical_and(st[1] > 0, st[0] < MAX_BISECT), bisect_body,
                   (jnp.int32(0), unresolved()))

    thr = jnp.concatenate([lo_ref[...]] * (TK2 // LANES), axis=1)
    m_rows = groups * tq
    for n in range(N_KV):
        qn_ref[n] = _stack_heads(q_ref, n, groups)
    m_ref[...] = jnp.full((N_KV, m_rows, LANES), NEG_BIG, f32)
    acc_ref[...] = jnp.zeros((N_KV, m_rows, 2 * HEAD_DIM), f32)

    def logits(kb2, n):
        kblk = k_ref[pl.ds(pl.multiple_of(kb2 * TK2, TK2), TK2), n * HEAD_DIM:(n + 1) * HEAD_DIM]
        return lax.dot_general(qn_ref[n], kblk, (((1,), (1,)), ((), ())), preferred_element_type=f32).astype(bf16)

    for n in range(N_KV):
        z_ref[n] = logits(0, n)

    def attn_body(kb2, carry):
        koff = pl.multiple_of(kb2 * TK2, TK2)
        ahead = jnp.minimum(kb2 + 1, nkb2 - 1)
        bias = jnp.where(score_ref[:, pl.ds(koff, TK2)] >= thr, 0.0, NEG_BIG).astype(bf16)
        bias = jnp.concatenate([bias] * groups, axis=0)
        for n in range(N_KV):
            vblk = v_ref[pl.ds(koff, TK2), n * HEAD_DIM:(n + 1) * HEAD_DIM]
            vext = jnp.concatenate([vblk, jnp.ones_like(vblk)], axis=1)
            z = z_ref[n] + bias
            z_ref[n] = logits(ahead, n)
            m_old = m_ref[n]
            m_new = jnp.maximum(m_old, jnp.max(z, axis=1, keepdims=True).astype(f32))
            p = jnp.exp2(z - jnp.concatenate([m_new.astype(bf16)] * (TK2 // LANES), axis=1))
            alpha = jnp.exp2(m_old - m_new)
            acc_ref[n] = (jnp.concatenate([alpha, alpha], axis=1) * acc_ref[n]
                          + jnp.dot(p, vext, preferred_element_type=f32))
            m_ref[n] = m_new
        return carry

    lax.fori_loop(0, nkb2, attn_body, 0)
    for n in range(N_KV):
        acc = acc_ref[n]
        _store_gated(o_ref, gate_ref, acc[:, :HEAD_DIM] / acc[:, HEAD_DIM:], n, groups)


def _mixer_a(q, ix, iw, gate, k, v, n_sel):
    s, width = q.shape
    groups = width // HEAD_DIM // N_KV
    f32 = jnp.float32
    idxw = IDX_HEADS * IDX_DIM
    ik_spec = pl.BlockSpec((s, LANES), lambda i: (0, idxw // LANES), pipeline_mode=pl.Buffered(1))
    kern = functools.partial(_mixer_a_kernel, n_sel=n_sel, idx_scale=(IDX_DIM ** -0.5) * (IDX_HEADS ** -0.5))
    return pl.pallas_call(
        kern,
        grid=(s // TQ,),
        in_specs=[pl.BlockSpec((TQ, width), lambda i: (i, 0)),
                  pl.BlockSpec((TQ, IDX_HEADS * IDX_DIM), lambda i: (i, 0)),
                  pl.BlockSpec((TQ, IDX_HEADS), lambda i: (i, 0)),
                  pl.BlockSpec((TQ, width), lambda i: (i, 0)),
                  ik_spec, _resident(k.shape), _resident(v.shape)],
        out_specs=pl.BlockSpec((TQ, width), lambda i: (i, 0)),
        out_shape=jax.ShapeDtypeStruct((s, width), jnp.bfloat16),
        scratch_shapes=[pltpu.VMEM((TQ, s + LANES), f32),
                        pltpu.VMEM((IDX_HEADS, TQ, LANES), f32),
                        pltpu.VMEM((TQ, LANES), f32),
                        pltpu.VMEM((TQ, LANES), f32),
                        pltpu.VMEM((TQ, TQ), f32),
                        pltpu.VMEM((TQ, TQ), f32),
                        pltpu.VMEM((N_KV, groups * TQ, HEAD_DIM), jnp.bfloat16),
                        pltpu.VMEM((N_KV, groups * TQ, TK2), jnp.bfloat16),
                        pltpu.VMEM((N_KV, groups * TQ, LANES), f32),
                        pltpu.VMEM((N_KV, groups * TQ, 2 * HEAD_DIM), f32)],
        compiler_params=_cparams(("parallel",)),
        name="mixer_a",
    )(q, ix, iw, gate, ix, k, v)


def _mixer_b_kernel(q_ref, gate_ref, k_ref, v_ref, o_ref, qn_ref, c_ref, acc_ref):
    tq = q_ref.shape[0]
    tk = tq
    groups = q_ref.shape[1] // HEAD_DIM // N_KV
    m_rows = groups * tq
    i = pl.program_id(0)
    f32, bf16 = jnp.float32, jnp.bfloat16

    row = lax.broadcasted_iota(jnp.int32, (tq, tk), 0)
    col = lax.broadcasted_iota(jnp.int32, (tq, tk), 1)
    later = jnp.where(row > col, 1.0, 0.0).astype(bf16)
    later2 = jnp.concatenate([later, later], axis=0)
    strict = jnp.concatenate([col < row] * groups, axis=0)

    for n in range(N_KV):
        qn_ref[n] = _stack_heads(q_ref, n, groups)

    def step(kb, diagonal):
        koff = pl.multiple_of(kb * tk, tk)
        c_min = None
        for n in range(N_KV):
            kblk = k_ref[pl.ds(koff, tk), n * HEAD_DIM:(n + 1) * HEAD_DIM]
            vblk = v_ref[pl.ds(koff, tk), n * HEAD_DIM:(n + 1) * HEAD_DIM]
            z = lax.dot_general(qn_ref[n], kblk, (((1,), (1,)), ((), ())), preferred_element_type=f32)
            t = jnp.log(1.0 + jnp.exp2(jnp.abs(z) * (-LOG2_E)))
            sp = jnp.maximum(z, 0.0) + t
            if diagonal:
                sp = jnp.where(strict, sp, 0.0)
            sp_hi = sp.astype(bf16)
            sp_lo = (sp - sp_hi.astype(f32)).astype(bf16)
            suffix = jnp.dot(jnp.concatenate([sp_hi, sp_lo], axis=1), later2, preferred_element_type=f32)
            c_old = c_ref[n]
            total = suffix + jnp.concatenate([c_old] * (tk // LANES), axis=1)
            a = jnp.exp((jnp.minimum(z, 0.0) - t) - total)
            if diagonal:
                a = jnp.where(strict, a, 0.0)
            acc_ref[n] += jnp.dot(a.astype(bf16), vblk, preferred_element_type=f32)
            c_new = c_old + jnp.broadcast_to(jnp.sum(sp, axis=1, keepdims=True), (m_rows, LANES))
            c_ref[n] = c_new
            c_min = c_new if c_min is None else jnp.minimum(c_min, c_new)
        return (jnp.min(c_min) <= EXP_ZERO_ABOVE).astype(jnp.int32)

    c_ref[...] = jnp.zeros((N_KV, m_rows, LANES), f32)
    acc_ref[...] = jnp.zeros((N_KV, m_rows, HEAD_DIM), f32)
    go = step(i, True)
    lax.while_loop(lambda st: jnp.logical_and(st[0] >= 0, st[1] > 0),
                   lambda st: (st[0] - 1, step(st[0], False)), (i - 1, go))
    for n in range(N_KV):
        _store_gated(o_ref, gate_ref, acc_ref[n], n, groups)


def _mixer_b(q, gate, kv):
    s, width = q.shape
    groups = width // HEAD_DIM // N_KV
    kvw = kv.shape[1] // 2
    half_spec = lambda c: pl.BlockSpec((s, kvw), lambda i: (0, c), pipeline_mode=pl.Buffered(1))
    return pl.pallas_call(
        _mixer_b_kernel,
        grid=(s // TQ,),
        in_specs=[pl.BlockSpec((TQ, width), lambda i: (i, 0)),
                  pl.BlockSpec((TQ, width), lambda i: (i, 0)),
                  half_spec(0), half_spec(1)],
        out_specs=pl.BlockSpec((TQ, width), lambda i: (i, 0)),
        out_shape=jax.ShapeDtypeStruct((s, width), jnp.bfloat16),
        scratch_shapes=[pltpu.VMEM((N_KV, groups * TQ, HEAD_DIM), jnp.bfloat16),
                        pltpu.VMEM((N_KV, groups * TQ, LANES), jnp.float32),
                        pltpu.VMEM((N_KV, groups * TQ, HEAD_DIM), jnp.float32)],
        compiler_params=_cparams(("parallel",)),
        name="mixer_b",
    )(q, gate, kv, kv)


def _post_kernel(og_ref, x_ref, p_ref, wo_ref, wg_ref, wp_ref, lng_ref, lnb_ref, *refs, alpha, n_side):
    f32 = jnp.float32
    o_ref = refs[n_side]
    _cast_sides(refs[:n_side], refs[n_side + 1:])
    h = jnp.dot(og_ref[...], wo_ref[...], preferred_element_type=f32)
    y = alpha * x_ref[...] + h
    mu = jnp.mean(y, axis=-1, keepdims=True)
    d = y - mu
    var = jnp.mean(d * d, axis=-1, keepdims=True)
    yn = d * lax.rsqrt(var + LN_EPS) * lng_ref[...] + lnb_ref[...]
    gl = jnp.dot(yn.astype(jnp.bfloat16), wg_ref[...], preferred_element_type=f32)
    pe = jnp.dot(p_ref[...].astype(jnp.bfloat16), wp_ref[...], preferred_element_type=f32)
    o_ref[...] = yn + pe * jax.nn.sigmoid(gl)


def _resident_slab(arr, idx):
    return pl.BlockSpec((None,) + arr.shape[1:], lambda *_: (idx,) + (0,) * (arr.ndim - 1),
                        pipeline_mode=pl.Buffered(1))


def _post(og, x, p, layer, batch, wo, wg, wp, lng, lnb, alpha, sides=()):
    s, d = x.shape
    tm = TM_PROJ
    side_in, side_out, side_shapes = _side_io(sides, s // tm)
    res = pl.pallas_call(
        functools.partial(_post_kernel, alpha=alpha, n_side=len(sides)),
        grid=(s // tm,),
        in_specs=[pl.BlockSpec((tm, og.shape[1]), lambda i: (i, 0)),
                  pl.BlockSpec((tm, d), lambda i: (i, 0)),
                  pl.BlockSpec((None, None, tm, p.shape[3]), lambda i: (layer, batch, i, 0)),
                  _resident(wo.shape), _resident(wg.shape), _resident_slab(wp, layer),
                  _resident_slab(lng, layer), _resident_slab(lnb, layer)] + side_in,
        out_specs=[pl.BlockSpec((tm, d), lambda i: (i, 0))] + side_out,
        out_shape=[jax.ShapeDtypeStruct((s, d), jnp.float32)] + side_shapes,
        compiler_params=_cparams(("parallel",)),
        name="post",
    )(og, x, p, wo, wg, wp, lng, lnb, *(sd.arr for sd in sides))
    return res[0], res[1:]


def _now_bf16(sd):
    return (sd.arr if sd.layer < 0 else sd.arr[sd.layer]).astype(jnp.bfloat16)


def kernel(x, p, positions, w_in_a, w_o_a, w_kv_b, w_in_b, w_o_b, ln_g, ln_b, w_ple, w_ple_gate):
    bsz, s, d = x.shape
    n_a, n_b = w_in_a.shape[0], w_in_b.shape[0]
    depth = n_a + n_b
    alpha = (2.0 * depth) ** 0.25
    width = w_o_a.shape[1]
    kvw = N_KV * HEAD_DIM
    idxw = IDX_HEADS * IDX_DIM
    n_sel = min(TOPK_MAX, s // 4)
    bf16, f32 = jnp.bfloat16, jnp.float32
    sm_scale = HEAD_DIM ** -0.5
    assert s % TK2 == 0
    n_steps = s // TM_PROJ
    wp_all = w_ple.astype(bf16)
    lng, lnb = ln_g[:, None, :], ln_b[:, None, :]

    def proj_sources(layer):
        if layer < n_a:
            return [Side(jnp.swapaxes(w_in_a[layer], 0, 1), -1)]
        first_b = [Side(w_kv_b, -1)] if layer == n_a else []
        return first_b + [Side(w_in_b, layer - n_a)]

    def riders(srcs):
        return srcs if all(_side_ok(sd.arr, n_steps) for sd in srcs) else []

    proj_w, post_w = {}, {}
    outs = []
    for b in range(bsz):
        xb = x[b]
        ropes = (positions[b].astype(f32)[:, None],
                 jnp.stack([_rope_freq_row(HEAD_DIM, ROT_DIM), _rope_freq_row(IDX_DIM, IDX_ROT)]),
                 ((HEAD_DIM, ROT_DIM // 2), (IDX_DIM, IDX_ROT // 2)))
        kv_shared = None
        for i in range(depth):
            if i not in proj_w:
                proj_w[i] = [_now_bf16(sd) for sd in proj_sources(i)]
            post_src = [Side(w_o_a, i) if i < n_a else Side(w_o_b, i - n_a), Side(w_ple_gate, i)]
            post_ride = riders(post_src) if i not in post_w else []
            if i < n_a:
                plan = [(0, width, width, 0, 0, LOG2_E * sm_scale), (0, kvw, kvw, 1, 0, 1.0),
                        (0, kvw, kvw, 2, -1, 1.0), (0, width, width, 3, -1, 1.0), (0, idxw, idxw, 4, 1, 1.0),
                        (0, IDX_DIM, LANES, 4, 1, 1.0), (0, IDX_HEADS, IDX_HEADS, 5, -1, 1.0)]
                (q, k, v, gate, ix, iw), cast = _fused_project(
                    xb, proj_w[i], [True], plan, [bf16, bf16, bf16, f32, bf16, f32], ropes, post_ride)
                og = _mixer_a(q, ix, iw, gate, k, v, n_sel)
            elif kv_shared is None:
                plan = [(0, 2 * kvw, 2 * kvw, 0, -1, 1.0), (1, width, width, 1, -1, sm_scale),
                        (1, width, width, 2, -1, 1.0)]
                (kv_shared, q, gate), cast = _fused_project(xb, proj_w[i], [False, False], plan, [bf16, bf16, f32],
                                                            None, post_ride)
                og = _mixer_b(q, gate, kv_shared)
            else:
                plan = [(0, width, width, 0, -1, sm_scale), (0, width, width, 1, -1, 1.0)]
                (q, gate), cast = _fused_project(xb, proj_w[i], [False], plan, [bf16, f32], None, post_ride)
                og = _mixer_b(q, gate, kv_shared)
            if i not in post_w:
                post_w[i] = tuple(cast) if post_ride else tuple(_now_bf16(sd) for sd in post_src)
            next_ride = riders(proj_sources(i + 1)) if i + 1 < depth and i + 1 not in proj_w else []
            xb, cast = _post(og, xb, p, i, b, post_w[i][0], post_w[i][1], wp_all, lng, lnb, alpha, next_ride)
            if next_ride:
                proj_w[i + 1] = list(cast)
        outs.append(xb)
    return jnp.stack(outs, axis=0)
```

```python
import functools
from typing import NamedTuple

import jax
import jax.numpy as jnp
from jax import lax
from jax.experimental import pallas as pl
from jax.experimental.pallas import tpu as pltpu

HEAD_DIM = 128
N_KV = 4
ROT_DIM = HEAD_DIM // 4
IDX_HEADS = 16
IDX_DIM = 64
IDX_ROT = IDX_DIM // 4
TOPK_MAX = 256
ROPE_THETA = 500000.0
LN_EPS = 1e-5
LOG2_E = 1.4426950408889634

LANES = 128
BF16_SUBLANES = 16
V7X_VMEM_BYTES = 64 * 1024 * 1024
VMEM_LIMIT = V7X_VMEM_BYTES - 8 * 1024 * 1024

TQ = 256
TK2 = 2 * TQ
TM_PROJ = 256
SEG_MAX = 1024
NEG_BIG = -1e30
EXP_ZERO_ABOVE = 110.0
MAX_BISECT = 256
BISECT_STEPS_PER_CHECK = 3


def _cparams(sem):
    return pltpu.CompilerParams(dimension_semantics=sem, vmem_limit_bytes=VMEM_LIMIT)


def _resident(shape):
    nd = len(shape)
    return pl.BlockSpec(shape, lambda *_: (0,) * nd, pipeline_mode=pl.Buffered(1))


class Seg(NamedTuple):
    w: int
    col: int
    width: int
    stored: int
    out: int
    out_col: int
    rope: int
    scale: float


def _rope_slab(y, half, c, s_up, s_dn):
    up = pltpu.roll(y, LANES - half, axis=1)
    dn = pltpu.roll(y, half, axis=1)
    return y * c + up * s_up + dn * s_dn


class Side(NamedTuple):
    arr: jax.Array
    layer: int


def _side_ok(arr, n_steps):
    rows = arr.shape[-2]
    return rows % (n_steps * BF16_SUBLANES) == 0


def _side_io(sides, n_steps):
    in_specs, out_specs, out_shapes = [], [], []
    for sd in sides:
        r, c = sd.arr.shape[-2:]
        rows = r // n_steps
        if sd.layer < 0:
            in_specs.append(pl.BlockSpec((rows, c), lambda i: (i, 0)))
        else:
            in_specs.append(pl.BlockSpec((None, rows, c), lambda i, l=sd.layer: (l, i, 0)))
        out_specs.append(pl.BlockSpec((rows, c), lambda i: (i, 0)))
        out_shapes.append(jax.ShapeDtypeStruct((r, c), jnp.bfloat16))
    return in_specs, out_specs, out_shapes


def _cast_sides(side_in, side_out):
    for src, dst in zip(side_in, side_out):
        dst[...] = src[...].astype(dst.dtype)


def _rope_slab_tables(pos, freq, head_dim, half):
    ang = pos * freq
    c, sn = jnp.cos(ang), jnp.sin(ang)
    lane = lax.broadcasted_iota(jnp.int32, ang.shape, 1) & (head_dim - 1)
    s_up = jnp.where(lane < half, -sn, 0.0)
    s_dn = jnp.where(lane < half, 0.0, jnp.where(lane < 2 * half, sn, 0.0))
    return c, s_up, s_dn


def _fused_proj_kernel(x_ref, *refs, segs, rope_dims, w_transposed, n_side, t_outs):
    n_w, n_rope = len(w_transposed), 2 if rope_dims else 0
    ws = refs[:n_w]
    side_in = refs[n_w + n_rope:n_w + n_rope + n_side]
    outs = refs[n_w + n_rope + n_side:len(refs) - n_side]
    _cast_sides(side_in, refs[len(refs) - n_side:])
    if rope_dims:
        pos_ref, freq_ref = refs[n_w:n_w + 2]
        tabs = [_rope_slab_tables(pos_ref[...], freq_ref[r:r + 1, :], hd, half)
                for r, (hd, half) in enumerate(rope_dims)]
    x = x_ref[...].astype(jnp.bfloat16)
    for sg in segs:
        if sg.out in t_outs:
            o_ref = outs[sg.out]
            yt = lax.dot_general(ws[sg.w][sg.col:sg.col + sg.width, :], x, (((1,), (1,)), ((), ())),
                                 preferred_element_type=jnp.float32)
            o_ref[sg.out_col:sg.out_col + sg.width, :] = yt.astype(o_ref.dtype)
            continue
        if w_transposed[sg.w]:
            y = lax.dot_general(x, ws[sg.w][sg.col:sg.col + sg.width, :], (((1,), (1,)), ((), ())),
                                preferred_element_type=jnp.float32)
        else:
            y = jnp.dot(x, ws[sg.w][:, sg.col:sg.col + sg.width], preferred_element_type=jnp.float32)
        if sg.scale != 1.0:
            y = y * sg.scale
        if sg.stored > sg.width:
            y = jnp.concatenate([y, jnp.zeros((y.shape[0], sg.stored - sg.width), y.dtype)], axis=1)
        if sg.rope >= 0:
            c, s_up, s_dn = tabs[sg.rope]
            slabs = [_rope_slab(y[:, j * LANES:(j + 1) * LANES], rope_dims[sg.rope][1], c, s_up, s_dn)
                     for j in range(sg.stored // LANES)]
            y = jnp.concatenate(slabs, axis=1) if len(slabs) > 1 else slabs[0]
        o_ref = outs[sg.out]
        o_ref[:, sg.out_col:sg.out_col + sg.stored] = y.astype(o_ref.dtype)


def _fused_project(x, ws, w_transposed, plan, out_dtypes, ropes, sides=(), t_outs=()):
    s, kdim = x.shape
    segs, cols, out_cols = [], [0] * len(ws), [0] * len(out_dtypes)
    for wi, width, stored, out, rope, scale in plan:
        assert stored == width or width <= SEG_MAX
        for c0 in range(0, width, SEG_MAX):
            wd = min(SEG_MAX, width - c0)
            segs.append(Seg(wi, cols[wi] + c0, wd, wd if stored == width else stored, out, out_cols[out] + c0,
                            rope, scale))
        cols[wi] += width
        out_cols[out] += stored
    assert all(c == w.shape[0 if t else 1] for c, w, t in zip(cols, ws, w_transposed))
    tm = TM_PROJ
    rope_in, rope_specs = [], []
    if ropes is not None:
        pos, freqs, _ = ropes
        rope_in = [pos, freqs]
        rope_specs = [pl.BlockSpec((tm, 1), lambda i: (i, 0)), _resident(freqs.shape)]
    side_in, side_out, side_shapes = _side_io(sides, s // tm)
    res = pl.pallas_call(
        functools.partial(_fused_proj_kernel, segs=tuple(segs), rope_dims=ropes[2] if ropes is not None else (),
                          w_transposed=tuple(w_transposed), n_side=len(sides), t_outs=tuple(t_outs)),
        grid=(s // tm,),
        in_specs=[pl.BlockSpec((tm, kdim), lambda i: (i, 0))] + [_resident(w.shape) for w in ws]
                 + rope_specs + side_in,
        out_specs=[pl.BlockSpec((n, tm), lambda i: (0, i)) if o in t_outs else pl.BlockSpec((tm, n), lambda i: (i, 0))
                   for o, n in enumerate(out_cols)] + side_out,
        out_shape=[jax.ShapeDtypeStruct((n, s) if o in t_outs else (s, n), dt)
                   for o, (n, dt) in enumerate(zip(out_cols, out_dtypes))] + side_shapes,
        compiler_params=_cparams(("parallel",)),
        name="fused_proj",
    )(x, *ws, *rope_in, *(sd.arr for sd in sides))
    return res[:len(out_dtypes)], res[len(out_dtypes):]


def _rope_freq_row(head_dim, rot_dim):
    half = rot_dim // 2
    inv = 1.0 / (ROPE_THETA ** (jnp.arange(half, dtype=jnp.float32) / half))
    row = jnp.concatenate([inv, inv, jnp.zeros((head_dim - rot_dim,), jnp.float32)])
    return jnp.tile(row, LANES // head_dim)


def _stack_heads(q_ref, n, groups):
    return jnp.concatenate(
        [q_ref[:, (n * groups + g) * HEAD_DIM:(n * groups + g + 1) * HEAD_DIM] for g in range(groups)], axis=0)


def _store_gated(o_ref, gate_ref, out, n, groups):
    tq = o_ref.shape[0]
    for g in range(groups):
        cols = slice((n * groups + g) * HEAD_DIM, (n * groups + g + 1) * HEAD_DIM)
        gate = gate_ref[:, cols]
        o_ref[:, cols] = (out[g * tq:(g + 1) * tq, :] * (gate * jax.nn.sigmoid(gate))).astype(o_ref.dtype)


def _mixer_a_kernel(q_ref, iq_ref, iw_ref, gate_ref, ik_ref, k_ref, v_ref, o_ref,
                    score_ref, wrep_ref, lo_ref, hi_ref, mn_ref, mx_ref,
                    qn_ref, z_ref, m_ref, acc_ref, *, n_sel, idx_scale):
    tq = q_ref.shape[0]
    tk = tq
    groups = q_ref.shape[1] // HEAD_DIM // N_KV
    i = pl.program_id(0)
    nkb2 = (i + 2) // 2
    f32, bf16 = jnp.float32, jnp.bfloat16

    row2 = lax.broadcasted_iota(jnp.int32, (tq, TK2), 0)
    col2 = lax.broadcasted_iota(jnp.int32, (tq, TK2), 1)

    for h in range(IDX_HEADS):
        wrep_ref[h] = jnp.broadcast_to(iw_ref[:, h:h + 1] * idx_scale, (tq, LANES))

    mn_ref[...] = jnp.full((tq, tk), jnp.inf, f32)
    mx_ref[...] = jnp.full((tq, tk), -jnp.inf, f32)

    def score_body(kb2, carry):
        koff = pl.multiple_of(kb2 * TK2, TK2)
        ik = ik_ref[pl.ds(koff, TK2), :IDX_DIM]
        acc = jnp.zeros((tq, TK2), f32)
        for h in range(IDX_HEADS):
            d = lax.dot_general(iq_ref[:, h * IDX_DIM:(h + 1) * IDX_DIM], ik, (((1,), (1,)), ((), ())),
                                preferred_element_type=f32)
            w = wrep_ref[h]
            acc = acc + jnp.concatenate([w] * (TK2 // LANES), axis=1) * jnp.maximum(d, 0.0)
        causal = (col2 + kb2 * TK2) <= (row2 + i * tq)
        score_ref[:, pl.ds(koff, TK2)] = jnp.where(causal, acc, -jnp.inf)
        lo_part = jnp.where(causal, acc, jnp.inf)
        hi_part = jnp.where(causal, acc, -jnp.inf)
        mn_ref[...] = jnp.minimum(mn_ref[...], jnp.minimum(lo_part[:, :tk], lo_part[:, tk:]))
        mx_ref[...] = jnp.maximum(mx_ref[...], jnp.maximum(hi_part[:, :tk], hi_part[:, tk:]))
        return carry

    lax.fori_loop(0, nkb2, score_body, 0)

    t1 = (lax.broadcasted_iota(jnp.int32, (tq, LANES), 0) + i * tq + 1).astype(f32)
    k_t = jnp.minimum(t1, float(n_sel))
    row_min = jnp.broadcast_to(jnp.min(mn_ref[...], axis=1, keepdims=True), (tq, LANES))
    if tk >= n_sel:
        group_lo = jnp.broadcast_to(jnp.min(mx_ref[...], axis=1, keepdims=True), (tq, LANES))
        lo_ref[...] = jnp.where(t1 >= float(tk), group_lo, row_min)
    else:
        lo_ref[...] = row_min
    row_max = jnp.broadcast_to(jnp.max(mx_ref[...], axis=1, keepdims=True), (tq, LANES))
    hi_ref[...] = jnp.where(t1 <= float(n_sel), row_min, row_max)

    def unresolved():
        lo, hi = lo_ref[...], hi_ref[...]
        mid = 0.5 * lo + 0.5 * hi
        open_ = jnp.where(mid > lo, 1.0, 0.0) * jnp.where(mid < hi, 1.0, 0.0)
        return (jnp.max(open_) > 0.0).astype(jnp.int32)

    n_half = tq // LANES

    def bisect_step():
        lo, hi = lo_ref[...], hi_ref[...]
        mid = 0.5 * lo + 0.5 * hi
        accs = []
        for rh in range(n_half):
            rows = slice(rh * LANES, (rh + 1) * LANES)
            mid_h = 0.5 * lo_ref[rows, :] + 0.5 * hi_ref[rows, :]

            def count_body(kb2, acc):
                koff = pl.multiple_of(kb2 * TK2, TK2)
                for c in range(TK2 // LANES):
                    s = score_ref[rows, pl.ds(koff + c * LANES, LANES)]
                    acc = acc + jnp.where(s >= mid_h, 1.0, 0.0)
                return acc

            accs.append(lax.fori_loop(0, nkb2, count_body, jnp.zeros((LANES, LANES), f32)))
        cnt = jnp.broadcast_to(jnp.sum(jnp.concatenate(accs, axis=0), axis=1, keepdims=True), (tq, LANES))
        ge = cnt >= k_t
        lo_ref[...] = jnp.where(ge, mid, lo)
        hi_ref[...] = jnp.where(cnt == k_t, mid, jnp.where(ge, hi, mid))

    def bisect_body(state):
        for _ in range(BISECT_STEPS_PER_CHECK):
            bisect_step()
        return state[0] + BISECT_STEPS_PER_CHECK, unresolved()

    lax.while_loop(lambda st: jnp.logical_and(st[1] > 0, st[0] < MAX_BISECT), bisect_body,
                   (jnp.int32(0), unresolved()))

    thr = jnp.concatenate([lo_ref[...]] * (TK2 // LANES), axis=1)
    m_rows = groups * tq
    for n in range(N_KV):
        qn_ref[n] = _stack_heads(q_ref, n, groups)
    m_ref[...] = jnp.full((N_KV, m_rows, LANES), NEG_BIG, f32)
    acc_ref[...] = jnp.zeros((N_KV, m_rows, 2 * HEAD_DIM), f32)

    def logits(kb2, n):
        kblk = k_ref[pl.ds(pl.multiple_of(kb2 * TK2, TK2), TK2), n * HEAD_DIM:(n + 1) * HEAD_DIM]
        return lax.dot_general(qn_ref[n], kblk, (((1,), (1,)), ((), ())), preferred_element_type=f32).astype(bf16)

    for n in range(N_KV):
        z_ref[n] = logits(0, n)

    def attn_body(kb2, carry):
        koff = pl.multiple_of(kb2 * TK2, TK2)
        ahead = jnp.minimum(kb2 + 1, nkb2 - 1)
        bias = jnp.where(score_ref[:, pl.ds(koff, TK2)] >= thr, 0.0, NEG_BIG).astype(bf16)
        bias = jnp.concatenate([bias] * groups, axis=0)
        for n in range(N_KV):
            vblk = v_ref[pl.ds(koff, TK2), n * HEAD_DIM:(n + 1) * HEAD_DIM]
            vext = jnp.concatenate([vblk, jnp.ones_like(vblk)], axis=1)
            z = z_ref[n] + bias
            z_ref[n] = logits(ahead, n)
            m_old = m_ref[n]
            m_new = jnp.maximum(m_old, jnp.max(z, axis=1, keepdims=True).astype(f32))
            p = jnp.exp2(z - jnp.concatenate([m_new.astype(bf16)] * (TK2 // LANES), axis=1))
            alpha = jnp.exp2(m_old - m_new)
            acc_ref[n] = (jnp.concatenate([alpha, alpha], axis=1) * acc_ref[n]
                          + jnp.dot(p, vext, preferred_element_type=f32))
            m_ref[n] = m_new
        return carry

    lax.fori_loop(0, nkb2, attn_body, 0)
    for n in range(N_KV):
        acc = acc_ref[n]
        _store_gated(o_ref, gate_ref, acc[:, :HEAD_DIM] / acc[:, HEAD_DIM:], n, groups)


def _mixer_a(q, ix, iw, gate, k, v, n_sel):
    s, width = q.shape
    groups = width // HEAD_DIM // N_KV
    f32 = jnp.float32
    idxw = IDX_HEADS * IDX_DIM
    ik_spec = pl.BlockSpec((s, LANES), lambda i: (0, idxw // LANES), pipeline_mode=pl.Buffered(1))
    kern = functools.partial(_mixer_a_kernel, n_sel=n_sel, idx_scale=(IDX_DIM ** -0.5) * (IDX_HEADS ** -0.5))
    return pl.pallas_call(
        kern,
        grid=(s // TQ,),
        in_specs=[pl.BlockSpec((TQ, width), lambda i: (i, 0)),
                  pl.BlockSpec((TQ, IDX_HEADS * IDX_DIM), lambda i: (i, 0)),
                  pl.BlockSpec((TQ, IDX_HEADS), lambda i: (i, 0)),
                  pl.BlockSpec((TQ, width), lambda i: (i, 0)),
                  ik_spec, _resident(k.shape), _resident(v.shape)],
        out_specs=pl.BlockSpec((TQ, width), lambda i: (i, 0)),
        out_shape=jax.ShapeDtypeStruct((s, width), jnp.bfloat16),
        scratch_shapes=[pltpu.VMEM((TQ, s + LANES), f32),
                        pltpu.VMEM((IDX_HEADS, TQ, LANES), f32),
                        pltpu.VMEM((TQ, LANES), f32),
                        pltpu.VMEM((TQ, LANES), f32),
                        pltpu.VMEM((TQ, TQ), f32),
                        pltpu.VMEM((TQ, TQ), f32),
                        pltpu.VMEM((N_KV, groups * TQ, HEAD_DIM), jnp.bfloat16),
                        pltpu.VMEM((N_KV, groups * TQ, TK2), jnp.bfloat16),
                        pltpu.VMEM((N_KV, groups * TQ, LANES), f32),
                        pltpu.VMEM((N_KV, groups * TQ, 2 * HEAD_DIM), f32)],
        compiler_params=_cparams(("parallel",)),
        name="mixer_a",
    )(q, ix, iw, gate, ix, k, v)


ONES_ROWS = BF16_SUBLANES
COUNT_ROWS = 32


def _mixer_at_kernel(q_ref, iq_ref, iw_ref, gate_ref, ik_ref, k_ref, vt_ref, o_ref,
                     score_ref, wt_ref, lo_ref, hi_ref, mn_ref, mx_ref,
                     qn_ref, z_ref, m_ref, acc_ref, *, n_sel, idx_scale):
    tq = q_ref.shape[0]
    groups = q_ref.shape[1] // HEAD_DIM // N_KV
    m_cols = groups * tq
    i = pl.program_id(0)
    nkb2 = (i + 2) // 2
    f32, bf16 = jnp.float32, jnp.bfloat16
    nt = (((1,), (1,)), ((), ()))

    key_in_blk = lax.broadcasted_iota(jnp.int32, (TK2, tq), 0)
    q_pos = lax.broadcasted_iota(jnp.int32, (TK2, tq), 1) + i * tq

    iw = iw_ref[...] * idx_scale
    wt_ref[...] = jnp.concatenate([iw, jnp.zeros((tq, LANES - IDX_HEADS), f32)], axis=1).T

    mn_ref[...] = jnp.full((tq, tq), jnp.inf, f32)
    mx_ref[...] = jnp.full((tq, tq), -jnp.inf, f32)

    def score_body(kb2, carry):
        koff = pl.multiple_of(kb2 * TK2, TK2)
        ik = ik_ref[pl.ds(koff, TK2), :IDX_DIM]
        acc = jnp.zeros((TK2, tq), f32)
        for h in range(IDX_HEADS):
            d = lax.dot_general(ik, iq_ref[:, h * IDX_DIM:(h + 1) * IDX_DIM], nt, preferred_element_type=f32)
            acc = acc + wt_ref[h:h + 1, :] * jnp.maximum(d, 0.0)
        causal = (key_in_blk + kb2 * TK2) <= q_pos
        sc = jnp.where(causal, acc, -jnp.inf)
        score_ref[pl.ds(koff, TK2), :] = sc
        lo_part = jnp.where(causal, acc, jnp.inf)
        mn_ref[...] = jnp.minimum(mn_ref[...], jnp.minimum(lo_part[:tq], lo_part[tq:]))
        mx_ref[...] = jnp.maximum(mx_ref[...], jnp.maximum(sc[:tq], sc[tq:]))
        return carry

    lax.fori_loop(0, nkb2, score_body, 0)

    t1 = (lax.broadcasted_iota(jnp.int32, (1, tq), 1) + i * tq + 1).astype(f32)
    k_t = jnp.minimum(t1, float(n_sel))
    row_min = jnp.min(mn_ref[...], axis=0, keepdims=True)
    row_max = jnp.max(mx_ref[...], axis=0, keepdims=True)
    if tq >= n_sel:
        lo0 = jnp.where(t1 >= float(tq), jnp.min(mx_ref[...], axis=0, keepdims=True), row_min)
    else:
        lo0 = row_min
    lo_ref[...] = jnp.broadcast_to(lo0, lo_ref.shape)
    hi_ref[...] = jnp.broadcast_to(jnp.where(t1 <= float(n_sel), row_min, row_max), hi_ref.shape)

    def unresolved():
        lo, hi = lo_ref[0:1, :], hi_ref[0:1, :]
        mid = 0.5 * lo + 0.5 * hi
        open_ = jnp.where(mid > lo, 1.0, 0.0) * jnp.where(mid < hi, 1.0, 0.0)
        return (jnp.max(open_) > 0.0).astype(jnp.int32)

    def bisect_step():
        lo, hi = lo_ref[0:1, :], hi_ref[0:1, :]
        mid = 0.5 * lo + 0.5 * hi

        def count_body(kb2, acc):
            koff = pl.multiple_of(kb2 * TK2, TK2)
            ind = jnp.where(score_ref[pl.ds(koff, TK2), :] >= mid, 1.0, 0.0)
            return acc + jnp.sum(ind.reshape(TK2 // COUNT_ROWS, COUNT_ROWS, tq), axis=0)

        acc = lax.fori_loop(0, nkb2, count_body, jnp.zeros((COUNT_ROWS, tq), f32))
        cnt = jnp.sum(acc, axis=0, keepdims=True)
        ge = cnt >= k_t
        lo_ref[...] = jnp.broadcast_to(jnp.where(ge, mid, lo), lo_ref.shape)
        hi_ref[...] = jnp.broadcast_to(jnp.where(cnt == k_t, mid, jnp.where(ge, hi, mid)), hi_ref.shape)

    def bisect_body(state):
        for _ in range(BISECT_STEPS_PER_CHECK):
            bisect_step()
        return state[0] + BISECT_STEPS_PER_CHECK, unresolved()

    lax.while_loop(lambda st: jnp.logical_and(st[1] > 0, st[0] < MAX_BISECT), bisect_body,
                   (jnp.int32(0), unresolved()))

    thr = lo_ref[0:1, :]
    for n in range(N_KV):
        qn_ref[n] = _stack_heads(q_ref, n, groups)
    m_ref[...] = jnp.full(m_ref.shape, NEG_BIG, f32)
    acc_ref[...] = jnp.zeros(acc_ref.shape, f32)
    ones_rows = jnp.ones((ONES_ROWS, TK2), bf16)

    def logits(kb2, n):
        kblk = k_ref[pl.ds(pl.multiple_of(kb2 * TK2, TK2), TK2), n * HEAD_DIM:(n + 1) * HEAD_DIM]
        return lax.dot_general(kblk, qn_ref[n], nt, preferred_element_type=f32).astype(bf16)

    for n in range(N_KV):
        z_ref[n] = logits(0, n)

    def attn_body(kb2, carry):
        koff = pl.multiple_of(kb2 * TK2, TK2)
        ahead = jnp.minimum(kb2 + 1, nkb2 - 1)
        bias = jnp.where(score_ref[pl.ds(koff, TK2), :] >= thr, 0.0, NEG_BIG).astype(bf16)
        bias = jnp.concatenate([bias] * groups, axis=1)
        for n in range(N_KV):
            vext = jnp.concatenate([vt_ref[n * HEAD_DIM:(n + 1) * HEAD_DIM, pl.ds(koff, TK2)], ones_rows], axis=0)
            z = z_ref[n] + bias
            z_ref[n] = logits(ahead, n)
            m_old = m_ref[n, 0:1, :]
            m_new = jnp.maximum(m_old, jnp.max(z, axis=0, keepdims=True).astype(f32))
            p = jnp.exp2(z - m_new.astype(bf16))
            alpha = jnp.exp2(m_old - m_new)
            acc_ref[n] = alpha * acc_ref[n] + jnp.dot(vext, p, preferred_element_type=f32)
            m_ref[n] = jnp.broadcast_to(m_new, m_ref.shape[1:])
        return carry

    lax.fori_loop(0, nkb2, attn_body, 0)
    for n in range(N_KV):
        acc = acc_ref[n]
        out_t = acc[:HEAD_DIM] / acc[HEAD_DIM:HEAD_DIM + 1]
        _store_gated(o_ref, gate_ref, out_t.T, n, groups)


def _mixer_at(q, ix, iw, gate, k, vt, n_sel):
    s, width = q.shape
    groups = width // HEAD_DIM // N_KV
    f32 = jnp.float32
    idxw = IDX_HEADS * IDX_DIM
    ik_spec = pl.BlockSpec((s, LANES), lambda i: (0, idxw // LANES), pipeline_mode=pl.Buffered(1))
    kern = functools.partial(_mixer_at_kernel, n_sel=n_sel, idx_scale=(IDX_DIM ** -0.5) * (IDX_HEADS ** -0.5))
    return pl.pallas_call(
        kern,
        grid=(s // TQ,),
        in_specs=[pl.BlockSpec((TQ, width), lambda i: (i, 0)),
                  pl.BlockSpec((TQ, IDX_HEADS * IDX_DIM), lambda i: (i, 0)),
                  pl.BlockSpec((TQ, IDX_HEADS), lambda i: (i, 0)),
                  pl.BlockSpec((TQ, width), lambda i: (i, 0)),
                  ik_spec, _resident(k.shape), _resident(vt.shape)],
        out_specs=pl.BlockSpec((TQ, width), lambda i: (i, 0)),
        out_shape=jax.ShapeDtypeStruct((s, width), jnp.bfloat16),
        scratch_shapes=[pltpu.VMEM((s, TQ), f32),
                        pltpu.VMEM((LANES, TQ), f32),
                        pltpu.VMEM((8, TQ), f32),
                        pltpu.VMEM((8, TQ), f32),
                        pltpu.VMEM((TQ, TQ), f32),
                        pltpu.VMEM((TQ, TQ), f32),
                        pltpu.VMEM((N_KV, groups * TQ, HEAD_DIM), jnp.bfloat16),
                        pltpu.VMEM((N_KV, TK2, groups * TQ), jnp.bfloat16),
                        pltpu.VMEM((N_KV, 8, groups * TQ), f32),
                        pltpu.VMEM((N_KV, HEAD_DIM + ONES_ROWS, groups * TQ), f32)],
        compiler_params=_cparams(("parallel",)),
        name="mixer_a",
    )(q, ix, iw, gate, ix, k, vt)


def _mixer_b_kernel(q_ref, gate_ref, k_ref, v_ref, o_ref, qn_ref, c_ref, acc_ref):
    tq = q_ref.shape[0]
    tk = tq
    groups = q_ref.shape[1] // HEAD_DIM // N_KV
    m_rows = groups * tq
    i = pl.program_id(0)
    f32, bf16 = jnp.float32, jnp.bfloat16

    row = lax.broadcasted_iota(jnp.int32, (tq, tk), 0)
    col = lax.broadcasted_iota(jnp.int32, (tq, tk), 1)
    later = jnp.where(row > col, 1.0, 0.0).astype(bf16)
    later2 = jnp.concatenate([later, later], axis=0)
    strict = jnp.concatenate([col < row] * groups, axis=0)

    for n in range(N_KV):
        qn_ref[n] = _stack_heads(q_ref, n, groups)

    def step(kb, diagonal):
        koff = pl.multiple_of(kb * tk, tk)
        c_min = None
        for n in range(N_KV):
            kblk = k_ref[pl.ds(koff, tk), n * HEAD_DIM:(n + 1) * HEAD_DIM]
            vblk = v_ref[pl.ds(koff, tk), n * HEAD_DIM:(n + 1) * HEAD_DIM]
            z = lax.dot_general(qn_ref[n], kblk, (((1,), (1,)), ((), ())), preferred_element_type=f32)
            t = jnp.log(1.0 + jnp.exp2(jnp.abs(z) * (-LOG2_E)))
            sp = jnp.maximum(z, 0.0) + t
            if diagonal:
                sp = jnp.where(strict, sp, 0.0)
            sp_hi = sp.astype(bf16)
            sp_lo = (sp - sp_hi.astype(f32)).astype(bf16)
            suffix = jnp.dot(jnp.concatenate([sp_hi, sp_lo], axis=1), later2, preferred_element_type=f32)
            c_old = c_ref[n]
            total = suffix + jnp.concatenate([c_old] * (tk // LANES), axis=1)
            a = jnp.exp((jnp.minimum(z, 0.0) - t) - total)
            if diagonal:
                a = jnp.where(strict, a, 0.0)
            acc_ref[n] += jnp.dot(a.astype(bf16), vblk, preferred_element_type=f32)
            c_new = c_old + jnp.broadcast_to(jnp.sum(sp, axis=1, keepdims=True), (m_rows, LANES))
            c_ref[n] = c_new
            c_min = c_new if c_min is None else jnp.minimum(c_min, c_new)
        return (jnp.min(c_min) <= EXP_ZERO_ABOVE).astype(jnp.int32)

    c_ref[...] = jnp.zeros((N_KV, m_rows, LANES), f32)
    acc_ref[...] = jnp.zeros((N_KV, m_rows, HEAD_DIM), f32)
    go = step(i, True)
    lax.while_loop(lambda st: jnp.logical_and(st[0] >= 0, st[1] > 0),
                   lambda st: (st[0] - 1, step(st[0], False)), (i - 1, go))
    for n in range(N_KV):
        _store_gated(o_ref, gate_ref, acc_ref[n], n, groups)


def _mixer_b(q, gate, kv):
    s, width = q.shape
    groups = width // HEAD_DIM // N_KV
    kvw = kv.shape[1] // 2
    half_spec = lambda c: pl.BlockSpec((s, kvw), lambda i: (0, c), pipeline_mode=pl.Buffered(1))
    return pl.pallas_call(
        _mixer_b_kernel,
        grid=(s // TQ,),
        in_specs=[pl.BlockSpec((TQ, width), lambda i: (i, 0)),
                  pl.BlockSpec((TQ, width), lambda i: (i, 0)),
                  half_spec(0), half_spec(1)],
        out_specs=pl.BlockSpec((TQ, width), lambda i: (i, 0)),
        out_shape=jax.ShapeDtypeStruct((s, width), jnp.bfloat16),
        scratch_shapes=[pltpu.VMEM((N_KV, groups * TQ, HEAD_DIM), jnp.bfloat16),
                        pltpu.VMEM((N_KV, groups * TQ, LANES), jnp.float32),
                        pltpu.VMEM((N_KV, groups * TQ, HEAD_DIM), jnp.float32)],
        compiler_params=_cparams(("parallel",)),
        name="mixer_b",
    )(q, gate, kv, kv)


def _post_kernel(og_ref, x_ref, p_ref, wo_ref, wg_ref, wp_ref, lng_ref, lnb_ref, *refs, alpha, n_side):
    f32 = jnp.float32
    o_ref = refs[n_side]
    _cast_sides(refs[:n_side], refs[n_side + 1:])
    h = jnp.dot(og_ref[...], wo_ref[...], preferred_element_type=f32)
    y = alpha * x_ref[...] + h
    mu = jnp.mean(y, axis=-1, keepdims=True)
    d = y - mu
    var = jnp.mean(d * d, axis=-1, keepdims=True)
    yn = d * lax.rsqrt(var + LN_EPS) * lng_ref[...] + lnb_ref[...]
    gl = jnp.dot(yn.astype(jnp.bfloat16), wg_ref[...], preferred_element_type=f32)
    pe = jnp.dot(p_ref[...].astype(jnp.bfloat16), wp_ref[...], preferred_element_type=f32)
    o_ref[...] = yn + pe * jax.nn.sigmoid(gl)


def _resident_slab(arr, idx):
    return pl.BlockSpec((None,) + arr.shape[1:], lambda *_: (idx,) + (0,) * (arr.ndim - 1),
                        pipeline_mode=pl.Buffered(1))


def _post(og, x, p, layer, batch, wo, wg, wp, lng, lnb, alpha, sides=()):
    s, d = x.shape
    tm = TM_PROJ
    side_in, side_out, side_shapes = _side_io(sides, s // tm)
    res = pl.pallas_call(
        functools.partial(_post_kernel, alpha=alpha, n_side=len(sides)),
        grid=(s // tm,),
        in_specs=[pl.BlockSpec((tm, og.shape[1]), lambda i: (i, 0)),
                  pl.BlockSpec((tm, d), lambda i: (i, 0)),
                  pl.BlockSpec((None, None, tm, p.shape[3]), lambda i: (layer, batch, i, 0)),
                  _resident(wo.shape), _resident(wg.shape), _resident_slab(wp, layer),
                  _resident_slab(lng, layer), _resident_slab(lnb, layer)] + side_in,
        out_specs=[pl.BlockSpec((tm, d), lambda i: (i, 0))] + side_out,
        out_shape=[jax.ShapeDtypeStruct((s, d), jnp.float32)] + side_shapes,
        compiler_params=_cparams(("parallel",)),
        name="post",
    )(og, x, p, wo, wg, wp, lng, lnb, *(sd.arr for sd in sides))
    return res[0], res[1:]


def _now_bf16(sd):
    return (sd.arr if sd.layer < 0 else sd.arr[sd.layer]).astype(jnp.bfloat16)


def kernel(x, p, positions, w_in_a, w_o_a, w_kv_b, w_in_b, w_o_b, ln_g, ln_b, w_ple, w_ple_gate):
    bsz, s, d = x.shape
    n_a, n_b = w_in_a.shape[0], w_in_b.shape[0]
    depth = n_a + n_b
    alpha = (2.0 * depth) ** 0.25
    width = w_o_a.shape[1]
    kvw = N_KV * HEAD_DIM
    idxw = IDX_HEADS * IDX_DIM
    n_sel = min(TOPK_MAX, s // 4)
    bf16, f32 = jnp.bfloat16, jnp.float32
    sm_scale = HEAD_DIM ** -0.5
    assert s % TK2 == 0
    n_steps = s // TM_PROJ
    wp_all = w_ple.astype(bf16)
    lng, lnb = ln_g[:, None, :], ln_b[:, None, :]

    def proj_sources(layer):
        if layer < n_a:
            return [Side(jnp.swapaxes(w_in_a[layer], 0, 1), -1)]
        first_b = [Side(w_kv_b, -1)] if layer == n_a else []
        return first_b + [Side(w_in_b, layer - n_a)]

    def riders(srcs):
        return srcs if all(_side_ok(sd.arr, n_steps) for sd in srcs) else []

    proj_w, post_w = {}, {}
    outs = []
    for b in range(bsz):
        xb = x[b]
        ropes = (positions[b].astype(f32)[:, None],
                 jnp.stack([_rope_freq_row(HEAD_DIM, ROT_DIM), _rope_freq_row(IDX_DIM, IDX_ROT)]),
                 ((HEAD_DIM, ROT_DIM // 2), (IDX_DIM, IDX_ROT // 2)))
        kv_shared = None
        for i in range(depth):
            if i not in proj_w:
                proj_w[i] = [_now_bf16(sd) for sd in proj_sources(i)]
            post_src = [Side(w_o_a, i) if i < n_a else Side(w_o_b, i - n_a), Side(w_ple_gate, i)]
            post_ride = riders(post_src) if i not in post_w else []
            if i < n_a:
                plan = [(0, width, width, 0, 0, LOG2_E * sm_scale), (0, kvw, kvw, 1, 0, 1.0),
                        (0, kvw, kvw, 2, -1, 1.0), (0, width, width, 3, -1, 1.0), (0, idxw, idxw, 4, 1, 1.0),
                        (0, IDX_DIM, LANES, 4, 1, 1.0), (0, IDX_HEADS, IDX_HEADS, 5, -1, 1.0)]
                (q, k, vt, gate, ix, iw), cast = _fused_project(
                    xb, proj_w[i], [True], plan, [bf16, bf16, bf16, f32, bf16, f32], ropes, post_ride, t_outs=(2,))
                og = _mixer_at(q, ix, iw, gate, k, vt, n_sel)
            elif kv_shared is None:
                plan = [(0, 2 * kvw, 2 * kvw, 0, -1, 1.0), (1, width, width, 1, -1, sm_scale),
                        (1, width, width, 2, -1, 1.0)]
                (kv_shared, q, gate), cast = _fused_project(xb, proj_w[i], [False, False], plan, [bf16, bf16, f32],
                                                            None, post_ride)
                og = _mixer_b(q, gate, kv_shared)
            else:
                plan = [(0, width, width, 0, -1, sm_scale), (0, width, width, 1, -1, 1.0)]
                (q, gate), cast = _fused_project(xb, proj_w[i], [False], plan, [bf16, f32], None, post_ride)
                og = _mixer_b(q, gate, kv_shared)
            if i not in post_w:
                post_w[i] = tuple(cast) if post_ride else tuple(_now_bf16(sd) for sd in post_src)
            next_ride = riders(proj_sources(i + 1)) if i + 1 < depth and i + 1 not in proj_w else []
            xb, cast = _post(og, xb, p, i, b, post_w[i][0], post_w[i][1], wp_all, lng, lnb, alpha, next_ride)
            if next_ride:
                proj_w[i + 1] = list(cast)
        outs.append(xb)
    return jnp.stack(outs, axis=0)
```

```python
import functools
from typing import NamedTuple

import jax
import jax.numpy as jnp
from jax import lax
from jax.experimental import pallas as pl
from jax.experimental.pallas import tpu as pltpu

HEAD_DIM = 128
N_KV = 4
ROT_DIM = HEAD_DIM // 4
IDX_HEADS = 16
IDX_DIM = 64
IDX_ROT = IDX_DIM // 4
TOPK_MAX = 256
ROPE_THETA = 500000.0
LN_EPS = 1e-5
LOG2_E = 1.4426950408889634

LANES = 128
BF16_SUBLANES = 16
V7X_VMEM_BYTES = 64 * 1024 * 1024
VMEM_LIMIT = V7X_VMEM_BYTES - 8 * 1024 * 1024

TQ = 256
TK2 = 2 * TQ
TM_PROJ = 256
SEG_MAX = 1024
NEG_BIG = -1e30
EXP_ZERO_ABOVE = 110.0
MAX_BISECT = 256
BISECT_STEPS_PER_CHECK = 3


def _cparams(sem):
    return pltpu.CompilerParams(dimension_semantics=sem, vmem_limit_bytes=VMEM_LIMIT)


def _resident(shape):
    nd = len(shape)
    return pl.BlockSpec(shape, lambda *_: (0,) * nd, pipeline_mode=pl.Buffered(1))


class Seg(NamedTuple):
    w: int
    col: int
    width: int
    stored: int
    out: int
    out_col: int
    rope: int
    scale: float


def _rope_slab(y, half, c, s_up, s_dn):
    up = pltpu.roll(y, LANES - half, axis=1)
    dn = pltpu.roll(y, half, axis=1)
    return y * c + up * s_up + dn * s_dn


class Side(NamedTuple):
    arr: jax.Array
    layer: int


def _side_ok(arr, n_steps):
    rows = arr.shape[-2]
    return rows % (n_steps * BF16_SUBLANES) == 0


def _side_io(sides, n_steps):
    in_specs, out_specs, out_shapes = [], [], []
    for sd in sides:
        r, c = sd.arr.shape[-2:]
        rows = r // n_steps
        if sd.layer < 0:
            in_specs.append(pl.BlockSpec((rows, c), lambda i: (i, 0)))
        else:
            in_specs.append(pl.BlockSpec((None, rows, c), lambda i, l=sd.layer: (l, i, 0)))
        out_specs.append(pl.BlockSpec((rows, c), lambda i: (i, 0)))
        out_shapes.append(jax.ShapeDtypeStruct((r, c), jnp.bfloat16))
    return in_specs, out_specs, out_shapes


def _cast_sides(side_in, side_out):
    for src, dst in zip(side_in, side_out):
        dst[...] = src[...].astype(dst.dtype)


def _rope_slab_tables(pos, freq, head_dim, half):
    ang = pos * freq
    c, sn = jnp.cos(ang), jnp.sin(ang)
    lane = lax.broadcasted_iota(jnp.int32, ang.shape, 1) & (head_dim - 1)
    s_up = jnp.where(lane < half, -sn, 0.0)
    s_dn = jnp.where(lane < half, 0.0, jnp.where(lane < 2 * half, sn, 0.0))
    return c, s_up, s_dn


def _fused_proj_kernel(x_ref, *refs, segs, rope_dims, w_transposed, n_side):
    n_w, n_rope = len(w_transposed), 2 if rope_dims else 0
    ws = refs[:n_w]
    side_in = refs[n_w + n_rope:n_w + n_rope + n_side]
    outs = refs[n_w + n_rope + n_side:len(refs) - n_side]
    _cast_sides(side_in, refs[len(refs) - n_side:])
    if rope_dims:
        pos_ref, freq_ref = refs[n_w:n_w + 2]
        tabs = [_rope_slab_tables(pos_ref[...], freq_ref[r:r + 1, :], hd, half)
                for r, (hd, half) in enumerate(rope_dims)]
    x = x_ref[...].astype(jnp.bfloat16)
    for sg in segs:
        if w_transposed[sg.w]:
            y = lax.dot_general(x, ws[sg.w][sg.col:sg.col + sg.width, :], (((1,), (1,)), ((), ())),
                                preferred_element_type=jnp.float32)
        else:
            y = jnp.dot(x, ws[sg.w][:, sg.col:sg.col + sg.width], preferred_element_type=jnp.float32)
        if sg.scale != 1.0:
            y = y * sg.scale
        if sg.stored > sg.width:
            y = jnp.concatenate([y, jnp.zeros((y.shape[0], sg.stored - sg.width), y.dtype)], axis=1)
        if sg.rope >= 0:
            c, s_up, s_dn = tabs[sg.rope]
            slabs = [_rope_slab(y[:, j * LANES:(j + 1) * LANES], rope_dims[sg.rope][1], c, s_up, s_dn)
                     for j in range(sg.stored // LANES)]
            y = jnp.concatenate(slabs, axis=1) if len(slabs) > 1 else slabs[0]
        o_ref = outs[sg.out]
        o_ref[:, sg.out_col:sg.out_col + sg.stored] = y.astype(o_ref.dtype)


def _fused_project(x, ws, w_transposed, plan, out_dtypes, ropes, sides=()):
    s, kdim = x.shape
    segs, cols, out_cols = [], [0] * len(ws), [0] * len(out_dtypes)
    for wi, width, stored, out, rope, scale in plan:
        assert stored == width or width <= SEG_MAX
        for c0 in range(0, width, SEG_MAX):
            wd = min(SEG_MAX, width - c0)
            segs.append(Seg(wi, cols[wi] + c0, wd, wd if stored == width else stored, out, out_cols[out] + c0,
                            rope, scale))
        cols[wi] += width
        out_cols[out] += stored
    assert all(c == w.shape[0 if t else 1] for c, w, t in zip(cols, ws, w_transposed))
    tm = TM_PROJ
    rope_in, rope_specs = [], []
    if ropes is not None:
        pos, freqs, _ = ropes
        rope_in = [pos, freqs]
        rope_specs = [pl.BlockSpec((tm, 1), lambda i: (i, 0)), _resident(freqs.shape)]
    side_in, side_out, side_shapes = _side_io(sides, s // tm)
    res = pl.pallas_call(
        functools.partial(_fused_proj_kernel, segs=tuple(segs), rope_dims=ropes[2] if ropes is not None else (),
                          w_transposed=tuple(w_transposed), n_side=len(sides)),
        grid=(s // tm,),
        in_specs=[pl.BlockSpec((tm, kdim), lambda i: (i, 0))] + [_resident(w.shape) for w in ws]
                 + rope_specs + side_in,
        out_specs=[pl.BlockSpec((tm, n), lambda i: (i, 0)) for n in out_cols] + side_out,
        out_shape=[jax.ShapeDtypeStruct((s, n), dt) for n, dt in zip(out_cols, out_dtypes)] + side_shapes,
        compiler_params=_cparams(("parallel",)),
        name="fused_proj",
    )(x, *ws, *rope_in, *(sd.arr for sd in sides))
    return res[:len(out_dtypes)], res[len(out_dtypes):]


def _rope_freq_row(head_dim, rot_dim):
    half = rot_dim // 2
    inv = 1.0 / (ROPE_THETA ** (jnp.arange(half, dtype=jnp.float32) / half))
    row = jnp.concatenate([inv, inv, jnp.zeros((head_dim - rot_dim,), jnp.float32)])
    return jnp.tile(row, LANES // head_dim)


def _stack_heads(q_ref, n, groups):
    return jnp.concatenate(
        [q_ref[:, (n * groups + g) * HEAD_DIM:(n * groups + g + 1) * HEAD_DIM] for g in range(groups)], axis=0)


def _store_gated(o_ref, gate_ref, out, n, groups):
    tq = o_ref.shape[0]
    for g in range(groups):
        cols = slice((n * groups + g) * HEAD_DIM, (n * groups + g + 1) * HEAD_DIM)
        gate = gate_ref[:, cols]
        o_ref[:, cols] = (out[g * tq:(g + 1) * tq, :] * (gate * jax.nn.sigmoid(gate))).astype(o_ref.dtype)


COUNT_ROWS = 32


def _mixer_a_kernel(q_ref, iq_ref, iw_ref, gate_ref, ik_ref, k_ref, v_ref, o_ref,
                     score_ref, wt_ref, lo_ref, hi_ref, mn_ref, mx_ref,
                     qn_ref, z_ref, bias_ref, m_ref, acc_ref, *, n_sel, idx_scale):
    tq = q_ref.shape[0]
    groups = q_ref.shape[1] // HEAD_DIM // N_KV
    m_cols = groups * tq
    i = pl.program_id(0)
    nkb2 = (i + 2) // 2
    f32, bf16 = jnp.float32, jnp.bfloat16
    nt = (((1,), (1,)), ((), ()))

    key_in_blk = lax.broadcasted_iota(jnp.int32, (TK2, tq), 0)
    q_pos = lax.broadcasted_iota(jnp.int32, (TK2, tq), 1) + i * tq

    iw = iw_ref[...] * idx_scale
    wt_ref[...] = jnp.concatenate([iw, jnp.zeros((tq, LANES - IDX_HEADS), f32)], axis=1).T

    mn_ref[...] = jnp.full((tq, tq), jnp.inf, f32)
    mx_ref[...] = jnp.full((tq, tq), -jnp.inf, f32)

    def score_body(kb2, carry):
        koff = pl.multiple_of(kb2 * TK2, TK2)
        ik = ik_ref[pl.ds(koff, TK2), :IDX_DIM]
        acc = jnp.zeros((TK2, tq), f32)
        for h in range(IDX_HEADS):
            d = lax.dot_general(ik, iq_ref[:, h * IDX_DIM:(h + 1) * IDX_DIM], nt, preferred_element_type=f32)
            acc = acc + wt_ref[h:h + 1, :] * jnp.maximum(d, 0.0)
        causal = (key_in_blk + kb2 * TK2) <= q_pos
        sc = jnp.where(causal, acc, -jnp.inf)
        score_ref[pl.ds(koff, TK2), :] = sc
        lo_part = jnp.where(causal, acc, jnp.inf)
        mn_ref[...] = jnp.minimum(mn_ref[...], jnp.minimum(lo_part[:tq], lo_part[tq:]))
        mx_ref[...] = jnp.maximum(mx_ref[...], jnp.maximum(sc[:tq], sc[tq:]))
        return carry

    lax.fori_loop(0, nkb2, score_body, 0)

    t1 = (lax.broadcasted_iota(jnp.int32, (1, tq), 1) + i * tq + 1).astype(f32)
    k_t = jnp.minimum(t1, float(n_sel))
    row_min = jnp.min(mn_ref[...], axis=0, keepdims=True)
    row_max = jnp.max(mx_ref[...], axis=0, keepdims=True)
    if tq >= n_sel:
        lo0 = jnp.where(t1 >= float(tq), jnp.min(mx_ref[...], axis=0, keepdims=True), row_min)
    else:
        lo0 = row_min
    lo_ref[...] = jnp.broadcast_to(lo0, lo_ref.shape)
    hi_ref[...] = jnp.broadcast_to(jnp.where(t1 <= float(n_sel), row_min, row_max), hi_ref.shape)

    def unresolved():
        lo, hi = lo_ref[0:1, :], hi_ref[0:1, :]
        mid = 0.5 * lo + 0.5 * hi
        open_ = jnp.where(mid > lo, 1.0, 0.0) * jnp.where(mid < hi, 1.0, 0.0)
        return (jnp.max(open_) > 0.0).astype(jnp.int32)

    def bisect_step():
        lo, hi = lo_ref[0:1, :], hi_ref[0:1, :]
        mid = 0.5 * lo + 0.5 * hi

        def count_body(kb2, acc):
            koff = pl.multiple_of(kb2 * TK2, TK2)
            ind = jnp.where(score_ref[pl.ds(koff, TK2), :] >= mid, 1.0, 0.0)
            return acc + jnp.sum(ind.reshape(TK2 // COUNT_ROWS, COUNT_ROWS, tq), axis=0)

        acc = lax.fori_loop(0, nkb2, count_body, jnp.zeros((COUNT_ROWS, tq), f32))
        cnt = jnp.sum(acc, axis=0, keepdims=True)
        ge = cnt >= k_t
        lo_ref[...] = jnp.broadcast_to(jnp.where(ge, mid, lo), lo_ref.shape)
        hi_ref[...] = jnp.broadcast_to(jnp.where(cnt == k_t, mid, jnp.where(ge, hi, mid)), hi_ref.shape)

    def bisect_body(state):
        for _ in range(BISECT_STEPS_PER_CHECK):
            bisect_step()
        return state[0] + BISECT_STEPS_PER_CHECK, unresolved()

    lax.while_loop(lambda st: jnp.logical_and(st[1] > 0, st[0] < MAX_BISECT), bisect_body,
                   (jnp.int32(0), unresolved()))

    thr = lo_ref[0:1, :]
    m_rows = groups * tq
    for n in range(N_KV):
        qn_ref[n] = _stack_heads(q_ref, n, groups)
    m_ref[...] = jnp.full(m_ref.shape, NEG_BIG, f32)
    acc_ref[...] = jnp.zeros(acc_ref.shape, f32)

    def logits(kb2, n):
        kblk = k_ref[pl.ds(pl.multiple_of(kb2 * TK2, TK2), TK2), n * HEAD_DIM:(n + 1) * HEAD_DIM]
        return lax.dot_general(qn_ref[n], kblk, nt, preferred_element_type=f32).astype(bf16)

    def mask_bias(kb2):
        blk = score_ref[pl.ds(pl.multiple_of(kb2 * TK2, TK2), TK2), :]
        return jnp.where(blk >= thr, 0.0, NEG_BIG).astype(bf16).T

    for n in range(N_KV):
        z_ref[n] = logits(0, n)
    bias_ref[...] = mask_bias(0)

    def attn_body(kb2, carry):
        koff = pl.multiple_of(kb2 * TK2, TK2)
        ahead = jnp.minimum(kb2 + 1, nkb2 - 1)
        bias = jnp.concatenate([bias_ref[...]] * groups, axis=0)
        bias_ref[...] = mask_bias(ahead)
        for n in range(N_KV):
            vblk = v_ref[pl.ds(koff, TK2), n * HEAD_DIM:(n + 1) * HEAD_DIM]
            vext = jnp.concatenate([vblk, jnp.ones_like(vblk)], axis=1)
            z = z_ref[n] + bias
            z_ref[n] = logits(ahead, n)
            m_old = m_ref[n]
            m_new = jnp.maximum(m_old, jnp.max(z, axis=1, keepdims=True).astype(f32))
            p = jnp.exp2(z - jnp.concatenate([m_new.astype(bf16)] * (TK2 // LANES), axis=1))
            alpha = jnp.exp2(m_old - m_new)
            acc_ref[n] = (jnp.concatenate([alpha, alpha], axis=1) * acc_ref[n]
                          + jnp.dot(p, vext, preferred_element_type=f32))
            m_ref[n] = m_new
        return carry

    lax.fori_loop(0, nkb2, attn_body, 0)
    for n in range(N_KV):
        acc = acc_ref[n]
        _store_gated(o_ref, gate_ref, acc[:, :HEAD_DIM] / acc[:, HEAD_DIM:], n, groups)


def _mixer_a(q, ix, iw, gate, k, v, n_sel):
    s, width = q.shape
    groups = width // HEAD_DIM // N_KV
    f32 = jnp.float32
    idxw = IDX_HEADS * IDX_DIM
    ik_spec = pl.BlockSpec((s, LANES), lambda i: (0, idxw // LANES), pipeline_mode=pl.Buffered(1))
    kern = functools.partial(_mixer_a_kernel, n_sel=n_sel, idx_scale=(IDX_DIM ** -0.5) * (IDX_HEADS ** -0.5))
    return pl.pallas_call(
        kern,
        grid=(s // TQ,),
        in_specs=[pl.BlockSpec((TQ, width), lambda i: (i, 0)),
                  pl.BlockSpec((TQ, IDX_HEADS * IDX_DIM), lambda i: (i, 0)),
                  pl.BlockSpec((TQ, IDX_HEADS), lambda i: (i, 0)),
                  pl.BlockSpec((TQ, width), lambda i: (i, 0)),
                  ik_spec, _resident(k.shape), _resident(v.shape)],
        out_specs=pl.BlockSpec((TQ, width), lambda i: (i, 0)),
        out_shape=jax.ShapeDtypeStruct((s, width), jnp.bfloat16),
        scratch_shapes=[pltpu.VMEM((s, TQ), f32),
                        pltpu.VMEM((LANES, TQ), f32),
                        pltpu.VMEM((8, TQ), f32),
                        pltpu.VMEM((8, TQ), f32),
                        pltpu.VMEM((TQ, TQ), f32),
                        pltpu.VMEM((TQ, TQ), f32),
                        pltpu.VMEM((N_KV, groups * TQ, HEAD_DIM), jnp.bfloat16),
                        pltpu.VMEM((N_KV, groups * TQ, TK2), jnp.bfloat16),
                        pltpu.VMEM((TQ, TK2), jnp.bfloat16),
                        pltpu.VMEM((N_KV, groups * TQ, LANES), f32),
                        pltpu.VMEM((N_KV, groups * TQ, 2 * HEAD_DIM), f32)],
        compiler_params=_cparams(("parallel",)),
        name="mixer_a",
    )(q, ix, iw, gate, ix, k, v)


def _mixer_b_kernel(q_ref, gate_ref, k_ref, v_ref, o_ref, qn_ref, c_ref, acc_ref):
    tq = q_ref.shape[0]
    tk = tq
    groups = q_ref.shape[1] // HEAD_DIM // N_KV
    m_rows = groups * tq
    i = pl.program_id(0)
    f32, bf16 = jnp.float32, jnp.bfloat16

    row = lax.broadcasted_iota(jnp.int32, (tq, tk), 0)
    col = lax.broadcasted_iota(jnp.int32, (tq, tk), 1)
    later = jnp.where(row > col, 1.0, 0.0).astype(bf16)
    later2 = jnp.concatenate([later, later], axis=0)
    strict = jnp.concatenate([col < row] * groups, axis=0)

    for n in range(N_KV):
        qn_ref[n] = _stack_heads(q_ref, n, groups)

    def step(kb, diagonal):
        koff = pl.multiple_of(kb * tk, tk)
        c_min = None
        for n in range(N_KV):
            kblk = k_ref[pl.ds(koff, tk), n * HEAD_DIM:(n + 1) * HEAD_DIM]
            vblk = v_ref[pl.ds(koff, tk), n * HEAD_DIM:(n + 1) * HEAD_DIM]
            z = lax.dot_general(qn_ref[n], kblk, (((1,), (1,)), ((), ())), preferred_element_type=f32)
            t = jnp.log(1.0 + jnp.exp2(jnp.abs(z) * (-LOG2_E)))
            sp = jnp.maximum(z, 0.0) + t
            if diagonal:
                sp = jnp.where(strict, sp, 0.0)
            sp_hi = sp.astype(bf16)
            sp_lo = (sp - sp_hi.astype(f32)).astype(bf16)
            suffix = jnp.dot(jnp.concatenate([sp_hi, sp_lo], axis=1), later2, preferred_element_type=f32)
            c_old = c_ref[n]
            total = suffix + jnp.concatenate([c_old] * (tk // LANES), axis=1)
            a = jnp.exp((jnp.minimum(z, 0.0) - t) - total)
            if diagonal:
                a = jnp.where(strict, a, 0.0)
            acc_ref[n] += jnp.dot(a.astype(bf16), vblk, preferred_element_type=f32)
            c_new = c_old + jnp.broadcast_to(jnp.sum(sp, axis=1, keepdims=True), (m_rows, LANES))
            c_ref[n] = c_new
            c_min = c_new if c_min is None else jnp.minimum(c_min, c_new)
        return (jnp.min(c_min) <= EXP_ZERO_ABOVE).astype(jnp.int32)

    c_ref[...] = jnp.zeros((N_KV, m_rows, LANES), f32)
    acc_ref[...] = jnp.zeros((N_KV, m_rows, HEAD_DIM), f32)
    go = step(i, True)
    lax.while_loop(lambda st: jnp.logical_and(st[0] >= 0, st[1] > 0),
                   lambda st: (st[0] - 1, step(st[0], False)), (i - 1, go))
    for n in range(N_KV):
        _store_gated(o_ref, gate_ref, acc_ref[n], n, groups)


def _mixer_b(q, gate, kv):
    s, width = q.shape
    groups = width // HEAD_DIM // N_KV
    kvw = kv.shape[1] // 2
    half_spec = lambda c: pl.BlockSpec((s, kvw), lambda i: (0, c), pipeline_mode=pl.Buffered(1))
    return pl.pallas_call(
        _mixer_b_kernel,
        grid=(s // TQ,),
        in_specs=[pl.BlockSpec((TQ, width), lambda i: (i, 0)),
                  pl.BlockSpec((TQ, width), lambda i: (i, 0)),
                  half_spec(0), half_spec(1)],
        out_specs=pl.BlockSpec((TQ, width), lambda i: (i, 0)),
        out_shape=jax.ShapeDtypeStruct((s, width), jnp.bfloat16),
        scratch_shapes=[pltpu.VMEM((N_KV, groups * TQ, HEAD_DIM), jnp.bfloat16),
                        pltpu.VMEM((N_KV, groups * TQ, LANES), jnp.float32),
                        pltpu.VMEM((N_KV, groups * TQ, HEAD_DIM), jnp.float32)],
        compiler_params=_cparams(("parallel",)),
        name="mixer_b",
    )(q, gate, kv, kv)


def _post_kernel(og_ref, x_ref, p_ref, wo_ref, wg_ref, wp_ref, lng_ref, lnb_ref, *refs, alpha, n_side):
    f32 = jnp.float32
    o_ref = refs[n_side]
    _cast_sides(refs[:n_side], refs[n_side + 1:])
    h = jnp.dot(og_ref[...], wo_ref[...], preferred_element_type=f32)
    y = alpha * x_ref[...] + h
    mu = jnp.mean(y, axis=-1, keepdims=True)
    d = y - mu
    var = jnp.mean(d * d, axis=-1, keepdims=True)
    yn = d * lax.rsqrt(var + LN_EPS) * lng_ref[...] + lnb_ref[...]
    gl = jnp.dot(yn.astype(jnp.bfloat16), wg_ref[...], preferred_element_type=f32)
    pe = jnp.dot(p_ref[...].astype(jnp.bfloat16), wp_ref[...], preferred_element_type=f32)
    o_ref[...] = yn + pe * jax.nn.sigmoid(gl)


def _resident_slab(arr, idx):
    return pl.BlockSpec((None,) + arr.shape[1:], lambda *_: (idx,) + (0,) * (arr.ndim - 1),
                        pipeline_mode=pl.Buffered(1))


def _post(og, x, p, layer, batch, wo, wg, wp, lng, lnb, alpha, sides=()):
    s, d = x.shape
    tm = TM_PROJ
    side_in, side_out, side_shapes = _side_io(sides, s // tm)
    res = pl.pallas_call(
        functools.partial(_post_kernel, alpha=alpha, n_side=len(sides)),
        grid=(s // tm,),
        in_specs=[pl.BlockSpec((tm, og.shape[1]), lambda i: (i, 0)),
                  pl.BlockSpec((tm, d), lambda i: (i, 0)),
                  pl.BlockSpec((None, None, tm, p.shape[3]), lambda i: (layer, batch, i, 0)),
                  _resident(wo.shape), _resident(wg.shape), _resident_slab(wp, layer),
                  _resident_slab(lng, layer), _resident_slab(lnb, layer)] + side_in,
        out_specs=[pl.BlockSpec((tm, d), lambda i: (i, 0))] + side_out,
        out_shape=[jax.ShapeDtypeStruct((s, d), jnp.float32)] + side_shapes,
        compiler_params=_cparams(("parallel",)),
        name="post",
    )(og, x, p, wo, wg, wp, lng, lnb, *(sd.arr for sd in sides))
    return res[0], res[1:]


def _now_bf16(sd):
    return (sd.arr if sd.layer < 0 else sd.arr[sd.layer]).astype(jnp.bfloat16)


def kernel(x, p, positions, w_in_a, w_o_a, w_kv_b, w_in_b, w_o_b, ln_g, ln_b, w_ple, w_ple_gate):
    bsz, s, d = x.shape
    n_a, n_b = w_in_a.shape[0], w_in_b.shape[0]
    depth = n_a + n_b
    alpha = (2.0 * depth) ** 0.25
    width = w_o_a.shape[1]
    kvw = N_KV * HEAD_DIM
    idxw = IDX_HEADS * IDX_DIM
    n_sel = min(TOPK_MAX, s // 4)
    bf16, f32 = jnp.bfloat16, jnp.float32
    sm_scale = HEAD_DIM ** -0.5
    assert s % TK2 == 0
    n_steps = s // TM_PROJ
    wp_all = w_ple.astype(bf16)
    lng, lnb = ln_g[:, None, :], ln_b[:, None, :]

    def proj_sources(layer):
        if layer < n_a:
            return [Side(jnp.swapaxes(w_in_a[layer], 0, 1), -1)]
        first_b = [Side(w_kv_b, -1)] if layer == n_a else []
        return first_b + [Side(w_in_b, layer - n_a)]

    def riders(srcs):
        return srcs if all(_side_ok(sd.arr, n_steps) for sd in srcs) else []

    proj_w, post_w = {}, {}
    outs = []
    for b in range(bsz):
        xb = x[b]
        ropes = (positions[b].astype(f32)[:, None],
                 jnp.stack([_rope_freq_row(HEAD_DIM, ROT_DIM), _rope_freq_row(IDX_DIM, IDX_ROT)]),
                 ((HEAD_DIM, ROT_DIM // 2), (IDX_DIM, IDX_ROT // 2)))
        kv_shared = None
        for i in range(depth):
            if i not in proj_w:
                proj_w[i] = [_now_bf16(sd) for sd in proj_sources(i)]
            post_src = [Side(w_o_a, i) if i < n_a else Side(w_o_b, i - n_a), Side(w_ple_gate, i)]
            post_ride = riders(post_src) if i not in post_w else []
            if i < n_a:
                plan = [(0, width, width, 0, 0, LOG2_E * sm_scale), (0, kvw, kvw, 1, 0, 1.0),
                        (0, kvw, kvw, 2, -1, 1.0), (0, width, width, 3, -1, 1.0), (0, idxw, idxw, 4, 1, 1.0),
                        (0, IDX_DIM, LANES, 4, 1, 1.0), (0, IDX_HEADS, IDX_HEADS, 5, -1, 1.0)]
                (q, k, v, gate, ix, iw), cast = _fused_project(
                    xb, proj_w[i], [True], plan, [bf16, bf16, bf16, f32, bf16, f32], ropes, post_ride)
                og = _mixer_a(q, ix, iw, gate, k, v, n_sel)
            elif kv_shared is None:
                plan = [(0, 2 * kvw, 2 * kvw, 0, -1, 1.0), (1, width, width, 1, -1, sm_scale),
                        (1, width, width, 2, -1, 1.0)]
                (kv_shared, q, gate), cast = _fused_project(xb, proj_w[i], [False, False], plan, [bf16, bf16, f32],
                                                            None, post_ride)
                og = _mixer_b(q, gate, kv_shared)
            else:
                plan = [(0, width, width, 0, -1, sm_scale), (0, width, width, 1, -1, 1.0)]
                (q, gate), cast = _fused_project(xb, proj_w[i], [False], plan, [bf16, f32], None, post_ride)
                og = _mixer_b(q, gate, kv_shared)
            if i not in post_w:
                post_w[i] = tuple(cast) if post_ride else tuple(_now_bf16(sd) for sd in post_src)
            next_ride = riders(proj_sources(i + 1)) if i + 1 < depth and i + 1 not in proj_w else []
            xb, cast = _post(og, xb, p, i, b, post_w[i][0], post_w[i][1], wp_all, lng, lnb, alpha, next_ride)
            if next_ride:
                proj_w[i + 1] = list(cast)
        outs.append(xb)
    return jnp.stack(outs, axis=0)
```

```python
import functools
from typing import NamedTuple

import jax
import jax.numpy as jnp
from jax import lax
from jax.experimental import pallas as pl
from jax.experimental.pallas import tpu as pltpu

HEAD_DIM = 128
N_KV = 4
ROT_DIM = HEAD_DIM // 4
IDX_HEADS = 16
IDX_DIM = 64
IDX_ROT = IDX_DIM // 4
TOPK_MAX = 256
ROPE_THETA = 500000.0
LN_EPS = 1e-5
LOG2_E = 1.4426950408889634

LANES = 128
BF16_SUBLANES = 16
V7X_VMEM_BYTES = 64 * 1024 * 1024
VMEM_LIMIT = V7X_VMEM_BYTES - 8 * 1024 * 1024

TQ = 256
TK2 = 2 * TQ
TM_PROJ = 256
SEG_MAX = 1024
NEG_BIG = -1e30
EXP_ZERO_ABOVE = 110.0
MAX_BISECT = 256
BISECT_STEPS_PER_CHECK = 3


def _cparams(sem):
    return pltpu.CompilerParams(dimension_semantics=sem, vmem_limit_bytes=VMEM_LIMIT)


def _resident(shape):
    nd = len(shape)
    return pl.BlockSpec(shape, lambda *_: (0,) * nd, pipeline_mode=pl.Buffered(1))


class Seg(NamedTuple):
    w: int
    col: int
    width: int
    stored: int
    out: int
    out_col: int
    rope: int
    scale: float


def _rope_slab(y, half, c, s_up, s_dn):
    up = pltpu.roll(y, LANES - half, axis=1)
    dn = pltpu.roll(y, half, axis=1)
    return y * c + up * s_up + dn * s_dn


class Side(NamedTuple):
    arr: jax.Array
    layer: int


def _side_ok(arr, n_steps):
    rows = arr.shape[-2]
    return rows % (n_steps * BF16_SUBLANES) == 0


def _side_io(sides, n_steps):
    in_specs, out_specs, out_shapes = [], [], []
    for sd in sides:
        r, c = sd.arr.shape[-2:]
        rows = r // n_steps
        if sd.layer < 0:
            in_specs.append(pl.BlockSpec((rows, c), lambda i: (i, 0)))
        else:
            in_specs.append(pl.BlockSpec((None, rows, c), lambda i, l=sd.layer: (l, i, 0)))
        out_specs.append(pl.BlockSpec((rows, c), lambda i: (i, 0)))
        out_shapes.append(jax.ShapeDtypeStruct((r, c), jnp.bfloat16))
    return in_specs, out_specs, out_shapes


def _cast_sides(side_in, side_out):
    for src, dst in zip(side_in, side_out):
        dst[...] = src[...].astype(dst.dtype)


def _rope_slab_tables(pos, freq, head_dim, half):
    ang = pos * freq
    c, sn = jnp.cos(ang), jnp.sin(ang)
    lane = lax.broadcasted_iota(jnp.int32, ang.shape, 1) & (head_dim - 1)
    s_up = jnp.where(lane < half, -sn, 0.0)
    s_dn = jnp.where(lane < half, 0.0, jnp.where(lane < 2 * half, sn, 0.0))
    return c, s_up, s_dn


def _fused_proj_kernel(x_ref, *refs, segs, rope_dims, w_transposed, n_side):
    n_w, n_rope = len(w_transposed), 2 if rope_dims else 0
    ws = refs[:n_w]
    side_in = refs[n_w + n_rope:n_w + n_rope + n_side]
    outs = refs[n_w + n_rope + n_side:len(refs) - n_side]
    _cast_sides(side_in, refs[len(refs) - n_side:])
    if rope_dims:
        pos_ref, freq_ref = refs[n_w:n_w + 2]
        tabs = [_rope_slab_tables(pos_ref[...], freq_ref[r:r + 1, :], hd, half)
                for r, (hd, half) in enumerate(rope_dims)]
    x = x_ref[...].astype(jnp.bfloat16)
    for sg in segs:
        if w_transposed[sg.w]:
            y = lax.dot_general(x, ws[sg.w][sg.col:sg.col + sg.width, :], (((1,), (1,)), ((), ())),
                                preferred_element_type=jnp.float32)
        else:
            y = jnp.dot(x, ws[sg.w][:, sg.col:sg.col + sg.width], preferred_element_type=jnp.float32)
        if sg.scale != 1.0:
            y = y * sg.scale
        if sg.stored > sg.width:
            y = jnp.concatenate([y, jnp.zeros((y.shape[0], sg.stored - sg.width), y.dtype)], axis=1)
        if sg.rope >= 0:
            c, s_up, s_dn = tabs[sg.rope]
            slabs = [_rope_slab(y[:, j * LANES:(j + 1) * LANES], rope_dims[sg.rope][1], c, s_up, s_dn)
                     for j in range(sg.stored // LANES)]
            y = jnp.concatenate(slabs, axis=1) if len(slabs) > 1 else slabs[0]
        o_ref = outs[sg.out]
        o_ref[:, sg.out_col:sg.out_col + sg.stored] = y.astype(o_ref.dtype)


def _fused_project(x, ws, w_transposed, plan, out_dtypes, ropes, sides=()):
    s, kdim = x.shape
    segs, cols, out_cols = [], [0] * len(ws), [0] * len(out_dtypes)
    for wi, width, stored, out, rope, scale in plan:
        assert stored == width or width <= SEG_MAX
        for c0 in range(0, width, SEG_MAX):
            wd = min(SEG_MAX, width - c0)
            segs.append(Seg(wi, cols[wi] + c0, wd, wd if stored == width else stored, out, out_cols[out] + c0,
                            rope, scale))
        cols[wi] += width
        out_cols[out] += stored
    assert all(c == w.shape[0 if t else 1] for c, w, t in zip(cols, ws, w_transposed))
    tm = TM_PROJ
    rope_in, rope_specs = [], []
    if ropes is not None:
        pos, freqs, _ = ropes
        rope_in = [pos, freqs]
        rope_specs = [pl.BlockSpec((tm, 1), lambda i: (i, 0)), _resident(freqs.shape)]
    side_in, side_out, side_shapes = _side_io(sides, s // tm)
    res = pl.pallas_call(
        functools.partial(_fused_proj_kernel, segs=tuple(segs), rope_dims=ropes[2] if ropes is not None else (),
                          w_transposed=tuple(w_transposed), n_side=len(sides)),
        grid=(s // tm,),
        in_specs=[pl.BlockSpec((tm, kdim), lambda i: (i, 0))] + [_resident(w.shape) for w in ws]
                 + rope_specs + side_in,
        out_specs=[pl.BlockSpec((tm, n), lambda i: (i, 0)) for n in out_cols] + side_out,
        out_shape=[jax.ShapeDtypeStruct((s, n), dt) for n, dt in zip(out_cols, out_dtypes)] + side_shapes,
        compiler_params=_cparams(("parallel",)),
        name="fused_proj",
    )(x, *ws, *rope_in, *(sd.arr for sd in sides))
    return res[:len(out_dtypes)], res[len(out_dtypes):]


def _rope_freq_row(head_dim, rot_dim):
    half = rot_dim // 2
    inv = 1.0 / (ROPE_THETA ** (jnp.arange(half, dtype=jnp.float32) / half))
    row = jnp.concatenate([inv, inv, jnp.zeros((head_dim - rot_dim,), jnp.float32)])
    return jnp.tile(row, LANES // head_dim)


def _stack_heads(q_ref, n, groups):
    return jnp.concatenate(
        [q_ref[:, (n * groups + g) * HEAD_DIM:(n * groups + g + 1) * HEAD_DIM] for g in range(groups)], axis=0)


def _store_gated(o_ref, gate_ref, out, n, groups):
    tq = o_ref.shape[0]
    for g in range(groups):
        cols = slice((n * groups + g) * HEAD_DIM, (n * groups + g + 1) * HEAD_DIM)
        gate = gate_ref[:, cols]
        o_ref[:, cols] = (out[g * tq:(g + 1) * tq, :] * (gate * jax.nn.sigmoid(gate))).astype(o_ref.dtype)


COUNT_ROWS = 64


def _mixer_a_kernel(q_ref, iq_ref, iw_ref, gate_ref, ik_ref, k_ref, v_ref, o_ref,
                     score_ref, wt_ref, lo_ref, hi_ref, mn_ref, mx_ref,
                     qn_ref, z_ref, bias_ref, m_ref, acc_ref, *, n_sel, idx_scale):
    tq = q_ref.shape[0]
    groups = q_ref.shape[1] // HEAD_DIM // N_KV
    m_cols = groups * tq
    i = pl.program_id(0)
    nkb2 = (i + 2) // 2
    f32, bf16 = jnp.float32, jnp.bfloat16
    nt = (((1,), (1,)), ((), ()))

    key_in_blk = lax.broadcasted_iota(jnp.int32, (TK2, tq), 0)
    q_pos = lax.broadcasted_iota(jnp.int32, (TK2, tq), 1) + i * tq

    iw = iw_ref[...] * idx_scale
    wt_ref[...] = jnp.concatenate([iw, jnp.zeros((tq, LANES - IDX_HEADS), f32)], axis=1).T

    mn_ref[...] = jnp.full((tq, tq), jnp.inf, f32)
    mx_ref[...] = jnp.full((tq, tq), -jnp.inf, f32)

    def score_body(kb2, carry):
        koff = pl.multiple_of(kb2 * TK2, TK2)
        ik = ik_ref[pl.ds(koff, TK2), :IDX_DIM]
        acc = jnp.zeros((TK2, tq), f32)
        for h in range(IDX_HEADS):
            d = lax.dot_general(ik, iq_ref[:, h * IDX_DIM:(h + 1) * IDX_DIM], nt, preferred_element_type=f32)
            acc = acc + wt_ref[h:h + 1, :] * jnp.maximum(d, 0.0)
        causal = (key_in_blk + kb2 * TK2) <= q_pos
        sc = jnp.where(causal, acc, -jnp.inf)
        score_ref[pl.ds(koff, TK2), :] = sc
        lo_part = jnp.where(causal, acc, jnp.inf)
        mn_ref[...] = jnp.minimum(mn_ref[...], jnp.minimum(lo_part[:tq], lo_part[tq:]))
        mx_ref[...] = jnp.maximum(mx_ref[...], jnp.maximum(sc[:tq], sc[tq:]))
        return carry

    lax.fori_loop(0, nkb2, score_body, 0)

    t1 = (lax.broadcasted_iota(jnp.int32, (1, tq), 1) + i * tq + 1).astype(f32)
    k_t = jnp.minimum(t1, float(n_sel))
    row_min = jnp.min(mn_ref[...], axis=0, keepdims=True)
    row_max = jnp.max(mx_ref[...], axis=0, keepdims=True)
    if tq >= n_sel:
        lo0 = jnp.where(t1 >= float(tq), jnp.min(mx_ref[...], axis=0, keepdims=True), row_min)
    else:
        lo0 = row_min
    lo_ref[...] = jnp.broadcast_to(lo0, lo_ref.shape)
    hi_ref[...] = jnp.broadcast_to(jnp.where(t1 <= float(n_sel), row_min, row_max), hi_ref.shape)

    def unresolved():
        lo, hi = lo_ref[0:1, :], hi_ref[0:1, :]
        mid = 0.5 * lo + 0.5 * hi
        open_ = jnp.where(mid > lo, 1.0, 0.0) * jnp.where(mid < hi, 1.0, 0.0)
        return (jnp.max(open_) > 0.0).astype(jnp.int32)

    def bisect_step():
        lo, hi = lo_ref[0:1, :], hi_ref[0:1, :]
        mid = 0.5 * lo + 0.5 * hi

        def count_body(kb2, acc):
            koff = pl.multiple_of(kb2 * TK2, TK2)
            for c in range(TK2 // COUNT_ROWS):
                blk = score_ref[pl.ds(koff + c * COUNT_ROWS, COUNT_ROWS), :]
                acc = acc + jnp.where(blk >= mid, 1.0, 0.0)
            return acc

        acc = lax.fori_loop(0, nkb2, count_body, jnp.zeros((COUNT_ROWS, tq), f32))
        cnt = jnp.sum(acc, axis=0, keepdims=True)
        ge = cnt >= k_t
        lo_ref[...] = jnp.broadcast_to(jnp.where(ge, mid, lo), lo_ref.shape)
        hi_ref[...] = jnp.broadcast_to(jnp.where(cnt == k_t, mid, jnp.where(ge, hi, mid)), hi_ref.shape)

    def bisect_body(state):
        for _ in range(BISECT_STEPS_PER_CHECK):
            bisect_step()
        return state[0] + BISECT_STEPS_PER_CHECK, unresolved()

    lax.while_loop(lambda st: jnp.logical_and(st[1] > 0, st[0] < MAX_BISECT), bisect_body,
                   (jnp.int32(0), unresolved()))

    thr = lo_ref[0:1, :]
    m_rows = groups * tq
    for n in range(N_KV):
        qn_ref[n] = _stack_heads(q_ref, n, groups)
    m_ref[...] = jnp.full(m_ref.shape, NEG_BIG, f32)
    acc_ref[...] = jnp.zeros(acc_ref.shape, f32)

    def logits(kb2, n):
        kblk = k_ref[pl.ds(pl.multiple_of(kb2 * TK2, TK2), TK2), n * HEAD_DIM:(n + 1) * HEAD_DIM]
        return lax.dot_general(qn_ref[n], kblk, nt, preferred_element_type=f32).astype(bf16)

    def mask_bias(kb2):
        blk = score_ref[pl.ds(pl.multiple_of(kb2 * TK2, TK2), TK2), :]
        return jnp.where(blk >= thr, 0.0, NEG_BIG).astype(bf16).T

    for n in range(N_KV):
        z_ref[n] = logits(0, n)
    bias_ref[...] = mask_bias(0)

    def attend(kb2, prefetch):
        koff = pl.multiple_of(kb2 * TK2, TK2)
        bias = jnp.concatenate([bias_ref[...]] * groups, axis=0)
        if prefetch:
            bias_ref[...] = mask_bias(kb2 + 1)
        for n in range(N_KV):
            vblk = v_ref[pl.ds(koff, TK2), n * HEAD_DIM:(n + 1) * HEAD_DIM]
            vext = jnp.concatenate([vblk, jnp.ones_like(vblk)], axis=1)
            z = z_ref[n] + bias
            if prefetch:
                z_ref[n] = logits(kb2 + 1, n)
            m_old = m_ref[n]
            m_new = jnp.maximum(m_old, jnp.max(z, axis=1, keepdims=True).astype(f32))
            p = jnp.exp2(z - jnp.concatenate([m_new.astype(bf16)] * (TK2 // LANES), axis=1))
            alpha = jnp.exp2(m_old - m_new)
            acc_ref[n] = (jnp.concatenate([alpha, alpha], axis=1) * acc_ref[n]
                          + jnp.dot(p, vext, preferred_element_type=f32))
            m_ref[n] = m_new

    def attn_body(kb2, carry):
        attend(kb2, True)
        return carry

    lax.fori_loop(0, nkb2 - 1, attn_body, 0)
    attend(nkb2 - 1, False)
    for n in range(N_KV):
        acc = acc_ref[n]
        _store_gated(o_ref, gate_ref, acc[:, :HEAD_DIM] / acc[:, HEAD_DIM:], n, groups)


def _mixer_a(q, ix, iw, gate, k, v, n_sel):
    s, width = q.shape
    groups = width // HEAD_DIM // N_KV
    f32 = jnp.float32
    idxw = IDX_HEADS * IDX_DIM
    ik_spec = pl.BlockSpec((s, LANES), lambda i: (0, idxw // LANES), pipeline_mode=pl.Buffered(1))
    kern = functools.partial(_mixer_a_kernel, n_sel=n_sel, idx_scale=(IDX_DIM ** -0.5) * (IDX_HEADS ** -0.5))
    return pl.pallas_call(
        kern,
        grid=(s // TQ,),
        in_specs=[pl.BlockSpec((TQ, width), lambda i: (i, 0)),
                  pl.BlockSpec((TQ, IDX_HEADS * IDX_DIM), lambda i: (i, 0)),
                  pl.BlockSpec((TQ, IDX_HEADS), lambda i: (i, 0)),
                  pl.BlockSpec((TQ, width), lambda i: (i, 0)),
                  ik_spec, _resident(k.shape), _resident(v.shape)],
        out_specs=pl.BlockSpec((TQ, width), lambda i: (i, 0)),
        out_shape=jax.ShapeDtypeStruct((s, width), jnp.bfloat16),
        scratch_shapes=[pltpu.VMEM((s, TQ), f32),
                        pltpu.VMEM((LANES, TQ), f32),
                        pltpu.VMEM((8, TQ), f32),
                        pltpu.VMEM((8, TQ), f32),
                        pltpu.VMEM((TQ, TQ), f32),
                        pltpu.VMEM((TQ, TQ), f32),
                        pltpu.VMEM((N_KV, groups * TQ, HEAD_DIM), jnp.bfloat16),
                        pltpu.VMEM((N_KV, groups * TQ, TK2), jnp.bfloat16),
                        pltpu.VMEM((TQ, TK2), jnp.bfloat16),
                        pltpu.VMEM((N_KV, groups * TQ, LANES), f32),
                        pltpu.VMEM((N_KV, groups * TQ, 2 * HEAD_DIM), f32)],
        compiler_params=_cparams(("parallel",)),
        name="mixer_a",
    )(q, ix, iw, gate, ix, k, v)


def _mixer_b_kernel(q_ref, gate_ref, k_ref, v_ref, o_ref, qn_ref, c_ref, acc_ref):
    tq = q_ref.shape[0]
    tk = tq
    groups = q_ref.shape[1] // HEAD_DIM // N_KV
    m_rows = groups * tq
    i = pl.program_id(0)
    f32, bf16 = jnp.float32, jnp.bfloat16

    row = lax.broadcasted_iota(jnp.int32, (tq, tk), 0)
    col = lax.broadcasted_iota(jnp.int32, (tq, tk), 1)
    later = jnp.where(row > col, 1.0, 0.0).astype(bf16)
    later2 = jnp.concatenate([later, later], axis=0)
    strict = jnp.concatenate([col < row] * groups, axis=0)

    for n in range(N_KV):
        qn_ref[n] = _stack_heads(q_ref, n, groups)

    def step(kb, diagonal):
        koff = pl.multiple_of(kb * tk, tk)
        c_min = None
        for n in range(N_KV):
            kblk = k_ref[pl.ds(koff, tk), n * HEAD_DIM:(n + 1) * HEAD_DIM]
            vblk = v_ref[pl.ds(koff, tk), n * HEAD_DIM:(n + 1) * HEAD_DIM]
            z = lax.dot_general(qn_ref[n], kblk, (((1,), (1,)), ((), ())), preferred_element_type=f32)
            t = jnp.log(1.0 + jnp.exp2(jnp.abs(z) * (-LOG2_E)))
            sp = jnp.maximum(z, 0.0) + t
            if diagonal:
                sp = jnp.where(strict, sp, 0.0)
            sp_hi = sp.astype(bf16)
            sp_lo = (sp - sp_hi.astype(f32)).astype(bf16)
            suffix = jnp.dot(jnp.concatenate([sp_hi, sp_lo], axis=1), later2, preferred_element_type=f32)
            c_old = c_ref[n]
            total = suffix + jnp.concatenate([c_old] * (tk // LANES), axis=1)
            a = jnp.exp((jnp.minimum(z, 0.0) - t) - total)
            if diagonal:
                a = jnp.where(strict, a, 0.0)
            acc_ref[n] += jnp.dot(a.astype(bf16), vblk, preferred_element_type=f32)
            c_new = c_old + jnp.broadcast_to(jnp.sum(sp, axis=1, keepdims=True), (m_rows, LANES))
            c_ref[n] = c_new
            c_min = c_new if c_min is None else jnp.minimum(c_min, c_new)
        return (jnp.min(c_min) <= EXP_ZERO_ABOVE).astype(jnp.int32)

    c_ref[...] = jnp.zeros((N_KV, m_rows, LANES), f32)
    acc_ref[...] = jnp.zeros((N_KV, m_rows, HEAD_DIM), f32)
    go = step(i, True)
    lax.while_loop(lambda st: jnp.logical_and(st[0] >= 0, st[1] > 0),
                   lambda st: (st[0] - 1, step(st[0], False)), (i - 1, go))
    for n in range(N_KV):
        _store_gated(o_ref, gate_ref, acc_ref[n], n, groups)


def _mixer_b(q, gate, kv):
    s, width = q.shape
    groups = width // HEAD_DIM // N_KV
    kvw = kv.shape[1] // 2
    half_spec = lambda c: pl.BlockSpec((s, kvw), lambda i: (0, c), pipeline_mode=pl.Buffered(1))
    return pl.pallas_call(
        _mixer_b_kernel,
        grid=(s // TQ,),
        in_specs=[pl.BlockSpec((TQ, width), lambda i: (i, 0)),
                  pl.BlockSpec((TQ, width), lambda i: (i, 0)),
                  half_spec(0), half_spec(1)],
        out_specs=pl.BlockSpec((TQ, width), lambda i: (i, 0)),
        out_shape=jax.ShapeDtypeStruct((s, width), jnp.bfloat16),
        scratch_shapes=[pltpu.VMEM((N_KV, groups * TQ, HEAD_DIM), jnp.bfloat16),
                        pltpu.VMEM((N_KV, groups * TQ, LANES), jnp.float32),
                        pltpu.VMEM((N_KV, groups * TQ, HEAD_DIM), jnp.float32)],
        compiler_params=_cparams(("parallel",)),
        name="mixer_b",
    )(q, gate, kv, kv)


def _post_kernel(og_ref, x_ref, p_ref, wo_ref, wg_ref, wp_ref, lng_ref, lnb_ref, *refs, alpha, n_side):
    f32 = jnp.float32
    o_ref = refs[n_side]
    _cast_sides(refs[:n_side], refs[n_side + 1:])
    h = jnp.dot(og_ref[...], wo_ref[...], preferred_element_type=f32)
    y = alpha * x_ref[...] + h
    mu = jnp.mean(y, axis=-1, keepdims=True)
    d = y - mu
    var = jnp.mean(d * d, axis=-1, keepdims=True)
    yn = d * lax.rsqrt(var + LN_EPS) * lng_ref[...] + lnb_ref[...]
    gl = jnp.dot(yn.astype(jnp.bfloat16), wg_ref[...], preferred_element_type=f32)
    pe = jnp.dot(p_ref[...].astype(jnp.bfloat16), wp_ref[...], preferred_element_type=f32)
    o_ref[...] = yn + pe * jax.nn.sigmoid(gl)


def _resident_slab(arr, idx):
    return pl.BlockSpec((None,) + arr.shape[1:], lambda *_: (idx,) + (0,) * (arr.ndim - 1),
                        pipeline_mode=pl.Buffered(1))


def _post(og, x, p, layer, batch, wo, wg, wp, lng, lnb, alpha, sides=()):
    s, d = x.shape
    tm = TM_PROJ
    side_in, side_out, side_shapes = _side_io(sides, s // tm)
    res = pl.pallas_call(
        functools.partial(_post_kernel, alpha=alpha, n_side=len(sides)),
        grid=(s // tm,),
        in_specs=[pl.BlockSpec((tm, og.shape[1]), lambda i: (i, 0)),
                  pl.BlockSpec((tm, d), lambda i: (i, 0)),
                  pl.BlockSpec((None, None, tm, p.shape[3]), lambda i: (layer, batch, i, 0)),
                  _resident(wo.shape), _resident(wg.shape), _resident_slab(wp, layer),
                  _resident_slab(lng, layer), _resident_slab(lnb, layer)] + side_in,
        out_specs=[pl.BlockSpec((tm, d), lambda i: (i, 0))] + side_out,
        out_shape=[jax.ShapeDtypeStruct((s, d), jnp.float32)] + side_shapes,
        compiler_params=_cparams(("parallel",)),
        name="post",
    )(og, x, p, wo, wg, wp, lng, lnb, *(sd.arr for sd in sides))
    return res[0], res[1:]


def _now_bf16(sd):
    return (sd.arr if sd.layer < 0 else sd.arr[sd.layer]).astype(jnp.bfloat16)


def kernel(x, p, positions, w_in_a, w_o_a, w_kv_b, w_in_b, w_o_b, ln_g, ln_b, w_ple, w_ple_gate):
    bsz, s, d = x.shape
    n_a, n_b = w_in_a.shape[0], w_in_b.shape[0]
    depth = n_a + n_b
    alpha = (2.0 * depth) ** 0.25
    width = w_o_a.shape[1]
    kvw = N_KV * HEAD_DIM
    idxw = IDX_HEADS * IDX_DIM
    n_sel = min(TOPK_MAX, s // 4)
    bf16, f32 = jnp.bfloat16, jnp.float32
    sm_scale = HEAD_DIM ** -0.5
    assert s % TK2 == 0
    n_steps = s // TM_PROJ
    wp_all = w_ple.astype(bf16)
    lng, lnb = ln_g[:, None, :], ln_b[:, None, :]

    def proj_sources(layer):
        if layer < n_a:
            return [Side(jnp.swapaxes(w_in_a[layer], 0, 1), -1)]
        first_b = [Side(w_kv_b, -1)] if layer == n_a else []
        return first_b + [Side(w_in_b, layer - n_a)]

    def riders(srcs):
        return srcs if all(_side_ok(sd.arr, n_steps) for sd in srcs) else []

    proj_w, post_w = {}, {}
    outs = []
    for b in range(bsz):
        xb = x[b]
        ropes = (positions[b].astype(f32)[:, None],
                 jnp.stack([_rope_freq_row(HEAD_DIM, ROT_DIM), _rope_freq_row(IDX_DIM, IDX_ROT)]),
                 ((HEAD_DIM, ROT_DIM // 2), (IDX_DIM, IDX_ROT // 2)))
        kv_shared = None
        for i in range(depth):
            if i not in proj_w:
                proj_w[i] = [_now_bf16(sd) for sd in proj_sources(i)]
            post_src = [Side(w_o_a, i) if i < n_a else Side(w_o_b, i - n_a), Side(w_ple_gate, i)]
            post_ride = riders(post_src) if i not in post_w else []
            if i < n_a:
                plan = [(0, width, width, 0, 0, LOG2_E * sm_scale), (0, kvw, kvw, 1, 0, 1.0),
                        (0, kvw, kvw, 2, -1, 1.0), (0, width, width, 3, -1, 1.0), (0, idxw, idxw, 4, 1, 1.0),
                        (0, IDX_DIM, LANES, 4, 1, 1.0), (0, IDX_HEADS, IDX_HEADS, 5, -1, 1.0)]
                (q, k, v, gate, ix, iw), cast = _fused_project(
                    xb, proj_w[i], [True], plan, [bf16, bf16, bf16, f32, bf16, f32], ropes, post_ride)
                og = _mixer_a(q, ix, iw, gate, k, v, n_sel)
            elif kv_shared is None:
                plan = [(0, 2 * kvw, 2 * kvw, 0, -1, 1.0), (1, width, width, 1, -1, sm_scale),
                        (1, width, width, 2, -1, 1.0)]
                (kv_shared, q, gate), cast = _fused_project(xb, proj_w[i], [False, False], plan, [bf16, bf16, f32],
                                                            None, post_ride)
                og = _mixer_b(q, gate, kv_shared)
            else:
                plan = [(0, width, width, 0, -1, sm_scale), (0, width, width, 1, -1, 1.0)]
                (q, gate), cast = _fused_project(xb, proj_w[i], [False], plan, [bf16, f32], None, post_ride)
                og = _mixer_b(q, gate, kv_shared)
            if i not in post_w:
                post_w[i] = tuple(cast) if post_ride else tuple(_now_bf16(sd) for sd in post_src)
            next_ride = riders(proj_sources(i + 1)) if i + 1 < depth and i + 1 not in proj_w else []
            xb, cast = _post(og, xb, p, i, b, post_w[i][0], post_w[i][1], wp_all, lng, lnb, alpha, next_ride)
            if next_ride:
                proj_w[i + 1] = list(cast)
        outs.append(xb)
    return jnp.stack(outs, axis=0)
```

```python
import functools
from typing import NamedTuple

import jax
import jax.numpy as jnp
from jax import lax
from jax.experimental import pallas as pl
from jax.experimental.pallas import tpu as pltpu

HEAD_DIM = 128
N_KV = 4
ROT_DIM = HEAD_DIM // 4
IDX_HEADS = 16
IDX_DIM = 64
IDX_ROT = IDX_DIM // 4
TOPK_MAX = 256
ROPE_THETA = 500000.0
LN_EPS = 1e-5
LOG2_E = 1.4426950408889634

LANES = 128
BF16_SUBLANES = 16
V7X_VMEM_BYTES = 64 * 1024 * 1024
VMEM_LIMIT = V7X_VMEM_BYTES - 8 * 1024 * 1024

TQ = 256
TK2 = 2 * TQ
TM_PROJ = 256
SEG_MAX = 1024
NEG_BIG = -1e30
EXP_ZERO_ABOVE = 110.0
MAX_BISECT = 256
BISECT_STEPS_PER_CHECK = 2
COARSE_STEPS = 9


def _cparams(sem):
    return pltpu.CompilerParams(dimension_semantics=sem, vmem_limit_bytes=VMEM_LIMIT)


def _resident(shape):
    nd = len(shape)
    return pl.BlockSpec(shape, lambda *_: (0,) * nd, pipeline_mode=pl.Buffered(1))


class Seg(NamedTuple):
    w: int
    col: int
    width: int
    stored: int
    out: int
    out_col: int
    rope: int
    scale: float


def _rope_slab(y, half, c, s_up, s_dn):
    up = pltpu.roll(y, LANES - half, axis=1)
    dn = pltpu.roll(y, half, axis=1)
    return y * c + up * s_up + dn * s_dn


class Side(NamedTuple):
    arr: jax.Array
    layer: int


def _side_ok(arr, n_steps):
    rows = arr.shape[-2]
    return rows % (n_steps * BF16_SUBLANES) == 0


def _side_io(sides, n_steps):
    in_specs, out_specs, out_shapes = [], [], []
    for sd in sides:
        r, c = sd.arr.shape[-2:]
        rows = r // n_steps
        if sd.layer < 0:
            in_specs.append(pl.BlockSpec((rows, c), lambda i: (i, 0)))
        else:
            in_specs.append(pl.BlockSpec((None, rows, c), lambda i, l=sd.layer: (l, i, 0)))
        out_specs.append(pl.BlockSpec((rows, c), lambda i: (i, 0)))
        out_shapes.append(jax.ShapeDtypeStruct((r, c), jnp.bfloat16))
    return in_specs, out_specs, out_shapes


def _cast_sides(side_in, side_out):
    for src, dst in zip(side_in, side_out):
        dst[...] = src[...].astype(dst.dtype)


def _rope_slab_tables(pos, freq, head_dim, half):
    ang = pos * freq
    c, sn = jnp.cos(ang), jnp.sin(ang)
    lane = lax.broadcasted_iota(jnp.int32, ang.shape, 1) & (head_dim - 1)
    s_up = jnp.where(lane < half, -sn, 0.0)
    s_dn = jnp.where(lane < half, 0.0, jnp.where(lane < 2 * half, sn, 0.0))
    return c, s_up, s_dn


def _fused_proj_kernel(x_ref, *refs, segs, rope_dims, w_transposed, n_side):
    n_w, n_rope = len(w_transposed), 2 if rope_dims else 0
    ws = refs[:n_w]
    side_in = refs[n_w + n_rope:n_w + n_rope + n_side]
    outs = refs[n_w + n_rope + n_side:len(refs) - n_side]
    _cast_sides(side_in, refs[len(refs) - n_side:])
    if rope_dims:
        pos_ref, freq_ref = refs[n_w:n_w + 2]
        tabs = [_rope_slab_tables(pos_ref[...], freq_ref[r:r + 1, :], hd, half)
                for r, (hd, half) in enumerate(rope_dims)]
    x = x_ref[...].astype(jnp.bfloat16)
    for sg in segs:
        if w_transposed[sg.w]:
            y = lax.dot_general(x, ws[sg.w][sg.col:sg.col + sg.width, :], (((1,), (1,)), ((), ())),
                                preferred_element_type=jnp.float32)
        else:
            y = jnp.dot(x, ws[sg.w][:, sg.col:sg.col + sg.width], preferred_element_type=jnp.float32)
        if sg.scale != 1.0:
            y = y * sg.scale
        if sg.stored > sg.width:
            y = jnp.concatenate([y, jnp.zeros((y.shape[0], sg.stored - sg.width), y.dtype)], axis=1)
        if sg.rope >= 0:
            c, s_up, s_dn = tabs[sg.rope]
            slabs = [_rope_slab(y[:, j * LANES:(j + 1) * LANES], rope_dims[sg.rope][1], c, s_up, s_dn)
                     for j in range(sg.stored // LANES)]
            y = jnp.concatenate(slabs, axis=1) if len(slabs) > 1 else slabs[0]
        o_ref = outs[sg.out]
        o_ref[:, sg.out_col:sg.out_col + sg.stored] = y.astype(o_ref.dtype)


def _fused_project(x, ws, w_transposed, plan, out_dtypes, ropes, sides=()):
    s, kdim = x.shape
    segs, cols, out_cols = [], [0] * len(ws), [0] * len(out_dtypes)
    for wi, width, stored, out, rope, scale in plan:
        assert stored == width or width <= SEG_MAX
        for c0 in range(0, width, SEG_MAX):
            wd = min(SEG_MAX, width - c0)
            segs.append(Seg(wi, cols[wi] + c0, wd, wd if stored == width else stored, out, out_cols[out] + c0,
                            rope, scale))
        cols[wi] += width
        out_cols[out] += stored
    assert all(c == w.shape[0 if t else 1] for c, w, t in zip(cols, ws, w_transposed))
    tm = TM_PROJ
    rope_in, rope_specs = [], []
    if ropes is not None:
        pos, freqs, _ = ropes
        rope_in = [pos, freqs]
        rope_specs = [pl.BlockSpec((tm, 1), lambda i: (i, 0)), _resident(freqs.shape)]
    side_in, side_out, side_shapes = _side_io(sides, s // tm)
    res = pl.pallas_call(
        functools.partial(_fused_proj_kernel, segs=tuple(segs), rope_dims=ropes[2] if ropes is not None else (),
                          w_transposed=tuple(w_transposed), n_side=len(sides)),
        grid=(s // tm,),
        in_specs=[pl.BlockSpec((tm, kdim), lambda i: (i, 0))] + [_resident(w.shape) for w in ws]
                 + rope_specs + side_in,
        out_specs=[pl.BlockSpec((tm, n), lambda i: (i, 0)) for n in out_cols] + side_out,
        out_shape=[jax.ShapeDtypeStruct((s, n), dt) for n, dt in zip(out_cols, out_dtypes)] + side_shapes,
        compiler_params=_cparams(("parallel",)),
        name="fused_proj",
    )(x, *ws, *rope_in, *(sd.arr for sd in sides))
    return res[:len(out_dtypes)], res[len(out_dtypes):]


def _rope_freq_row(head_dim, rot_dim):
    half = rot_dim // 2
    inv = 1.0 / (ROPE_THETA ** (jnp.arange(half, dtype=jnp.float32) / half))
    row = jnp.concatenate([inv, inv, jnp.zeros((head_dim - rot_dim,), jnp.float32)])
    return jnp.tile(row, LANES // head_dim)


def _stack_heads(q_ref, n, groups):
    return jnp.concatenate(
        [q_ref[:, (n * groups + g) * HEAD_DIM:(n * groups + g + 1) * HEAD_DIM] for g in range(groups)], axis=0)


def _store_gated(o_ref, gate_ref, out, n, groups):
    tq = o_ref.shape[0]
    for g in range(groups):
        cols = slice((n * groups + g) * HEAD_DIM, (n * groups + g + 1) * HEAD_DIM)
        gate = gate_ref[:, cols]
        o_ref[:, cols] = (out[g * tq:(g + 1) * tq, :] * (gate * jax.nn.sigmoid(gate))).astype(o_ref.dtype)


COUNT_ROWS = 64


def _mixer_a_kernel(q_ref, iq_ref, iw_ref, gate_ref, ik_ref, k_ref, v_ref, o_ref,
                     score_ref, sb_ref, wt_ref, lo_ref, hi_ref, mn_ref, mx_ref,
                     qn_ref, z_ref, bias_ref, m_ref, acc_ref, *, n_sel, idx_scale):
    tq = q_ref.shape[0]
    groups = q_ref.shape[1] // HEAD_DIM // N_KV
    m_cols = groups * tq
    i = pl.program_id(0)
    nkb2 = (i + 2) // 2
    f32, bf16 = jnp.float32, jnp.bfloat16
    nt = (((1,), (1,)), ((), ()))

    key_in_blk = lax.broadcasted_iota(jnp.int32, (TK2, tq), 0)
    q_pos = lax.broadcasted_iota(jnp.int32, (TK2, tq), 1) + i * tq

    iw = iw_ref[...] * idx_scale
    wt_ref[...] = jnp.concatenate([iw, jnp.zeros((tq, LANES - IDX_HEADS), f32)], axis=1).T

    mn_ref[...] = jnp.full((tq, tq), jnp.inf, f32)
    mx_ref[...] = jnp.full((tq, tq), -jnp.inf, f32)

    def score_body(kb2, carry):
        koff = pl.multiple_of(kb2 * TK2, TK2)
        ik = ik_ref[pl.ds(koff, TK2), :IDX_DIM]
        acc = jnp.zeros((TK2, tq), f32)
        for h in range(IDX_HEADS):
            d = lax.dot_general(ik, iq_ref[:, h * IDX_DIM:(h + 1) * IDX_DIM], nt, preferred_element_type=f32)
            acc = acc + wt_ref[h:h + 1, :] * jnp.maximum(d, 0.0)
        causal = (key_in_blk + kb2 * TK2) <= q_pos
        sc = jnp.where(causal, acc, -jnp.inf)
        score_ref[pl.ds(koff, TK2), :] = sc
        sb_ref[pl.ds(koff, TK2), :] = sc.astype(bf16)
        lo_part = jnp.where(causal, acc, jnp.inf)
        mn_ref[...] = jnp.minimum(mn_ref[...], jnp.minimum(lo_part[:tq], lo_part[tq:]))
        mx_ref[...] = jnp.maximum(mx_ref[...], jnp.maximum(sc[:tq], sc[tq:]))
        return carry

    lax.fori_loop(0, nkb2, score_body, 0)

    t1 = (lax.broadcasted_iota(jnp.int32, (1, tq), 1) + i * tq + 1).astype(f32)
    k_t = jnp.minimum(t1, float(n_sel))
    row_min = jnp.min(mn_ref[...], axis=0, keepdims=True)
    row_max = jnp.max(mx_ref[...], axis=0, keepdims=True)
    if tq >= n_sel:
        lo0 = jnp.where(t1 >= float(tq), jnp.min(mx_ref[...], axis=0, keepdims=True), row_min)
    else:
        lo0 = row_min
    def to_b(x):
        return x.astype(bf16).astype(f32)

    def coarse_step(_, bracket):
        lo_b, hi_b = bracket
        mid = to_b(0.5 * lo_b + 0.5 * hi_b)
        mid16 = mid.astype(bf16)

        def count_body(kb2, acc):
            koff = pl.multiple_of(kb2 * TK2, TK2)
            for c in range(TK2 // COUNT_ROWS):
                blk = sb_ref[pl.ds(koff + c * COUNT_ROWS, COUNT_ROWS), :]
                acc = acc + jnp.where(blk >= mid16, jnp.ones((), bf16), jnp.zeros((), bf16))
            return acc

        acc = lax.fori_loop(0, nkb2, count_body, jnp.zeros((COUNT_ROWS, tq), bf16))
        ge = jnp.sum(acc.astype(f32), axis=0, keepdims=True) >= k_t
        return jnp.where(ge, mid, lo_b), jnp.where(ge, hi_b, mid)

    hi_b0 = to_b(row_max + jnp.abs(row_max) * 2.0 ** -6 + 1e-30)
    lo_b, hi_b = lax.fori_loop(0, COARSE_STEPS, coarse_step, (to_b(lo0), hi_b0))
    lo1 = jnp.maximum(lo0, lo_b - jnp.abs(lo_b) * 2.0 ** -7 - 1e-30)
    hi1 = jnp.minimum(row_max, hi_b)

    settled = t1 <= float(n_sel)
    lo_ref[...] = jnp.broadcast_to(jnp.where(settled, lo0, lo1), lo_ref.shape)
    hi_ref[...] = jnp.broadcast_to(jnp.where(settled, row_min, hi1), hi_ref.shape)

    def unresolved():
        lo, hi = lo_ref[0:1, :], hi_ref[0:1, :]
        mid = 0.5 * lo + 0.5 * hi
        open_ = jnp.where(mid > lo, 1.0, 0.0) * jnp.where(mid < hi, 1.0, 0.0)
        return (jnp.max(open_) > 0.0).astype(jnp.int32)

    def bisect_step():
        lo, hi = lo_ref[0:1, :], hi_ref[0:1, :]
        mid = 0.5 * lo + 0.5 * hi

        def count_body(kb2, acc):
            koff = pl.multiple_of(kb2 * TK2, TK2)
            for c in range(TK2 // COUNT_ROWS):
                blk = score_ref[pl.ds(koff + c * COUNT_ROWS, COUNT_ROWS), :]
                acc = acc + jnp.where(blk >= mid, 1.0, 0.0)
            return acc

        acc = lax.fori_loop(0, nkb2, count_body, jnp.zeros((COUNT_ROWS, tq), f32))
        cnt = jnp.sum(acc, axis=0, keepdims=True)
        ge = cnt >= k_t
        lo_ref[...] = jnp.broadcast_to(jnp.where(ge, mid, lo), lo_ref.shape)
        hi_ref[...] = jnp.broadcast_to(jnp.where(cnt == k_t, mid, jnp.where(ge, hi, mid)), hi_ref.shape)

    def bisect_body(state):
        for _ in range(BISECT_STEPS_PER_CHECK):
            bisect_step()
        return state[0] + BISECT_STEPS_PER_CHECK, unresolved()

    lax.while_loop(lambda st: jnp.logical_and(st[1] > 0, st[0] < MAX_BISECT), bisect_body,
                   (jnp.int32(0), unresolved()))

    thr = lo_ref[0:1, :]
    m_rows = groups * tq
    for n in range(N_KV):
        qn_ref[n] = _stack_heads(q_ref, n, groups)
    m_ref[...] = jnp.full(m_ref.shape, NEG_BIG, f32)
    acc_ref[...] = jnp.zeros(acc_ref.shape, f32)

    def logits(kb2, n):
        kblk = k_ref[pl.ds(pl.multiple_of(kb2 * TK2, TK2), TK2), n * HEAD_DIM:(n + 1) * HEAD_DIM]
        return lax.dot_general(qn_ref[n], kblk, nt, preferred_element_type=f32).astype(bf16)

    def mask_bias(kb2):
        blk = score_ref[pl.ds(pl.multiple_of(kb2 * TK2, TK2), TK2), :]
        return jnp.where(blk >= thr, 0.0, NEG_BIG).astype(bf16).T

    for n in range(N_KV):
        z_ref[n] = logits(0, n)
    bias_ref[...] = mask_bias(0)

    def attend(kb2, prefetch):
        koff = pl.multiple_of(kb2 * TK2, TK2)
        bias = jnp.concatenate([bias_ref[...]] * groups, axis=0)
        if prefetch:
            bias_ref[...] = mask_bias(kb2 + 1)
        for n in range(N_KV):
            vblk = v_ref[pl.ds(koff, TK2), n * HEAD_DIM:(n + 1) * HEAD_DIM]
            vext = jnp.concatenate([vblk, jnp.ones_like(vblk)], axis=1)
            z = z_ref[n] + bias
            if prefetch:
                z_ref[n] = logits(kb2 + 1, n)
            m_old = m_ref[n]
            m_new = jnp.maximum(m_old, jnp.max(z, axis=1, keepdims=True).astype(f32))
            p = jnp.exp2(z - jnp.concatenate([m_new.astype(bf16)] * (TK2 // LANES), axis=1))
            alpha = jnp.exp2(m_old - m_new)
            acc_ref[n] = (jnp.concatenate([alpha, alpha], axis=1) * acc_ref[n]
                          + jnp.dot(p, vext, preferred_element_type=f32))
            m_ref[n] = m_new

    def attn_body(kb2, carry):
        attend(kb2, True)
        return carry

    lax.fori_loop(0, nkb2 - 1, attn_body, 0)
    attend(nkb2 - 1, False)
    for n in range(N_KV):
        acc = acc_ref[n]
        _store_gated(o_ref, gate_ref, acc[:, :HEAD_DIM] / acc[:, HEAD_DIM:], n, groups)


def _mixer_a(q, ix, iw, gate, k, v, n_sel):
    s, width = q.shape
    groups = width // HEAD_DIM // N_KV
    f32 = jnp.float32
    idxw = IDX_HEADS * IDX_DIM
    ik_spec = pl.BlockSpec((s, LANES), lambda i: (0, idxw // LANES), pipeline_mode=pl.Buffered(1))
    kern = functools.partial(_mixer_a_kernel, n_sel=n_sel, idx_scale=(IDX_DIM ** -0.5) * (IDX_HEADS ** -0.5))
    return pl.pallas_call(
        kern,
        grid=(s // TQ,),
        in_specs=[pl.BlockSpec((TQ, width), lambda i: (i, 0)),
                  pl.BlockSpec((TQ, IDX_HEADS * IDX_DIM), lambda i: (i, 0)),
                  pl.BlockSpec((TQ, IDX_HEADS), lambda i: (i, 0)),
                  pl.BlockSpec((TQ, width), lambda i: (i, 0)),
                  ik_spec, _resident(k.shape), _resident(v.shape)],
        out_specs=pl.BlockSpec((TQ, width), lambda i: (i, 0)),
        out_shape=jax.ShapeDtypeStruct((s, width), jnp.bfloat16),
        scratch_shapes=[pltpu.VMEM((s, TQ), f32),
                        pltpu.VMEM((s, TQ), jnp.bfloat16),
                        pltpu.VMEM((LANES, TQ), f32),
                        pltpu.VMEM((8, TQ), f32),
                        pltpu.VMEM((8, TQ), f32),
                        pltpu.VMEM((TQ, TQ), f32),
                        pltpu.VMEM((TQ, TQ), f32),
                        pltpu.VMEM((N_KV, groups * TQ, HEAD_DIM), jnp.bfloat16),
                        pltpu.VMEM((N_KV, groups * TQ, TK2), jnp.bfloat16),
                        pltpu.VMEM((TQ, TK2), jnp.bfloat16),
                        pltpu.VMEM((N_KV, groups * TQ, LANES), f32),
                        pltpu.VMEM((N_KV, groups * TQ, 2 * HEAD_DIM), f32)],
        compiler_params=_cparams(("parallel",)),
        name="mixer_a",
    )(q, ix, iw, gate, ix, k, v)


def _mixer_b_kernel(q_ref, gate_ref, k_ref, v_ref, o_ref, qn_ref, c_ref, acc_ref):
    tq = q_ref.shape[0]
    tk = tq
    groups = q_ref.shape[1] // HEAD_DIM // N_KV
    m_rows = groups * tq
    i = pl.program_id(0)
    f32, bf16 = jnp.float32, jnp.bfloat16

    row = lax.broadcasted_iota(jnp.int32, (tq, tk), 0)
    col = lax.broadcasted_iota(jnp.int32, (tq, tk), 1)
    later = jnp.where(row > col, 1.0, 0.0).astype(bf16)
    later2 = jnp.concatenate([later, later], axis=0)
    strict = jnp.concatenate([col < row] * groups, axis=0)

    for n in range(N_KV):
        qn_ref[n] = _stack_heads(q_ref, n, groups)

    def step(kb, diagonal):
        koff = pl.multiple_of(kb * tk, tk)
        c_min = None
        for n in range(N_KV):
            kblk = k_ref[pl.ds(koff, tk), n * HEAD_DIM:(n + 1) * HEAD_DIM]
            vblk = v_ref[pl.ds(koff, tk), n * HEAD_DIM:(n + 1) * HEAD_DIM]
            z = lax.dot_general(qn_ref[n], kblk, (((1,), (1,)), ((), ())), preferred_element_type=f32)
            t = jnp.log(1.0 + jnp.exp2(jnp.abs(z) * (-LOG2_E)))
            sp = jnp.maximum(z, 0.0) + t
            if diagonal:
                sp = jnp.where(strict, sp, 0.0)
            sp_hi = sp.astype(bf16)
            sp_lo = (sp - sp_hi.astype(f32)).astype(bf16)
            suffix = jnp.dot(jnp.concatenate([sp_hi, sp_lo], axis=1), later2, preferred_element_type=f32)
            c_old = c_ref[n]
            total = suffix + jnp.concatenate([c_old] * (tk // LANES), axis=1)
            a = jnp.exp((jnp.minimum(z, 0.0) - t) - total)
            if diagonal:
                a = jnp.where(strict, a, 0.0)
            acc_ref[n] += jnp.dot(a.astype(bf16), vblk, preferred_element_type=f32)
            c_new = c_old + jnp.broadcast_to(jnp.sum(sp, axis=1, keepdims=True), (m_rows, LANES))
            c_ref[n] = c_new
            c_min = c_new if c_min is None else jnp.minimum(c_min, c_new)
        return (jnp.min(c_min) <= EXP_ZERO_ABOVE).astype(jnp.int32)

    c_ref[...] = jnp.zeros((N_KV, m_rows, LANES), f32)
    acc_ref[...] = jnp.zeros((N_KV, m_rows, HEAD_DIM), f32)
    go = step(i, True)
    lax.while_loop(lambda st: jnp.logical_and(st[0] >= 0, st[1] > 0),
                   lambda st: (st[0] - 1, step(st[0], False)), (i - 1, go))
    for n in range(N_KV):
        _store_gated(o_ref, gate_ref, acc_ref[n], n, groups)


def _mixer_b(q, gate, kv):
    s, width = q.shape
    groups = width // HEAD_DIM // N_KV
    kvw = kv.shape[1] // 2
    half_spec = lambda c: pl.BlockSpec((s, kvw), lambda i: (0, c), pipeline_mode=pl.Buffered(1))
    return pl.pallas_call(
        _mixer_b_kernel,
        grid=(s // TQ,),
        in_specs=[pl.BlockSpec((TQ, width), lambda i: (i, 0)),
                  pl.BlockSpec((TQ, width), lambda i: (i, 0)),
                  half_spec(0), half_spec(1)],
        out_specs=pl.BlockSpec((TQ, width), lambda i: (i, 0)),
        out_shape=jax.ShapeDtypeStruct((s, width), jnp.bfloat16),
        scratch_shapes=[pltpu.VMEM((N_KV, groups * TQ, HEAD_DIM), jnp.bfloat16),
                        pltpu.VMEM((N_KV, groups * TQ, LANES), jnp.float32),
                        pltpu.VMEM((N_KV, groups * TQ, HEAD_DIM), jnp.float32)],
        compiler_params=_cparams(("parallel",)),
        name="mixer_b",
    )(q, gate, kv, kv)


def _post_kernel(og_ref, x_ref, p_ref, wo_ref, wg_ref, wp_ref, lng_ref, lnb_ref, *refs, alpha, n_side):
    f32 = jnp.float32
    o_ref = refs[n_side]
    _cast_sides(refs[:n_side], refs[n_side + 1:])
    h = jnp.dot(og_ref[...], wo_ref[...], preferred_element_type=f32)
    y = alpha * x_ref[...] + h
    mu = jnp.mean(y, axis=-1, keepdims=True)
    d = y - mu
    var = jnp.mean(d * d, axis=-1, keepdims=True)
    yn = d * lax.rsqrt(var + LN_EPS) * lng_ref[...] + lnb_ref[...]
    gl = jnp.dot(yn.astype(jnp.bfloat16), wg_ref[...], preferred_element_type=f32)
    pe = jnp.dot(p_ref[...].astype(jnp.bfloat16), wp_ref[...], preferred_element_type=f32)
    o_ref[...] = yn + pe * jax.nn.sigmoid(gl)


def _resident_slab(arr, idx):
    return pl.BlockSpec((None,) + arr.shape[1:], lambda *_: (idx,) + (0,) * (arr.ndim - 1),
                        pipeline_mode=pl.Buffered(1))


def _post(og, x, p, layer, batch, wo, wg, wp, lng, lnb, alpha, sides=()):
    s, d = x.shape
    tm = TM_PROJ
    side_in, side_out, side_shapes = _side_io(sides, s // tm)
    res = pl.pallas_call(
        functools.partial(_post_kernel, alpha=alpha, n_side=len(sides)),
        grid=(s // tm,),
        in_specs=[pl.BlockSpec((tm, og.shape[1]), lambda i: (i, 0)),
                  pl.BlockSpec((tm, d), lambda i: (i, 0)),
                  pl.BlockSpec((None, None, tm, p.shape[3]), lambda i: (layer, batch, i, 0)),
                  _resident(wo.shape), _resident(wg.shape), _resident_slab(wp, layer),
                  _resident_slab(lng, layer), _resident_slab(lnb, layer)] + side_in,
        out_specs=[pl.BlockSpec((tm, d), lambda i: (i, 0))] + side_out,
        out_shape=[jax.ShapeDtypeStruct((s, d), jnp.float32)] + side_shapes,
        compiler_params=_cparams(("parallel",)),
        name="post",
    )(og, x, p, wo, wg, wp, lng, lnb, *(sd.arr for sd in sides))
    return res[0], res[1:]


def _now_bf16(sd):
    return (sd.arr if sd.layer < 0 else sd.arr[sd.layer]).astype(jnp.bfloat16)


def kernel(x, p, positions, w_in_a, w_o_a, w_kv_b, w_in_b, w_o_b, ln_g, ln_b, w_ple, w_ple_gate):
    bsz, s, d = x.shape
    n_a, n_b = w_in_a.shape[0], w_in_b.shape[0]
    depth = n_a + n_b
    alpha = (2.0 * depth) ** 0.25
    width = w_o_a.shape[1]
    kvw = N_KV * HEAD_DIM
    idxw = IDX_HEADS * IDX_DIM
    n_sel = min(TOPK_MAX, s // 4)
    bf16, f32 = jnp.bfloat16, jnp.float32
    sm_scale = HEAD_DIM ** -0.5
    assert s % TK2 == 0 and s // COUNT_ROWS <= 256
    n_steps = s // TM_PROJ
    wp_all = w_ple.astype(bf16)
    lng, lnb = ln_g[:, None, :], ln_b[:, None, :]

    def proj_sources(layer):
        if layer < n_a:
            return [Side(jnp.swapaxes(w_in_a[layer], 0, 1), -1)]
        first_b = [Side(w_kv_b, -1)] if layer == n_a else []
        return first_b + [Side(w_in_b, layer - n_a)]

    def riders(srcs):
        return srcs if all(_side_ok(sd.arr, n_steps) for sd in srcs) else []

    proj_w, post_w = {}, {}
    outs = []
    for b in range(bsz):
        xb = x[b]
        ropes = (positions[b].astype(f32)[:, None],
                 jnp.stack([_rope_freq_row(HEAD_DIM, ROT_DIM), _rope_freq_row(IDX_DIM, IDX_ROT)]),
                 ((HEAD_DIM, ROT_DIM // 2), (IDX_DIM, IDX_ROT // 2)))
        kv_shared = None
        for i in range(depth):
            if i not in proj_w:
                proj_w[i] = [_now_bf16(sd) for sd in proj_sources(i)]
            post_src = [Side(w_o_a, i) if i < n_a else Side(w_o_b, i - n_a), Side(w_ple_gate, i)]
            post_ride = riders(post_src) if i not in post_w else []
            if i < n_a:
                plan = [(0, width, width, 0, 0, LOG2_E * sm_scale), (0, kvw, kvw, 1, 0, 1.0),
                        (0, kvw, kvw, 2, -1, 1.0), (0, width, width, 3, -1, 1.0), (0, idxw, idxw, 4, 1, 1.0),
                        (0, IDX_DIM, LANES, 4, 1, 1.0), (0, IDX_HEADS, IDX_HEADS, 5, -1, 1.0)]
                (q, k, v, gate, ix, iw), cast = _fused_project(
                    xb, proj_w[i], [True], plan, [bf16, bf16, bf16, f32, bf16, f32], ropes, post_ride)
                og = _mixer_a(q, ix, iw, gate, k, v, n_sel)
            elif kv_shared is None:
                plan = [(0, 2 * kvw, 2 * kvw, 0, -1, 1.0), (1, width, width, 1, -1, sm_scale),
                        (1, width, width, 2, -1, 1.0)]
                (kv_shared, q, gate), cast = _fused_project(xb, proj_w[i], [False, False], plan, [bf16, bf16, f32],
                                                            None, post_ride)
                og = _mixer_b(q, gate, kv_shared)
            else:
                plan = [(0, width, width, 0, -1, sm_scale), (0, width, width, 1, -1, 1.0)]
                (q, gate), cast = _fused_project(xb, proj_w[i], [False], plan, [bf16, f32], None, post_ride)
                og = _mixer_b(q, gate, kv_shared)
            if i not in post_w:
                post_w[i] = tuple(cast) if post_ride else tuple(_now_bf16(sd) for sd in post_src)
            next_ride = riders(proj_sources(i + 1)) if i + 1 < depth and i + 1 not in proj_w else []
            xb, cast = _post(og, xb, p, i, b, post_w[i][0], post_w[i][1], wp_all, lng, lnb, alpha, next_ride)
            if next_ride:
                proj_w[i + 1] = list(cast)
        outs.append(xb)
    return jnp.stack(outs, axis=0)
```

```python
import functools
from typing import NamedTuple

import jax
import jax.numpy as jnp
from jax import lax
from jax.experimental import pallas as pl
from jax.experimental.pallas import tpu as pltpu

HEAD_DIM = 128
N_KV = 4
ROT_DIM = HEAD_DIM // 4
IDX_HEADS = 16
IDX_DIM = 64
IDX_ROT = IDX_DIM // 4
TOPK_MAX = 256
ROPE_THETA = 500000.0
LN_EPS = 1e-5
LOG2_E = 1.4426950408889634

LANES = 128
BF16_SUBLANES = 16
V7X_VMEM_BYTES = 64 * 1024 * 1024
VMEM_LIMIT = V7X_VMEM_BYTES - 8 * 1024 * 1024

TQ = 256
TK2 = 2 * TQ
TM_PROJ = 256
SEG_MAX = 1024
NEG_BIG = -1e30
EXP_ZERO_ABOVE = 110.0
MAX_BISECT = 256
BISECT_STEPS_PER_CHECK = 2
COARSE_STEPS = 9


def _cparams(sem):
    return pltpu.CompilerParams(dimension_semantics=sem, vmem_limit_bytes=VMEM_LIMIT)


def _resident(shape):
    nd = len(shape)
    return pl.BlockSpec(shape, lambda *_: (0,) * nd, pipeline_mode=pl.Buffered(1))


class Seg(NamedTuple):
    w: int
    col: int
    width: int
    stored: int
    out: int
    out_col: int
    rope: int
    scale: float
    silu: bool


def _rope_slab(y, half, c, s_up, s_dn):
    up = pltpu.roll(y, LANES - half, axis=1)
    dn = pltpu.roll(y, half, axis=1)
    return y * c + up * s_up + dn * s_dn


class Side(NamedTuple):
    arr: jax.Array
    layer: int


def _side_ok(arr, n_steps):
    rows = arr.shape[-2]
    return rows % (n_steps * BF16_SUBLANES) == 0


def _side_io(sides, n_steps):
    in_specs, out_specs, out_shapes = [], [], []
    for sd in sides:
        r, c = sd.arr.shape[-2:]
        rows = r // n_steps
        if sd.layer < 0:
            in_specs.append(pl.BlockSpec((rows, c), lambda i: (i, 0)))
        else:
            in_specs.append(pl.BlockSpec((None, rows, c), lambda i, l=sd.layer: (l, i, 0)))
        out_specs.append(pl.BlockSpec((rows, c), lambda i: (i, 0)))
        out_shapes.append(jax.ShapeDtypeStruct((r, c), jnp.bfloat16))
    return in_specs, out_specs, out_shapes


def _cast_sides(side_in, side_out):
    for src, dst in zip(side_in, side_out):
        dst[...] = src[...].astype(dst.dtype)


def _rope_slab_tables(pos, freq, head_dim, half):
    ang = pos * freq
    c, sn = jnp.cos(ang), jnp.sin(ang)
    lane = lax.broadcasted_iota(jnp.int32, ang.shape, 1) & (head_dim - 1)
    s_up = jnp.where(lane < half, -sn, 0.0)
    s_dn = jnp.where(lane < half, 0.0, jnp.where(lane < 2 * half, sn, 0.0))
    return c, s_up, s_dn


def _fused_proj_kernel(x_ref, *refs, segs, rope_dims, w_transposed, n_side):
    n_w, n_rope = len(w_transposed), 2 if rope_dims else 0
    ws = refs[:n_w]
    side_in = refs[n_w + n_rope:n_w + n_rope + n_side]
    outs = refs[n_w + n_rope + n_side:len(refs) - n_side]
    _cast_sides(side_in, refs[len(refs) - n_side:])
    if rope_dims:
        pos_ref, freq_ref = refs[n_w:n_w + 2]
        tabs = [_rope_slab_tables(pos_ref[...], freq_ref[r:r + 1, :], hd, half)
                for r, (hd, half) in enumerate(rope_dims)]
    x = x_ref[...].astype(jnp.bfloat16)
    for sg in segs:
        if w_transposed[sg.w]:
            y = lax.dot_general(x, ws[sg.w][sg.col:sg.col + sg.width, :], (((1,), (1,)), ((), ())),
                                preferred_element_type=jnp.float32)
        else:
            y = jnp.dot(x, ws[sg.w][:, sg.col:sg.col + sg.width], preferred_element_type=jnp.float32)
        if sg.scale != 1.0:
            y = y * sg.scale
        if sg.silu:
            y = y * jax.nn.sigmoid(y)
        if sg.stored > sg.width:
            y = jnp.concatenate([y, jnp.zeros((y.shape[0], sg.stored - sg.width), y.dtype)], axis=1)
        if sg.rope >= 0:
            c, s_up, s_dn = tabs[sg.rope]
            slabs = [_rope_slab(y[:, j * LANES:(j + 1) * LANES], rope_dims[sg.rope][1], c, s_up, s_dn)
                     for j in range(sg.stored // LANES)]
            y = jnp.concatenate(slabs, axis=1) if len(slabs) > 1 else slabs[0]
        o_ref = outs[sg.out]
        o_ref[:, sg.out_col:sg.out_col + sg.stored] = y.astype(o_ref.dtype)


def _fused_project(x, ws, w_transposed, plan, out_dtypes, ropes, sides=()):
    s, kdim = x.shape
    segs, cols, out_cols = [], [0] * len(ws), [0] * len(out_dtypes)
    for wi, width, stored, out, rope, scale, silu in plan:
        assert stored == width or width <= SEG_MAX
        for c0 in range(0, width, SEG_MAX):
            wd = min(SEG_MAX, width - c0)
            segs.append(Seg(wi, cols[wi] + c0, wd, wd if stored == width else stored, out, out_cols[out] + c0,
                            rope, scale, silu))
        cols[wi] += width
        out_cols[out] += stored
    assert all(c == w.shape[0 if t else 1] for c, w, t in zip(cols, ws, w_transposed))
    tm = TM_PROJ
    rope_in, rope_specs = [], []
    if ropes is not None:
        pos, freqs, _ = ropes
        rope_in = [pos, freqs]
        rope_specs = [pl.BlockSpec((tm, 1), lambda i: (i, 0)), _resident(freqs.shape)]
    side_in, side_out, side_shapes = _side_io(sides, s // tm)
    res = pl.pallas_call(
        functools.partial(_fused_proj_kernel, segs=tuple(segs), rope_dims=ropes[2] if ropes is not None else (),
                          w_transposed=tuple(w_transposed), n_side=len(sides)),
        grid=(s // tm,),
        in_specs=[pl.BlockSpec((tm, kdim), lambda i: (i, 0))] + [_resident(w.shape) for w in ws]
                 + rope_specs + side_in,
        out_specs=[pl.BlockSpec((tm, n), lambda i: (i, 0)) for n in out_cols] + side_out,
        out_shape=[jax.ShapeDtypeStruct((s, n), dt) for n, dt in zip(out_cols, out_dtypes)] + side_shapes,
        compiler_params=_cparams(("parallel",)),
        name="fused_proj",
    )(x, *ws, *rope_in, *(sd.arr for sd in sides))
    return res[:len(out_dtypes)], res[len(out_dtypes):]


def _rope_freq_row(head_dim, rot_dim):
    half = rot_dim // 2
    inv = 1.0 / (ROPE_THETA ** (jnp.arange(half, dtype=jnp.float32) / half))
    row = jnp.concatenate([inv, inv, jnp.zeros((head_dim - rot_dim,), jnp.float32)])
    return jnp.tile(row, LANES // head_dim)


def _stack_heads(q_ref, n, groups):
    return jnp.concatenate(
        [q_ref[:, (n * groups + g) * HEAD_DIM:(n * groups + g + 1) * HEAD_DIM] for g in range(groups)], axis=0)


def _store_gated(o_ref, gate_ref, out, n, groups):
    tq = o_ref.shape[0]
    for g in range(groups):
        cols = slice((n * groups + g) * HEAD_DIM, (n * groups + g + 1) * HEAD_DIM)
        o_ref[:, cols] = (out[g * tq:(g + 1) * tq, :] * gate_ref[:, cols]).astype(o_ref.dtype)


COUNT_ROWS = 64


def _mixer_a_kernel(q_ref, iq_ref, iw_ref, gate_ref, ik_ref, k_ref, v_ref, o_ref,
                     score_ref, sb_ref, wt_ref, lo_ref, hi_ref, mn_ref, mx_ref,
                     qn_ref, z_ref, bias_ref, m_ref, acc_ref, *, n_sel, idx_scale):
    tq = q_ref.shape[0]
    groups = q_ref.shape[1] // HEAD_DIM // N_KV
    m_cols = groups * tq
    i = pl.program_id(0)
    nkb2 = (i + 2) // 2
    f32, bf16 = jnp.float32, jnp.bfloat16
    nt = (((1,), (1,)), ((), ()))

    key_in_blk = lax.broadcasted_iota(jnp.int32, (TK2, tq), 0)
    q_pos = lax.broadcasted_iota(jnp.int32, (TK2, tq), 1) + i * tq

    iw = iw_ref[...] * idx_scale
    wt_ref[...] = jnp.concatenate([iw, jnp.zeros((tq, LANES - IDX_HEADS), f32)], axis=1).T

    mn_ref[...] = jnp.full((tq, tq), jnp.inf, f32)
    mx_ref[...] = jnp.full((tq, tq), -jnp.inf, f32)

    def score_body(kb2, carry):
        koff = pl.multiple_of(kb2 * TK2, TK2)
        ik = ik_ref[pl.ds(koff, TK2), :IDX_DIM]
        acc = jnp.zeros((TK2, tq), f32)
        for h in range(IDX_HEADS):
            d = lax.dot_general(ik, iq_ref[:, h * IDX_DIM:(h + 1) * IDX_DIM], nt, preferred_element_type=f32)
            acc = acc + wt_ref[h:h + 1, :] * jnp.maximum(d, 0.0)
        causal = (key_in_blk + kb2 * TK2) <= q_pos
        sc = jnp.where(causal, acc, -jnp.inf)
        score_ref[pl.ds(koff, TK2), :] = sc
        sb_ref[pl.ds(koff, TK2), :] = sc.astype(bf16)
        lo_part = jnp.where(causal, acc, jnp.inf)
        mn_ref[...] = jnp.minimum(mn_ref[...], jnp.minimum(lo_part[:tq], lo_part[tq:]))
        mx_ref[...] = jnp.maximum(mx_ref[...], jnp.maximum(sc[:tq], sc[tq:]))
        return carry

    lax.fori_loop(0, nkb2, score_body, 0)

    t1 = (lax.broadcasted_iota(jnp.int32, (1, tq), 1) + i * tq + 1).astype(f32)
    k_t = jnp.minimum(t1, float(n_sel))
    row_min = jnp.min(mn_ref[...], axis=0, keepdims=True)
    row_max = jnp.max(mx_ref[...], axis=0, keepdims=True)
    if tq >= n_sel:
        lo0 = jnp.where(t1 >= float(tq), jnp.min(mx_ref[...], axis=0, keepdims=True), row_min)
    else:
        lo0 = row_min
    def to_b(x):
        return x.astype(bf16).astype(f32)

    def coarse_step(_, bracket):
        lo_b, hi_b = bracket
        mid = to_b(0.5 * lo_b + 0.5 * hi_b)
        mid16 = mid.astype(bf16)

        def count_body(kb2, acc):
            koff = pl.multiple_of(kb2 * TK2, TK2)
            for c in range(TK2 // COUNT_ROWS):
                blk = sb_ref[pl.ds(koff + c * COUNT_ROWS, COUNT_ROWS), :]
                acc = acc + jnp.where(blk >= mid16, jnp.ones((), bf16), jnp.zeros((), bf16))
            return acc

        acc = lax.fori_loop(0, nkb2, count_body, jnp.zeros((COUNT_ROWS, tq), bf16))
        ge = jnp.sum(acc.astype(f32), axis=0, keepdims=True) >= k_t
        return jnp.where(ge, mid, lo_b), jnp.where(ge, hi_b, mid)

    hi_b0 = to_b(row_max + jnp.abs(row_max) * 2.0 ** -6 + 1e-30)
    lo_b, hi_b = lax.fori_loop(0, COARSE_STEPS, coarse_step, (to_b(lo0), hi_b0))
    lo1 = jnp.maximum(lo0, lo_b - jnp.abs(lo_b) * 2.0 ** -7 - 1e-30)
    hi1 = jnp.minimum(row_max, hi_b)

    settled = t1 <= float(n_sel)
    lo_ref[...] = jnp.broadcast_to(jnp.where(settled, lo0, lo1), lo_ref.shape)
    hi_ref[...] = jnp.broadcast_to(jnp.where(settled, row_min, hi1), hi_ref.shape)

    def unresolved():
        lo, hi = lo_ref[0:1, :], hi_ref[0:1, :]
        mid = 0.5 * lo + 0.5 * hi
        open_ = jnp.where(mid > lo, 1.0, 0.0) * jnp.where(mid < hi, 1.0, 0.0)
        return (jnp.max(open_) > 0.0).astype(jnp.int32)

    def bisect_step():
        lo, hi = lo_ref[0:1, :], hi_ref[0:1, :]
        mid = 0.5 * lo + 0.5 * hi

        def count_body(kb2, acc):
            koff = pl.multiple_of(kb2 * TK2, TK2)
            for c in range(TK2 // COUNT_ROWS):
                blk = score_ref[pl.ds(koff + c * COUNT_ROWS, COUNT_ROWS), :]
                acc = acc + jnp.where(blk >= mid, 1.0, 0.0)
            return acc

        acc = lax.fori_loop(0, nkb2, count_body, jnp.zeros((COUNT_ROWS, tq), f32))
        cnt = jnp.sum(acc, axis=0, keepdims=True)
        ge = cnt >= k_t
        lo_ref[...] = jnp.broadcast_to(jnp.where(ge, mid, lo), lo_ref.shape)
        hi_ref[...] = jnp.broadcast_to(jnp.where(cnt == k_t, mid, jnp.where(ge, hi, mid)), hi_ref.shape)

    def bisect_body(state):
        for _ in range(BISECT_STEPS_PER_CHECK):
            bisect_step()
        return state[0] + BISECT_STEPS_PER_CHECK, unresolved()

    lax.while_loop(lambda st: jnp.logical_and(st[1] > 0, st[0] < MAX_BISECT), bisect_body,
                   (jnp.int32(0), unresolved()))

    thr = lo_ref[0:1, :]
    m_rows = groups * tq
    for n in range(N_KV):
        qn_ref[n] = _stack_heads(q_ref, n, groups)
    m_ref[...] = jnp.full(m_ref.shape, NEG_BIG, f32)
    acc_ref[...] = jnp.zeros(acc_ref.shape, f32)

    def logits(kb2, n):
        kblk = k_ref[pl.ds(pl.multiple_of(kb2 * TK2, TK2), TK2), n * HEAD_DIM:(n + 1) * HEAD_DIM]
        return lax.dot_general(qn_ref[n], kblk, nt, preferred_element_type=f32).astype(bf16)

    def mask_bias(kb2):
        blk = score_ref[pl.ds(pl.multiple_of(kb2 * TK2, TK2), TK2), :]
        return jnp.where(blk >= thr, 0.0, NEG_BIG).astype(bf16).T

    for n in range(N_KV):
        z_ref[n] = logits(0, n)
    bias_ref[...] = mask_bias(0)

    def attend(kb2, prefetch):
        koff = pl.multiple_of(kb2 * TK2, TK2)
        bias = jnp.concatenate([bias_ref[...]] * groups, axis=0)
        if prefetch:
            bias_ref[...] = mask_bias(kb2 + 1)
        for n in range(N_KV):
            vblk = v_ref[pl.ds(koff, TK2), n * HEAD_DIM:(n + 1) * HEAD_DIM]
            vext = jnp.concatenate([vblk, jnp.ones_like(vblk)], axis=1)
            z = z_ref[n] + bias
            if prefetch:
                z_ref[n] = logits(kb2 + 1, n)
            m_old = m_ref[n]
            m_new = jnp.maximum(m_old, jnp.max(z, axis=1, keepdims=True).astype(f32))
            p = jnp.exp2(z - jnp.concatenate([m_new.astype(bf16)] * (TK2 // LANES), axis=1))
            alpha = jnp.exp2(m_old - m_new)
            acc_ref[n] = (jnp.concatenate([alpha, alpha], axis=1) * acc_ref[n]
                          + jnp.dot(p, vext, preferred_element_type=f32))
            m_ref[n] = m_new

    def attn_body(kb2, carry):
        attend(kb2, True)
        return carry

    lax.fori_loop(0, nkb2 - 1, attn_body, 0)
    attend(nkb2 - 1, False)
    for n in range(N_KV):
        acc = acc_ref[n]
        _store_gated(o_ref, gate_ref, acc[:, :HEAD_DIM] / acc[:, HEAD_DIM:], n, groups)


def _mixer_a(q, ix, iw, gate, k, v, n_sel):
    s, width = q.shape
    groups = width // HEAD_DIM // N_KV
    f32 = jnp.float32
    idxw = IDX_HEADS * IDX_DIM
    ik_spec = pl.BlockSpec((s, LANES), lambda i: (0, idxw // LANES), pipeline_mode=pl.Buffered(1))
    kern = functools.partial(_mixer_a_kernel, n_sel=n_sel, idx_scale=(IDX_DIM ** -0.5) * (IDX_HEADS ** -0.5))
    return pl.pallas_call(
        kern,
        grid=(s // TQ,),
        in_specs=[pl.BlockSpec((TQ, width), lambda i: (i, 0)),
                  pl.BlockSpec((TQ, IDX_HEADS * IDX_DIM), lambda i: (i, 0)),
                  pl.BlockSpec((TQ, IDX_HEADS), lambda i: (i, 0)),
                  pl.BlockSpec((TQ, width), lambda i: (i, 0)),
                  ik_spec, _resident(k.shape), _resident(v.shape)],
        out_specs=pl.BlockSpec((TQ, width), lambda i: (i, 0)),
        out_shape=jax.ShapeDtypeStruct((s, width), jnp.bfloat16),
        scratch_shapes=[pltpu.VMEM((s, TQ), f32),
                        pltpu.VMEM((s, TQ), jnp.bfloat16),
                        pltpu.VMEM((LANES, TQ), f32),
                        pltpu.VMEM((8, TQ), f32),
                        pltpu.VMEM((8, TQ), f32),
                        pltpu.VMEM((TQ, TQ), f32),
                        pltpu.VMEM((TQ, TQ), f32),
                        pltpu.VMEM((N_KV, groups * TQ, HEAD_DIM), jnp.bfloat16),
                        pltpu.VMEM((N_KV, groups * TQ, TK2), jnp.bfloat16),
                        pltpu.VMEM((TQ, TK2), jnp.bfloat16),
                        pltpu.VMEM((N_KV, groups * TQ, LANES), f32),
                        pltpu.VMEM((N_KV, groups * TQ, 2 * HEAD_DIM), f32)],
        compiler_params=_cparams(("parallel",)),
        name="mixer_a",
    )(q, ix, iw, gate, ix, k, v)


def _mixer_b_kernel(q_ref, gate_ref, k_ref, v_ref, o_ref, qn_ref, c_ref, acc_ref):
    tq = q_ref.shape[0]
    tk = tq
    groups = q_ref.shape[1] // HEAD_DIM // N_KV
    m_rows = groups * tq
    i = pl.program_id(0)
    f32, bf16 = jnp.float32, jnp.bfloat16

    row = lax.broadcasted_iota(jnp.int32, (tq, tk), 0)
    col = lax.broadcasted_iota(jnp.int32, (tq, tk), 1)
    later = jnp.where(row > col, 1.0, 0.0).astype(bf16)
    later2 = jnp.concatenate([later, later], axis=0)
    strict = jnp.concatenate([col < row] * groups, axis=0)

    for n in range(N_KV):
        qn_ref[n] = _stack_heads(q_ref, n, groups)

    def step(kb, diagonal):
        koff = pl.multiple_of(kb * tk, tk)
        c_min = None
        for n in range(N_KV):
            kblk = k_ref[pl.ds(koff, tk), n * HEAD_DIM:(n + 1) * HEAD_DIM]
            vblk = v_ref[pl.ds(koff, tk), n * HEAD_DIM:(n + 1) * HEAD_DIM]
            z = lax.dot_general(qn_ref[n], kblk, (((1,), (1,)), ((), ())), preferred_element_type=f32)
            t = jnp.log(1.0 + jnp.exp2(jnp.abs(z) * (-LOG2_E)))
            sp = jnp.maximum(z, 0.0) + t
            if diagonal:
                sp = jnp.where(strict, sp, 0.0)
            sp_hi = sp.astype(bf16)
            sp_lo = (sp - sp_hi.astype(f32)).astype(bf16)
            suffix = jnp.dot(jnp.concatenate([sp_hi, sp_lo], axis=1), later2, preferred_element_type=f32)
            c_old = c_ref[n]
            total = suffix + jnp.concatenate([c_old] * (tk // LANES), axis=1)
            a = jnp.exp((jnp.minimum(z, 0.0) - t) - total)
            if diagonal:
                a = jnp.where(strict, a, 0.0)
            acc_ref[n] += jnp.dot(a.astype(bf16), vblk, preferred_element_type=f32)
            c_new = c_old + jnp.broadcast_to(jnp.sum(sp, axis=1, keepdims=True), (m_rows, LANES))
            c_ref[n] = c_new
            c_min = c_new if c_min is None else jnp.minimum(c_min, c_new)
        return (jnp.min(c_min) <= EXP_ZERO_ABOVE).astype(jnp.int32)

    c_ref[...] = jnp.zeros((N_KV, m_rows, LANES), f32)
    acc_ref[...] = jnp.zeros((N_KV, m_rows, HEAD_DIM), f32)
    go = step(i, True)
    lax.while_loop(lambda st: jnp.logical_and(st[0] >= 0, st[1] > 0),
                   lambda st: (st[0] - 1, step(st[0], False)), (i - 1, go))
    for n in range(N_KV):
        _store_gated(o_ref, gate_ref, acc_ref[n], n, groups)


def _mixer_b(q, gate, kv):
    s, width = q.shape
    groups = width // HEAD_DIM // N_KV
    kvw = kv.shape[1] // 2
    half_spec = lambda c: pl.BlockSpec((s, kvw), lambda i: (0, c), pipeline_mode=pl.Buffered(1))
    return pl.pallas_call(
        _mixer_b_kernel,
        grid=(s // TQ,),
        in_specs=[pl.BlockSpec((TQ, width), lambda i: (i, 0)),
                  pl.BlockSpec((TQ, width), lambda i: (i, 0)),
                  half_spec(0), half_spec(1)],
        out_specs=pl.BlockSpec((TQ, width), lambda i: (i, 0)),
        out_shape=jax.ShapeDtypeStruct((s, width), jnp.bfloat16),
        scratch_shapes=[pltpu.VMEM((N_KV, groups * TQ, HEAD_DIM), jnp.bfloat16),
                        pltpu.VMEM((N_KV, groups * TQ, LANES), jnp.float32),
                        pltpu.VMEM((N_KV, groups * TQ, HEAD_DIM), jnp.float32)],
        compiler_params=_cparams(("parallel",)),
        name="mixer_b",
    )(q, gate, kv, kv)


def _post_kernel(og_ref, x_ref, p_ref, wo_ref, wg_ref, wp_ref, lng_ref, lnb_ref, *refs, alpha, n_side):
    f32 = jnp.float32
    o_ref = refs[n_side]
    _cast_sides(refs[:n_side], refs[n_side + 1:])
    h = jnp.dot(og_ref[...], wo_ref[...], preferred_element_type=f32)
    y = alpha * x_ref[...] + h
    mu = jnp.mean(y, axis=-1, keepdims=True)
    d = y - mu
    var = jnp.mean(d * d, axis=-1, keepdims=True)
    yn = d * lax.rsqrt(var + LN_EPS) * lng_ref[...] + lnb_ref[...]
    gl = jnp.dot(yn.astype(jnp.bfloat16), wg_ref[...], preferred_element_type=f32)
    pe = jnp.dot(p_ref[...].astype(jnp.bfloat16), wp_ref[...], preferred_element_type=f32)
    o_ref[...] = yn + pe * jax.nn.sigmoid(gl)


def _resident_slab(arr, idx):
    return pl.BlockSpec((None,) + arr.shape[1:], lambda *_: (idx,) + (0,) * (arr.ndim - 1),
                        pipeline_mode=pl.Buffered(1))


def _post(og, x, p, layer, batch, wo, wg, wp, lng, lnb, alpha, sides=()):
    s, d = x.shape
    tm = TM_PROJ
    side_in, side_out, side_shapes = _side_io(sides, s // tm)
    res = pl.pallas_call(
        functools.partial(_post_kernel, alpha=alpha, n_side=len(sides)),
        grid=(s // tm,),
        in_specs=[pl.BlockSpec((tm, og.shape[1]), lambda i: (i, 0)),
                  pl.BlockSpec((tm, d), lambda i: (i, 0)),
                  pl.BlockSpec((None, None, tm, p.shape[3]), lambda i: (layer, batch, i, 0)),
                  _resident(wo.shape), _resident(wg.shape), _resident_slab(wp, layer),
                  _resident_slab(lng, layer), _resident_slab(lnb, layer)] + side_in,
        out_specs=[pl.BlockSpec((tm, d), lambda i: (i, 0))] + side_out,
        out_shape=[jax.ShapeDtypeStruct((s, d), jnp.float32)] + side_shapes,
        compiler_params=_cparams(("parallel",)),
        name="post",
    )(og, x, p, wo, wg, wp, lng, lnb, *(sd.arr for sd in sides))
    return res[0], res[1:]


def _now_bf16(sd):
    return (sd.arr if sd.layer < 0 else sd.arr[sd.layer]).astype(jnp.bfloat16)


def kernel(x, p, positions, w_in_a, w_o_a, w_kv_b, w_in_b, w_o_b, ln_g, ln_b, w_ple, w_ple_gate):
    bsz, s, d = x.shape
    n_a, n_b = w_in_a.shape[0], w_in_b.shape[0]
    depth = n_a + n_b
    alpha = (2.0 * depth) ** 0.25
    width = w_o_a.shape[1]
    kvw = N_KV * HEAD_DIM
    idxw = IDX_HEADS * IDX_DIM
    n_sel = min(TOPK_MAX, s // 4)
    bf16, f32 = jnp.bfloat16, jnp.float32
    sm_scale = HEAD_DIM ** -0.5
    assert s % TK2 == 0 and s // COUNT_ROWS <= 256
    n_steps = s // TM_PROJ
    wp_all = w_ple.astype(bf16)
    lng, lnb = ln_g[:, None, :], ln_b[:, None, :]

    def proj_sources(layer):
        if layer < n_a:
            return [Side(jnp.swapaxes(w_in_a[layer], 0, 1), -1)]
        first_b = [Side(w_kv_b, -1)] if layer == n_a else []
        return first_b + [Side(w_in_b, layer - n_a)]

    def riders(srcs):
        return srcs if all(_side_ok(sd.arr, n_steps) for sd in srcs) else []

    proj_w, post_w = {}, {}
    outs = []
    for b in range(bsz):
        xb = x[b]
        ropes = (positions[b].astype(f32)[:, None],
                 jnp.stack([_rope_freq_row(HEAD_DIM, ROT_DIM), _rope_freq_row(IDX_DIM, IDX_ROT)]),
                 ((HEAD_DIM, ROT_DIM // 2), (IDX_DIM, IDX_ROT // 2)))
        kv_shared = None
        for i in range(depth):
            if i not in proj_w:
                proj_w[i] = [_now_bf16(sd) for sd in proj_sources(i)]
            post_src = [Side(w_o_a, i) if i < n_a else Side(w_o_b, i - n_a), Side(w_ple_gate, i)]
            post_ride = riders(post_src) if i not in post_w else []
            if i < n_a:
                plan = [(0, width, width, 0, 0, LOG2_E * sm_scale, False), (0, kvw, kvw, 1, 0, 1.0, False),
                        (0, kvw, kvw, 2, -1, 1.0, False), (0, width, width, 3, -1, 1.0, True),
                        (0, idxw, idxw, 4, 1, 1.0, False), (0, IDX_DIM, LANES, 4, 1, 1.0, False),
                        (0, IDX_HEADS, IDX_HEADS, 5, -1, 1.0, False)]
                (q, k, v, gate, ix, iw), cast = _fused_project(
                    xb, proj_w[i], [True], plan, [bf16, bf16, bf16, f32, bf16, f32], ropes, post_ride)
                og = _mixer_a(q, ix, iw, gate, k, v, n_sel)
            elif kv_shared is None:
                plan = [(0, 2 * kvw, 2 * kvw, 0, -1, 1.0, False), (1, width, width, 1, -1, sm_scale, False),
                        (1, width, width, 2, -1, 1.0, True)]
                (kv_shared, q, gate), cast = _fused_project(xb, proj_w[i], [False, False], plan, [bf16, bf16, f32],
                                                            None, post_ride)
                og = _mixer_b(q, gate, kv_shared)
            else:
                plan = [(0, width, width, 0, -1, sm_scale, False), (0, width, width, 1, -1, 1.0, True)]
                (q, gate), cast = _fused_project(xb, proj_w[i], [False], plan, [bf16, f32], None, post_ride)
                og = _mixer_b(q, gate, kv_shared)
            if i not in post_w:
                post_w[i] = tuple(cast) if post_ride else tuple(_now_bf16(sd) for sd in post_src)
            next_ride = riders(proj_sources(i + 1)) if i + 1 < depth and i + 1 not in proj_w else []
            xb, cast = _post(og, xb, p, i, b, post_w[i][0], post_w[i][1], wp_all, lng, lnb, alpha, next_ride)
            if next_ride:
                proj_w[i + 1] = list(cast)
        outs.append(xb)
    return jnp.stack(outs, axis=0)
```

```python
import functools
from typing import NamedTuple

import jax
import jax.numpy as jnp
from jax import lax
from jax.experimental import pallas as pl
from jax.experimental.pallas import tpu as pltpu

HEAD_DIM = 128
N_KV = 4
ROT_DIM = HEAD_DIM // 4
IDX_HEADS = 16
IDX_DIM = 64
IDX_ROT = IDX_DIM // 4
TOPK_MAX = 256
ROPE_THETA = 500000.0
LN_EPS = 1e-5
LOG2_E = 1.4426950408889634

LANES = 128
BF16_SUBLANES = 16
V7X_VMEM_BYTES = 64 * 1024 * 1024
VMEM_LIMIT = V7X_VMEM_BYTES - 8 * 1024 * 1024

TQ = 256
TK2 = 2 * TQ
TM_PROJ = 256
SEG_MAX = 1024
NEG_BIG = -1e30
EXP_ZERO_ABOVE = 110.0
MAX_BISECT = 256
BISECT_STEPS_PER_CHECK = 2
UNCHECKED_STEPS = 4
COARSE_STEPS = 9


def _cparams(sem):
    return pltpu.CompilerParams(dimension_semantics=sem, vmem_limit_bytes=VMEM_LIMIT)


def _resident(shape):
    nd = len(shape)
    return pl.BlockSpec(shape, lambda *_: (0,) * nd, pipeline_mode=pl.Buffered(1))


class Seg(NamedTuple):
    w: int
    col: int
    width: int
    stored: int
    out: int
    out_col: int
    rope: int
    scale: float
    silu: bool


def _rope_slab(y, half, c, s_up, s_dn):
    up = pltpu.roll(y, LANES - half, axis=1)
    dn = pltpu.roll(y, half, axis=1)
    return y * c + up * s_up + dn * s_dn


class Side(NamedTuple):
    arr: jax.Array
    layer: int


def _side_ok(arr, n_steps):
    rows = arr.shape[-2]
    return rows % (n_steps * BF16_SUBLANES) == 0


def _side_io(sides, n_steps):
    in_specs, out_specs, out_shapes = [], [], []
    for sd in sides:
        r, c = sd.arr.shape[-2:]
        rows = r // n_steps
        if sd.layer < 0:
            in_specs.append(pl.BlockSpec((rows, c), lambda i: (i, 0)))
        else:
            in_specs.append(pl.BlockSpec((None, rows, c), lambda i, l=sd.layer: (l, i, 0)))
        out_specs.append(pl.BlockSpec((rows, c), lambda i: (i, 0)))
        out_shapes.append(jax.ShapeDtypeStruct((r, c), jnp.bfloat16))
    return in_specs, out_specs, out_shapes


def _cast_sides(side_in, side_out):
    for src, dst in zip(side_in, side_out):
        dst[...] = src[...].astype(dst.dtype)


def _rope_slab_tables(pos, freq, head_dim, half):
    ang = pos * freq
    c, sn = jnp.cos(ang), jnp.sin(ang)
    lane = lax.broadcasted_iota(jnp.int32, ang.shape, 1) & (head_dim - 1)
    s_up = jnp.where(lane < half, -sn, 0.0)
    s_dn = jnp.where(lane < half, 0.0, jnp.where(lane < 2 * half, sn, 0.0))
    return c, s_up, s_dn


def _fused_proj_kernel(x_ref, *refs, segs, rope_dims, w_transposed, n_side):
    n_w, n_rope = len(w_transposed), 2 if rope_dims else 0
    ws = refs[:n_w]
    side_in = refs[n_w + n_rope:n_w + n_rope + n_side]
    outs = refs[n_w + n_rope + n_side:len(refs) - n_side]
    _cast_sides(side_in, refs[len(refs) - n_side:])
    if rope_dims:
        pos_ref, freq_ref = refs[n_w:n_w + 2]
        tabs = [_rope_slab_tables(pos_ref[...], freq_ref[r:r + 1, :], hd, half)
                for r, (hd, half) in enumerate(rope_dims)]
    x = x_ref[...].astype(jnp.bfloat16)
    for sg in segs:
        if w_transposed[sg.w]:
            y = lax.dot_general(x, ws[sg.w][sg.col:sg.col + sg.width, :], (((1,), (1,)), ((), ())),
                                preferred_element_type=jnp.float32)
        else:
            y = jnp.dot(x, ws[sg.w][:, sg.col:sg.col + sg.width], preferred_element_type=jnp.float32)
        if sg.scale != 1.0:
            y = y * sg.scale
        if sg.silu:
            y = y * jax.nn.sigmoid(y)
        if sg.stored > sg.width:
            y = jnp.concatenate([y, jnp.zeros((y.shape[0], sg.stored - sg.width), y.dtype)], axis=1)
        if sg.rope >= 0:
            c, s_up, s_dn = tabs[sg.rope]
            slabs = [_rope_slab(y[:, j * LANES:(j + 1) * LANES], rope_dims[sg.rope][1], c, s_up, s_dn)
                     for j in range(sg.stored // LANES)]
            y = jnp.concatenate(slabs, axis=1) if len(slabs) > 1 else slabs[0]
        o_ref = outs[sg.out]
        o_ref[:, sg.out_col:sg.out_col + sg.stored] = y.astype(o_ref.dtype)


def _fused_project(x, ws, w_transposed, plan, out_dtypes, ropes, sides=()):
    s, kdim = x.shape
    segs, cols, out_cols = [], [0] * len(ws), [0] * len(out_dtypes)
    for wi, width, stored, out, rope, scale, silu in plan:
        assert stored == width or width <= SEG_MAX
        for c0 in range(0, width, SEG_MAX):
            wd = min(SEG_MAX, width - c0)
            segs.append(Seg(wi, cols[wi] + c0, wd, wd if stored == width else stored, out, out_cols[out] + c0,
                            rope, scale, silu))
        cols[wi] += width
        out_cols[out] += stored
    assert all(c == w.shape[0 if t else 1] for c, w, t in zip(cols, ws, w_transposed))
    tm = TM_PROJ
    rope_in, rope_specs = [], []
    if ropes is not None:
        pos, freqs, _ = ropes
        rope_in = [pos, freqs]
        rope_specs = [pl.BlockSpec((tm, 1), lambda i: (i, 0)), _resident(freqs.shape)]
    side_in, side_out, side_shapes = _side_io(sides, s // tm)
    res = pl.pallas_call(
        functools.partial(_fused_proj_kernel, segs=tuple(segs), rope_dims=ropes[2] if ropes is not None else (),
                          w_transposed=tuple(w_transposed), n_side=len(sides)),
        grid=(s // tm,),
        in_specs=[pl.BlockSpec((tm, kdim), lambda i: (i, 0))] + [_resident(w.shape) for w in ws]
                 + rope_specs + side_in,
        out_specs=[pl.BlockSpec((tm, n), lambda i: (i, 0)) for n in out_cols] + side_out,
        out_shape=[jax.ShapeDtypeStruct((s, n), dt) for n, dt in zip(out_cols, out_dtypes)] + side_shapes,
        compiler_params=_cparams(("parallel",)),
        name="fused_proj",
    )(x, *ws, *rope_in, *(sd.arr for sd in sides))
    return res[:len(out_dtypes)], res[len(out_dtypes):]


def _rope_freq_row(head_dim, rot_dim):
    half = rot_dim // 2
    inv = 1.0 / (ROPE_THETA ** (jnp.arange(half, dtype=jnp.float32) / half))
    row = jnp.concatenate([inv, inv, jnp.zeros((head_dim - rot_dim,), jnp.float32)])
    return jnp.tile(row, LANES // head_dim)


def _stack_heads(q_ref, n, groups):
    return jnp.concatenate(
        [q_ref[:, (n * groups + g) * HEAD_DIM:(n * groups + g + 1) * HEAD_DIM] for g in range(groups)], axis=0)


def _store_gated(o_ref, gate_ref, out, n, groups):
    tq = o_ref.shape[0]
    for g in range(groups):
        cols = slice((n * groups + g) * HEAD_DIM, (n * groups + g + 1) * HEAD_DIM)
        o_ref[:, cols] = (out[g * tq:(g + 1) * tq, :] * gate_ref[:, cols]).astype(o_ref.dtype)


COUNT_ROWS = 64


def _mixer_a_kernel(q_ref, iq_ref, iw_ref, gate_ref, ik_ref, k_ref, v_ref, o_ref,
                     score_ref, sb_ref, wt_ref, lo_ref, hi_ref, mn_ref, mx_ref,
                     qn_ref, z_ref, bias_ref, m_ref, acc_ref, *, n_sel, idx_scale):
    tq = q_ref.shape[0]
    groups = q_ref.shape[1] // HEAD_DIM // N_KV
    m_cols = groups * tq
    i = pl.program_id(0)
    nkb2 = (i + 2) // 2
    f32, bf16 = jnp.float32, jnp.bfloat16
    nt = (((1,), (1,)), ((), ()))

    key_in_blk = lax.broadcasted_iota(jnp.int32, (TK2, tq), 0)
    q_pos = lax.broadcasted_iota(jnp.int32, (TK2, tq), 1) + i * tq

    iw = iw_ref[...] * idx_scale
    wt_ref[...] = jnp.concatenate([iw, jnp.zeros((tq, LANES - IDX_HEADS), f32)], axis=1).T

    mn_ref[...] = jnp.full((tq, tq), jnp.inf, f32)
    mx_ref[...] = jnp.full((tq, tq), -jnp.inf, f32)

    def score_body(kb2, carry):
        koff = pl.multiple_of(kb2 * TK2, TK2)
        ik = ik_ref[pl.ds(koff, TK2), :IDX_DIM]
        acc = jnp.zeros((TK2, tq), f32)
        for h in range(IDX_HEADS):
            d = lax.dot_general(ik, iq_ref[:, h * IDX_DIM:(h + 1) * IDX_DIM], nt, preferred_element_type=f32)
            acc = acc + wt_ref[h:h + 1, :] * jnp.maximum(d, 0.0)
        causal = (key_in_blk + kb2 * TK2) <= q_pos
        sc = jnp.where(causal, acc, -jnp.inf)
        score_ref[pl.ds(koff, TK2), :] = sc
        sb_ref[pl.ds(koff, TK2), :] = sc.astype(bf16)
        lo_part = jnp.where(causal, acc, jnp.inf)
        mn_ref[...] = jnp.minimum(mn_ref[...], jnp.minimum(lo_part[:tq], lo_part[tq:]))
        mx_ref[...] = jnp.maximum(mx_ref[...], jnp.maximum(sc[:tq], sc[tq:]))
        return carry

    lax.fori_loop(0, nkb2, score_body, 0)

    t1 = (lax.broadcasted_iota(jnp.int32, (1, tq), 1) + i * tq + 1).astype(f32)
    k_t = jnp.minimum(t1, float(n_sel))
    row_min = jnp.min(mn_ref[...], axis=0, keepdims=True)
    row_max = jnp.max(mx_ref[...], axis=0, keepdims=True)
    if tq >= n_sel:
        lo0 = jnp.where(t1 >= float(tq), jnp.min(mx_ref[...], axis=0, keepdims=True), row_min)
    else:
        lo0 = row_min
    def to_b(x):
        return x.astype(bf16).astype(f32)

    def coarse_step(_, bracket):
        lo_b, hi_b = bracket
        mid = to_b(0.5 * lo_b + 0.5 * hi_b)
        mid16 = mid.astype(bf16)

        def count_body(kb2, acc):
            koff = pl.multiple_of(kb2 * TK2, TK2)
            for c in range(TK2 // COUNT_ROWS):
                blk = sb_ref[pl.ds(koff + c * COUNT_ROWS, COUNT_ROWS), :]
                acc = acc + jnp.where(blk >= mid16, jnp.ones((), bf16), jnp.zeros((), bf16))
            return acc

        acc = lax.fori_loop(0, nkb2, count_body, jnp.zeros((COUNT_ROWS, tq), bf16))
        ge = jnp.sum(acc.astype(f32), axis=0, keepdims=True) >= k_t
        return jnp.where(ge, mid, lo_b), jnp.where(ge, hi_b, mid)

    hi_b0 = to_b(row_max + jnp.abs(row_max) * 2.0 ** -6 + 1e-30)
    lo_b, hi_b = lax.fori_loop(0, COARSE_STEPS, coarse_step, (to_b(lo0), hi_b0))
    lo1 = jnp.maximum(lo0, lo_b - jnp.abs(lo_b) * 2.0 ** -7 - 1e-30)
    hi1 = jnp.minimum(row_max, hi_b)

    settled = t1 <= float(n_sel)
    lo_ref[...] = jnp.broadcast_to(jnp.where(settled, lo0, lo1), lo_ref.shape)
    hi_ref[...] = jnp.broadcast_to(jnp.where(settled, row_min, hi1), hi_ref.shape)

    def unresolved():
        lo, hi = lo_ref[0:1, :], hi_ref[0:1, :]
        mid = 0.5 * lo + 0.5 * hi
        open_ = jnp.where(mid > lo, 1.0, 0.0) * jnp.where(mid < hi, 1.0, 0.0)
        return (jnp.max(open_) > 0.0).astype(jnp.int32)

    def bisect_step():
        lo, hi = lo_ref[0:1, :], hi_ref[0:1, :]
        mid = 0.5 * lo + 0.5 * hi

        def count_body(kb2, acc):
            koff = pl.multiple_of(kb2 * TK2, TK2)
            for c in range(TK2 // COUNT_ROWS):
                blk = score_ref[pl.ds(koff + c * COUNT_ROWS, COUNT_ROWS), :]
                acc = acc + jnp.where(blk >= mid, 1.0, 0.0)
            return acc

        acc = lax.fori_loop(0, nkb2, count_body, jnp.zeros((COUNT_ROWS, tq), f32))
        cnt = jnp.sum(acc, axis=0, keepdims=True)
        ge = cnt >= k_t
        lo_ref[...] = jnp.broadcast_to(jnp.where(ge, mid, lo), lo_ref.shape)
        hi_ref[...] = jnp.broadcast_to(jnp.where(cnt == k_t, mid, jnp.where(ge, hi, mid)), hi_ref.shape)

    def bisect_body(state):
        for _ in range(BISECT_STEPS_PER_CHECK):
            bisect_step()
        return state[0] + BISECT_STEPS_PER_CHECK, unresolved()

    lax.fori_loop(0, UNCHECKED_STEPS, lambda _, c: (bisect_step(), c)[1], 0)
    lax.while_loop(lambda st: jnp.logical_and(st[1] > 0, st[0] < MAX_BISECT), bisect_body,
                   (jnp.int32(0), unresolved()))

    thr = lo_ref[0:1, :]
    m_rows = groups * tq
    for n in range(N_KV):
        qn_ref[n] = _stack_heads(q_ref, n, groups)
    m_ref[...] = jnp.full(m_ref.shape, NEG_BIG, f32)
    acc_ref[...] = jnp.zeros(acc_ref.shape, f32)

    def logits(kb2, n):
        kblk = k_ref[pl.ds(pl.multiple_of(kb2 * TK2, TK2), TK2), n * HEAD_DIM:(n + 1) * HEAD_DIM]
        return lax.dot_general(qn_ref[n], kblk, nt, preferred_element_type=f32).astype(bf16)

    def mask_bias(kb2):
        blk = score_ref[pl.ds(pl.multiple_of(kb2 * TK2, TK2), TK2), :]
        return jnp.where(blk >= thr, 0.0, NEG_BIG).astype(bf16).T

    for n in range(N_KV):
        z_ref[n] = logits(0, n)
    bias_ref[...] = mask_bias(0)

    def attend(kb2, prefetch):
        koff = pl.multiple_of(kb2 * TK2, TK2)
        bias = jnp.concatenate([bias_ref[...]] * groups, axis=0)
        if prefetch:
            bias_ref[...] = mask_bias(kb2 + 1)
        for n in range(N_KV):
            vblk = v_ref[pl.ds(koff, TK2), n * HEAD_DIM:(n + 1) * HEAD_DIM]
            vext = jnp.concatenate([vblk, jnp.ones_like(vblk)], axis=1)
            z = z_ref[n] + bias
            if prefetch:
                z_ref[n] = logits(kb2 + 1, n)
            m_old = m_ref[n]
            m_new = jnp.maximum(m_old, jnp.max(z, axis=1, keepdims=True).astype(f32))
            p = jnp.exp2(z - jnp.concatenate([m_new.astype(bf16)] * (TK2 // LANES), axis=1))
            alpha = jnp.exp2(m_old - m_new)
            acc_ref[n] = (jnp.concatenate([alpha, alpha], axis=1) * acc_ref[n]
                          + jnp.dot(p, vext, preferred_element_type=f32))
            m_ref[n] = m_new

    def attn_body(kb2, carry):
        attend(kb2, True)
        return carry

    lax.fori_loop(0, nkb2 - 1, attn_body, 0)
    attend(nkb2 - 1, False)
    for n in range(N_KV):
        acc = acc_ref[n]
        _store_gated(o_ref, gate_ref, acc[:, :HEAD_DIM] / acc[:, HEAD_DIM:], n, groups)


def _mixer_a(q, ix, iw, gate, k, v, n_sel):
    s, width = q.shape
    groups = width // HEAD_DIM // N_KV
    f32 = jnp.float32
    idxw = IDX_HEADS * IDX_DIM
    ik_spec = pl.BlockSpec((s, LANES), lambda i: (0, idxw // LANES), pipeline_mode=pl.Buffered(1))
    kern = functools.partial(_mixer_a_kernel, n_sel=n_sel, idx_scale=(IDX_DIM ** -0.5) * (IDX_HEADS ** -0.5))
    return pl.pallas_call(
        kern,
        grid=(s // TQ,),
        in_specs=[pl.BlockSpec((TQ, width), lambda i: (i, 0)),
                  pl.BlockSpec((TQ, IDX_HEADS * IDX_DIM), lambda i: (i, 0)),
                  pl.BlockSpec((TQ, IDX_HEADS), lambda i: (i, 0)),
                  pl.BlockSpec((TQ, width), lambda i: (i, 0)),
                  ik_spec, _resident(k.shape), _resident(v.shape)],
        out_specs=pl.BlockSpec((TQ, width), lambda i: (i, 0)),
        out_shape=jax.ShapeDtypeStruct((s, width), jnp.bfloat16),
        scratch_shapes=[pltpu.VMEM((s, TQ), f32),
                        pltpu.VMEM((s, TQ), jnp.bfloat16),
                        pltpu.VMEM((LANES, TQ), f32),
                        pltpu.VMEM((8, TQ), f32),
                        pltpu.VMEM((8, TQ), f32),
                        pltpu.VMEM((TQ, TQ), f32),
                        pltpu.VMEM((TQ, TQ), f32),
                        pltpu.VMEM((N_KV, groups * TQ, HEAD_DIM), jnp.bfloat16),
                        pltpu.VMEM((N_KV, groups * TQ, TK2), jnp.bfloat16),
                        pltpu.VMEM((TQ, TK2), jnp.bfloat16),
                        pltpu.VMEM((N_KV, groups * TQ, LANES), f32),
                        pltpu.VMEM((N_KV, groups * TQ, 2 * HEAD_DIM), f32)],
        compiler_params=_cparams(("parallel",)),
        name="mixer_a",
    )(q, ix, iw, gate, ix, k, v)


def _mixer_b_kernel(q_ref, gate_ref, k_ref, v_ref, o_ref, qn_ref, c_ref, acc_ref):
    tq = q_ref.shape[0]
    tk = tq
    groups = q_ref.shape[1] // HEAD_DIM // N_KV
    m_rows = groups * tq
    i = pl.program_id(0)
    f32, bf16 = jnp.float32, jnp.bfloat16

    row = lax.broadcasted_iota(jnp.int32, (tq, tk), 0)
    col = lax.broadcasted_iota(jnp.int32, (tq, tk), 1)
    later = jnp.where(row > col, 1.0, 0.0).astype(bf16)
    later2 = jnp.concatenate([later, later], axis=0)
    strict = jnp.concatenate([col < row] * groups, axis=0)

    for n in range(N_KV):
        qn_ref[n] = _stack_heads(q_ref, n, groups)

    def step(kb, diagonal):
        koff = pl.multiple_of(kb * tk, tk)
        c_min = None
        for n in range(N_KV):
            kblk = k_ref[pl.ds(koff, tk), n * HEAD_DIM:(n + 1) * HEAD_DIM]
            vblk = v_ref[pl.ds(koff, tk), n * HEAD_DIM:(n + 1) * HEAD_DIM]
            z = lax.dot_general(qn_ref[n], kblk, (((1,), (1,)), ((), ())), preferred_element_type=f32)
            t = jnp.log(1.0 + jnp.exp2(jnp.abs(z) * (-LOG2_E)))
            sp = jnp.maximum(z, 0.0) + t
            if diagonal:
                sp = jnp.where(strict, sp, 0.0)
            sp_hi = sp.astype(bf16)
            sp_lo = (sp - sp_hi.astype(f32)).astype(bf16)
            suffix = jnp.dot(jnp.concatenate([sp_hi, sp_lo], axis=1), later2, preferred_element_type=f32)
            c_old = c_ref[n]
            total = suffix + jnp.concatenate([c_old] * (tk // LANES), axis=1)
            a = jnp.exp((jnp.minimum(z, 0.0) - t) - total)
            if diagonal:
                a = jnp.where(strict, a, 0.0)
            acc_ref[n] += jnp.dot(a.astype(bf16), vblk, preferred_element_type=f32)
            c_new = c_old + jnp.broadcast_to(jnp.sum(sp, axis=1, keepdims=True), (m_rows, LANES))
            c_ref[n] = c_new
            c_min = c_new if c_min is None else jnp.minimum(c_min, c_new)
        return (jnp.min(c_min) <= EXP_ZERO_ABOVE).astype(jnp.int32)

    c_ref[...] = jnp.zeros((N_KV, m_rows, LANES), f32)
    acc_ref[...] = jnp.zeros((N_KV, m_rows, HEAD_DIM), f32)
    go = step(i, True)
    lax.while_loop(lambda st: jnp.logical_and(st[0] >= 0, st[1] > 0),
                   lambda st: (st[0] - 1, step(st[0], False)), (i - 1, go))
    for n in range(N_KV):
        _store_gated(o_ref, gate_ref, acc_ref[n], n, groups)


def _mixer_b(q, gate, kv):
    s, width = q.shape
    groups = width // HEAD_DIM // N_KV
    kvw = kv.shape[1] // 2
    half_spec = lambda c: pl.BlockSpec((s, kvw), lambda i: (0, c), pipeline_mode=pl.Buffered(1))
    return pl.pallas_call(
        _mixer_b_kernel,
        grid=(s // TQ,),
        in_specs=[pl.BlockSpec((TQ, width), lambda i: (i, 0)),
                  pl.BlockSpec((TQ, width), lambda i: (i, 0)),
                  half_spec(0), half_spec(1)],
        out_specs=pl.BlockSpec((TQ, width), lambda i: (i, 0)),
        out_shape=jax.ShapeDtypeStruct((s, width), jnp.bfloat16),
        scratch_shapes=[pltpu.VMEM((N_KV, groups * TQ, HEAD_DIM), jnp.bfloat16),
                        pltpu.VMEM((N_KV, groups * TQ, LANES), jnp.float32),
                        pltpu.VMEM((N_KV, groups * TQ, HEAD_DIM), jnp.float32)],
        compiler_params=_cparams(("parallel",)),
        name="mixer_b",
    )(q, gate, kv, kv)


def _post_kernel(og_ref, x_ref, p_ref, wo_ref, wg_ref, wp_ref, lng_ref, lnb_ref, *refs, alpha, n_side):
    f32 = jnp.float32
    o_ref = refs[n_side]
    _cast_sides(refs[:n_side], refs[n_side + 1:])
    h = jnp.dot(og_ref[...], wo_ref[...], preferred_element_type=f32)
    y = alpha * x_ref[...] + h
    mu = jnp.mean(y, axis=-1, keepdims=True)
    d = y - mu
    var = jnp.mean(d * d, axis=-1, keepdims=True)
    yn = d * lax.rsqrt(var + LN_EPS) * lng_ref[...] + lnb_ref[...]
    gl = jnp.dot(yn.astype(jnp.bfloat16), wg_ref[...], preferred_element_type=f32)
    pe = jnp.dot(p_ref[...].astype(jnp.bfloat16), wp_ref[...], preferred_element_type=f32)
    o_ref[...] = yn + pe * jax.nn.sigmoid(gl)


def _resident_slab(arr, idx):
    return pl.BlockSpec((None,) + arr.shape[1:], lambda *_: (idx,) + (0,) * (arr.ndim - 1),
                        pipeline_mode=pl.Buffered(1))


def _post(og, x, p, layer, batch, wo, wg, wp, lng, lnb, alpha, sides=()):
    s, d = x.shape
    tm = TM_PROJ
    side_in, side_out, side_shapes = _side_io(sides, s // tm)
    res = pl.pallas_call(
        functools.partial(_post_kernel, alpha=alpha, n_side=len(sides)),
        grid=(s // tm,),
        in_specs=[pl.BlockSpec((tm, og.shape[1]), lambda i: (i, 0)),
                  pl.BlockSpec((tm, d), lambda i: (i, 0)),
                  pl.BlockSpec((None, None, tm, p.shape[3]), lambda i: (layer, batch, i, 0)),
                  _resident(wo.shape), _resident(wg.shape), _resident_slab(wp, layer),
                  _resident_slab(lng, layer), _resident_slab(lnb, layer)] + side_in,
        out_specs=[pl.BlockSpec((tm, d), lambda i: (i, 0))] + side_out,
        out_shape=[jax.ShapeDtypeStruct((s, d), jnp.float32)] + side_shapes,
        compiler_params=_cparams(("parallel",)),
        name="post",
    )(og, x, p, wo, wg, wp, lng, lnb, *(sd.arr for sd in sides))
    return res[0], res[1:]


def _now_bf16(sd):
    return (sd.arr if sd.layer < 0 else sd.arr[sd.layer]).astype(jnp.bfloat16)


def kernel(x, p, positions, w_in_a, w_o_a, w_kv_b, w_in_b, w_o_b, ln_g, ln_b, w_ple, w_ple_gate):
    bsz, s, d = x.shape
    n_a, n_b = w_in_a.shape[0], w_in_b.shape[0]
    depth = n_a + n_b
    alpha = (2.0 * depth) ** 0.25
    width = w_o_a.shape[1]
    kvw = N_KV * HEAD_DIM
    idxw = IDX_HEADS * IDX_DIM
    n_sel = min(TOPK_MAX, s // 4)
    bf16, f32 = jnp.bfloat16, jnp.float32
    sm_scale = HEAD_DIM ** -0.5
    assert s % TK2 == 0 and s // COUNT_ROWS <= 256
    n_steps = s // TM_PROJ
    wp_all = w_ple.astype(bf16)
    lng, lnb = ln_g[:, None, :], ln_b[:, None, :]

    def proj_sources(layer):
        if layer < n_a:
            return [Side(jnp.swapaxes(w_in_a[layer], 0, 1), -1)]
        first_b = [Side(w_kv_b, -1)] if layer == n_a else []
        return first_b + [Side(w_in_b, layer - n_a)]

    def riders(srcs):
        return srcs if all(_side_ok(sd.arr, n_steps) for sd in srcs) else []

    proj_w, post_w = {}, {}
    outs = []
    for b in range(bsz):
        xb = x[b]
        ropes = (positions[b].astype(f32)[:, None],
                 jnp.stack([_rope_freq_row(HEAD_DIM, ROT_DIM), _rope_freq_row(IDX_DIM, IDX_ROT)]),
                 ((HEAD_DIM, ROT_DIM // 2), (IDX_DIM, IDX_ROT // 2)))
        kv_shared = None
        for i in range(depth):
            if i not in proj_w:
                proj_w[i] = [_now_bf16(sd) for sd in proj_sources(i)]
            post_src = [Side(w_o_a, i) if i < n_a else Side(w_o_b, i - n_a), Side(w_ple_gate, i)]
            post_ride = riders(post_src) if i not in post_w else []
            if i < n_a:
                plan = [(0, width, width, 0, 0, LOG2_E * sm_scale, False), (0, kvw, kvw, 1, 0, 1.0, False),
                        (0, kvw, kvw, 2, -1, 1.0, False), (0, width, width, 3, -1, 1.0, True),
                        (0, idxw, idxw, 4, 1, 1.0, False), (0, IDX_DIM, LANES, 4, 1, 1.0, False),
                        (0, IDX_HEADS, IDX_HEADS, 5, -1, 1.0, False)]
                (q, k, v, gate, ix, iw), cast = _fused_project(
                    xb, proj_w[i], [True], plan, [bf16, bf16, bf16, f32, bf16, f32], ropes, post_ride)
                og = _mixer_a(q, ix, iw, gate, k, v, n_sel)
            elif kv_shared is None:
                plan = [(0, 2 * kvw, 2 * kvw, 0, -1, 1.0, False), (1, width, width, 1, -1, sm_scale, False),
                        (1, width, width, 2, -1, 1.0, True)]
                (kv_shared, q, gate), cast = _fused_project(xb, proj_w[i], [False, False], plan, [bf16, bf16, f32],
                                                            None, post_ride)
                og = _mixer_b(q, gate, kv_shared)
            else:
                plan = [(0, width, width, 0, -1, sm_scale, False), (0, width, width, 1, -1, 1.0, True)]
                (q, gate), cast = _fused_project(xb, proj_w[i], [False], plan, [bf16, f32], None, post_ride)
                og = _mixer_b(q, gate, kv_shared)
            if i not in post_w:
                post_w[i] = tuple(cast) if post_ride else tuple(_now_bf16(sd) for sd in post_src)
            next_ride = riders(proj_sources(i + 1)) if i + 1 < depth and i + 1 not in proj_w else []
            xb, cast = _post(og, xb, p, i, b, post_w[i][0], post_w[i][1], wp_all, lng, lnb, alpha, next_ride)
            if next_ride:
                proj_w[i + 1] = list(cast)
        outs.append(xb)
    return jnp.stack(outs, axis=0)
```

```python
import functools
from typing import NamedTuple

import jax
import jax.numpy as jnp
from jax import lax
from jax.experimental import pallas as pl
from jax.experimental.pallas import tpu as pltpu

HEAD_DIM = 128
N_KV = 4
ROT_DIM = HEAD_DIM // 4
IDX_HEADS = 16
IDX_DIM = 64
IDX_ROT = IDX_DIM // 4
TOPK_MAX = 256
ROPE_THETA = 500000.0
LN_EPS = 1e-5
LOG2_E = 1.4426950408889634

LANES = 128
BF16_SUBLANES = 16
V7X_VMEM_BYTES = 64 * 1024 * 1024
VMEM_LIMIT = V7X_VMEM_BYTES - 8 * 1024 * 1024

TQ = 256
TK2 = 2 * TQ
TM_PROJ = 256
SEG_MAX = 1024
NEG_BIG = -1e30
EXP_ZERO_ABOVE = 110.0
MAX_BISECT = 256
BISECT_STEPS_PER_CHECK = 2
UNCHECKED_STEPS = 4
COARSE_STEPS = 9


def _cparams(sem):
    return pltpu.CompilerParams(dimension_semantics=sem, vmem_limit_bytes=VMEM_LIMIT)


def _resident(shape):
    nd = len(shape)
    return pl.BlockSpec(shape, lambda *_: (0,) * nd, pipeline_mode=pl.Buffered(1))


class Seg(NamedTuple):
    w: int
    col: int
    width: int
    stored: int
    out: int
    out_col: int
    rope: int
    scale: float
    silu: bool


def _rope_slab(y, half, c, s_up, s_dn):
    up = pltpu.roll(y, LANES - half, axis=1)
    dn = pltpu.roll(y, half, axis=1)
    return y * c + up * s_up + dn * s_dn


class Side(NamedTuple):
    arr: jax.Array
    layer: int


def _side_ok(arr, n_steps):
    rows = arr.shape[-2]
    return rows % (n_steps * BF16_SUBLANES) == 0


def _side_io(sides, n_steps):
    in_specs, out_specs, out_shapes = [], [], []
    for sd in sides:
        r, c = sd.arr.shape[-2:]
        rows = r // n_steps
        if sd.layer < 0:
            in_specs.append(pl.BlockSpec((rows, c), lambda i: (i, 0)))
        else:
            in_specs.append(pl.BlockSpec((None, rows, c), lambda i, l=sd.layer: (l, i, 0)))
        out_specs.append(pl.BlockSpec((rows, c), lambda i: (i, 0)))
        out_shapes.append(jax.ShapeDtypeStruct((r, c), jnp.bfloat16))
    return in_specs, out_specs, out_shapes


def _cast_sides(side_in, side_out):
    for src, dst in zip(side_in, side_out):
        dst[...] = src[...].astype(dst.dtype)


def _rope_slab_tables(pos, freq, head_dim, half):
    ang = pos * freq
    c, sn = jnp.cos(ang), jnp.sin(ang)
    lane = lax.broadcasted_iota(jnp.int32, ang.shape, 1) & (head_dim - 1)
    s_up = jnp.where(lane < half, -sn, 0.0)
    s_dn = jnp.where(lane < half, 0.0, jnp.where(lane < 2 * half, sn, 0.0))
    return c, s_up, s_dn


def _fused_proj_kernel(x_ref, *refs, segs, rope_dims, w_transposed, n_side):
    n_w, n_rope = len(w_transposed), 2 if rope_dims else 0
    ws = refs[:n_w]
    side_in = refs[n_w + n_rope:n_w + n_rope + n_side]
    outs = refs[n_w + n_rope + n_side:len(refs) - n_side]
    _cast_sides(side_in, refs[len(refs) - n_side:])
    if rope_dims:
        pos_ref, freq_ref = refs[n_w:n_w + 2]
        tabs = [_rope_slab_tables(pos_ref[...], freq_ref[r:r + 1, :], hd, half)
                for r, (hd, half) in enumerate(rope_dims)]
    x = x_ref[...].astype(jnp.bfloat16)
    for sg in segs:
        if w_transposed[sg.w]:
            y = lax.dot_general(x, ws[sg.w][sg.col:sg.col + sg.width, :], (((1,), (1,)), ((), ())),
                                preferred_element_type=jnp.float32)
        else:
            y = jnp.dot(x, ws[sg.w][:, sg.col:sg.col + sg.width], preferred_element_type=jnp.float32)
        if sg.scale != 1.0:
            y = y * sg.scale
        if sg.silu:
            y = y * jax.nn.sigmoid(y)
        if sg.stored > sg.width:
            y = jnp.concatenate([y, jnp.zeros((y.shape[0], sg.stored - sg.width), y.dtype)], axis=1)
        if sg.rope >= 0:
            c, s_up, s_dn = tabs[sg.rope]
            slabs = [_rope_slab(y[:, j * LANES:(j + 1) * LANES], rope_dims[sg.rope][1], c, s_up, s_dn)
                     for j in range(sg.stored // LANES)]
            y = jnp.concatenate(slabs, axis=1) if len(slabs) > 1 else slabs[0]
        o_ref = outs[sg.out]
        o_ref[:, sg.out_col:sg.out_col + sg.stored] = y.astype(o_ref.dtype)


def _fused_project(x, ws, w_transposed, plan, out_dtypes, ropes, sides=()):
    s, kdim = x.shape
    segs, cols, out_cols = [], [0] * len(ws), [0] * len(out_dtypes)
    for wi, width, stored, out, rope, scale, silu in plan:
        assert stored == width or width <= SEG_MAX
        for c0 in range(0, width, SEG_MAX):
            wd = min(SEG_MAX, width - c0)
            segs.append(Seg(wi, cols[wi] + c0, wd, wd if stored == width else stored, out, out_cols[out] + c0,
                            rope, scale, silu))
        cols[wi] += width
        out_cols[out] += stored
    assert all(c == w.shape[0 if t else 1] for c, w, t in zip(cols, ws, w_transposed))
    tm = TM_PROJ
    rope_in, rope_specs = [], []
    if ropes is not None:
        pos, freqs, _ = ropes
        rope_in = [pos, freqs]
        rope_specs = [pl.BlockSpec((tm, 1), lambda i: (i, 0)), _resident(freqs.shape)]
    side_in, side_out, side_shapes = _side_io(sides, s // tm)
    res = pl.pallas_call(
        functools.partial(_fused_proj_kernel, segs=tuple(segs), rope_dims=ropes[2] if ropes is not None else (),
                          w_transposed=tuple(w_transposed), n_side=len(sides)),
        grid=(s // tm,),
        in_specs=[pl.BlockSpec((tm, kdim), lambda i: (i, 0))] + [_resident(w.shape) for w in ws]
                 + rope_specs + side_in,
        out_specs=[pl.BlockSpec((tm, n), lambda i: (i, 0)) for n in out_cols] + side_out,
        out_shape=[jax.ShapeDtypeStruct((s, n), dt) for n, dt in zip(out_cols, out_dtypes)] + side_shapes,
        compiler_params=_cparams(("parallel",)),
        name="fused_proj",
    )(x, *ws, *rope_in, *(sd.arr for sd in sides))
    return res[:len(out_dtypes)], res[len(out_dtypes):]


def _rope_freq_row(head_dim, rot_dim):
    half = rot_dim // 2
    inv = 1.0 / (ROPE_THETA ** (jnp.arange(half, dtype=jnp.float32) / half))
    row = jnp.concatenate([inv, inv, jnp.zeros((head_dim - rot_dim,), jnp.float32)])
    return jnp.tile(row, LANES // head_dim)


def _stack_heads(q_ref, n, groups):
    return jnp.concatenate(
        [q_ref[:, (n * groups + g) * HEAD_DIM:(n * groups + g + 1) * HEAD_DIM] for g in range(groups)], axis=0)


def _store_gated(o_ref, gate_ref, out, n, groups):
    tq = o_ref.shape[0]
    for g in range(groups):
        cols = slice((n * groups + g) * HEAD_DIM, (n * groups + g + 1) * HEAD_DIM)
        o_ref[:, cols] = (out[g * tq:(g + 1) * tq, :] * gate_ref[:, cols]).astype(o_ref.dtype)


COUNT_ROWS = 64


def _mixer_a_kernel(q_ref, iq_ref, iw_ref, gate_ref, ik_ref, k_ref, v_ref, o_ref,
                     score_ref, sb_ref, wt_ref, lo_ref, hi_ref, mn_ref, mx_ref,
                     qn_ref, z_ref, bias_ref, m_ref, acc_ref, *, n_sel, idx_scale):
    tq = q_ref.shape[0]
    groups = q_ref.shape[1] // HEAD_DIM // N_KV
    i = pl.program_id(0)
    nkb2 = (i + 2) // 2
    f32, bf16 = jnp.float32, jnp.bfloat16
    nt = (((1,), (1,)), ((), ()))

    key_in_blk = lax.broadcasted_iota(jnp.int32, (TK2, tq), 0)
    q_pos = lax.broadcasted_iota(jnp.int32, (TK2, tq), 1) + i * tq

    iw = iw_ref[...] * idx_scale
    wt_ref[...] = jnp.concatenate([iw, jnp.zeros((tq, LANES - IDX_HEADS), f32)], axis=1).T

    mn_ref[...] = jnp.full((tq, tq), jnp.inf, f32)
    mx_ref[...] = jnp.full((tq, tq), -jnp.inf, f32)

    def score_body(kb2, carry):
        koff = pl.multiple_of(kb2 * TK2, TK2)
        ik = ik_ref[pl.ds(koff, TK2), :IDX_DIM]
        acc = jnp.zeros((TK2, tq), f32)
        for h in range(IDX_HEADS):
            d = lax.dot_general(ik, iq_ref[:, h * IDX_DIM:(h + 1) * IDX_DIM], nt, preferred_element_type=f32)
            acc = acc + wt_ref[h:h + 1, :] * jnp.maximum(d, 0.0)
        causal = (key_in_blk + kb2 * TK2) <= q_pos
        sc = jnp.where(causal, acc, -jnp.inf)
        score_ref[pl.ds(koff, TK2), :] = sc
        sb_ref[pl.ds(koff, TK2), :] = sc.astype(bf16)
        lo_part = jnp.where(causal, acc, jnp.inf)
        mn_ref[...] = jnp.minimum(mn_ref[...], jnp.minimum(lo_part[:tq], lo_part[tq:]))
        mx_ref[...] = jnp.maximum(mx_ref[...], jnp.maximum(sc[:tq], sc[tq:]))
        return carry

    lax.fori_loop(0, nkb2, score_body, 0)

    t1 = (lax.broadcasted_iota(jnp.int32, (1, tq), 1) + i * tq + 1).astype(f32)
    k_t = jnp.minimum(t1, float(n_sel))
    row_min = jnp.min(mn_ref[...], axis=0, keepdims=True)
    row_max = jnp.max(mx_ref[...], axis=0, keepdims=True)
    if tq >= n_sel:
        lo0 = jnp.where(t1 >= float(tq), jnp.min(mx_ref[...], axis=0, keepdims=True), row_min)
    else:
        lo0 = row_min
    def to_b(x):
        return x.astype(bf16).astype(f32)

    def coarse_step(_, bracket):
        lo_b, hi_b = bracket
        mid = to_b(0.5 * lo_b + 0.5 * hi_b)
        mid16 = mid.astype(bf16)

        def count_body(kb2, acc):
            koff = pl.multiple_of(kb2 * TK2, TK2)
            for c in range(TK2 // COUNT_ROWS):
                blk = sb_ref[pl.ds(koff + c * COUNT_ROWS, COUNT_ROWS), :]
                acc = acc + jnp.where(blk >= mid16, jnp.ones((), bf16), jnp.zeros((), bf16))
            return acc

        acc = lax.fori_loop(0, nkb2, count_body, jnp.zeros((COUNT_ROWS, tq), bf16))
        ge = jnp.sum(acc.astype(f32), axis=0, keepdims=True) >= k_t
        return jnp.where(ge, mid, lo_b), jnp.where(ge, hi_b, mid)

    hi_b0 = to_b(row_max + jnp.abs(row_max) * 2.0 ** -6 + 1e-30)
    lo_b, hi_b = lax.fori_loop(0, COARSE_STEPS, coarse_step, (to_b(lo0), hi_b0))
    lo1 = jnp.maximum(lo0, lo_b - jnp.abs(lo_b) * 2.0 ** -7 - 1e-30)
    hi1 = jnp.minimum(row_max, hi_b)

    settled = t1 <= float(n_sel)
    lo_ref[...] = jnp.broadcast_to(jnp.where(settled, lo0, lo1), lo_ref.shape)
    hi_ref[...] = jnp.broadcast_to(jnp.where(settled, row_min, hi1), hi_ref.shape)

    def unresolved():
        lo, hi = lo_ref[0:1, :], hi_ref[0:1, :]
        mid = 0.5 * lo + 0.5 * hi
        open_ = jnp.where(mid > lo, 1.0, 0.0) * jnp.where(mid < hi, 1.0, 0.0)
        return (jnp.max(open_) > 0.0).astype(jnp.int32)

    def bisect_step():
        lo, hi = lo_ref[0:1, :], hi_ref[0:1, :]
        mid = 0.5 * lo + 0.5 * hi

        def count_body(kb2, acc):
            koff = pl.multiple_of(kb2 * TK2, TK2)
            for c in range(TK2 // COUNT_ROWS):
                blk = score_ref[pl.ds(koff + c * COUNT_ROWS, COUNT_ROWS), :]
                acc = acc + jnp.where(blk >= mid, 1.0, 0.0)
            return acc

        acc = lax.fori_loop(0, nkb2, count_body, jnp.zeros((COUNT_ROWS, tq), f32))
        cnt = jnp.sum(acc, axis=0, keepdims=True)
        ge = cnt >= k_t
        lo_ref[...] = jnp.broadcast_to(jnp.where(ge, mid, lo), lo_ref.shape)
        hi_ref[...] = jnp.broadcast_to(jnp.where(cnt == k_t, mid, jnp.where(ge, hi, mid)), hi_ref.shape)

    def bisect_body(state):
        for _ in range(BISECT_STEPS_PER_CHECK):
            bisect_step()
        return state[0] + BISECT_STEPS_PER_CHECK, unresolved()

    lax.fori_loop(0, UNCHECKED_STEPS, lambda _, c: (bisect_step(), c)[1], 0)
    lax.while_loop(lambda st: jnp.logical_and(st[1] > 0, st[0] < MAX_BISECT), bisect_body,
                   (jnp.int32(0), unresolved()))

    thr = lo_ref[0:1, :]
    m_rows = groups * tq
    for n in range(N_KV):
        qn_ref[n] = _stack_heads(q_ref, n, groups)
    m_ref[...] = jnp.full(m_ref.shape, NEG_BIG, f32)
    acc_ref[...] = jnp.zeros(acc_ref.shape, f32)

    def logits(kb2, n):
        kblk = k_ref[pl.ds(pl.multiple_of(kb2 * TK2, TK2), TK2), n * HEAD_DIM:(n + 1) * HEAD_DIM]
        return lax.dot_general(qn_ref[n], kblk, nt, preferred_element_type=f32).astype(bf16)

    def mask_bias(kb2):
        blk = score_ref[pl.ds(pl.multiple_of(kb2 * TK2, TK2), TK2), :]
        return jnp.where(blk >= thr, 0.0, NEG_BIG).astype(bf16).T

    for n in range(N_KV):
        z_ref[n] = logits(0, n)
    bias_ref[...] = mask_bias(0)

    def attend(kb2, prefetch):
        koff = pl.multiple_of(kb2 * TK2, TK2)
        bias = jnp.concatenate([bias_ref[...]] * groups, axis=0)
        if prefetch:
            bias_ref[...] = mask_bias(kb2 + 1)
        for n in range(N_KV):
            vblk = v_ref[pl.ds(koff, TK2), n * HEAD_DIM:(n + 1) * HEAD_DIM]
            vext = jnp.concatenate([vblk, jnp.ones_like(vblk)], axis=1)
            z = z_ref[n] + bias
            if prefetch:
                z_ref[n] = logits(kb2 + 1, n)
            m_old = m_ref[n]
            m_new = jnp.maximum(m_old, jnp.max(z, axis=1, keepdims=True).astype(f32))
            p = jnp.exp2(z - jnp.concatenate([m_new.astype(bf16)] * (TK2 // LANES), axis=1))
            alpha = jnp.exp2(m_old - m_new)
            acc_ref[n] = (jnp.concatenate([alpha, alpha], axis=1) * acc_ref[n]
                          + jnp.dot(p, vext, preferred_element_type=f32))
            m_ref[n] = m_new

    def attn_body(kb2, carry):
        attend(kb2, True)
        return carry

    lax.fori_loop(0, nkb2 - 1, attn_body, 0)
    attend(nkb2 - 1, False)
    for n in range(N_KV):
        acc = acc_ref[n]
        _store_gated(o_ref, gate_ref, acc[:, :HEAD_DIM] / acc[:, HEAD_DIM:], n, groups)


def _mixer_a(q, ix, iw, gate, k, v, n_sel):
    s, width = q.shape
    groups = width // HEAD_DIM // N_KV
    f32 = jnp.float32
    idxw = IDX_HEADS * IDX_DIM
    ik_spec = pl.BlockSpec((s, LANES), lambda i: (0, idxw // LANES), pipeline_mode=pl.Buffered(1))
    kern = functools.partial(_mixer_a_kernel, n_sel=n_sel, idx_scale=(IDX_DIM ** -0.5) * (IDX_HEADS ** -0.5))
    return pl.pallas_call(
        kern,
        grid=(s // TQ,),
        in_specs=[pl.BlockSpec((TQ, width), lambda i: (i, 0)),
                  pl.BlockSpec((TQ, IDX_HEADS * IDX_DIM), lambda i: (i, 0)),
                  pl.BlockSpec((TQ, IDX_HEADS), lambda i: (i, 0)),
                  pl.BlockSpec((TQ, width), lambda i: (i, 0)),
                  ik_spec, _resident(k.shape), _resident(v.shape)],
        out_specs=pl.BlockSpec((TQ, width), lambda i: (i, 0)),
        out_shape=jax.ShapeDtypeStruct((s, width), jnp.bfloat16),
        scratch_shapes=[pltpu.VMEM((s, TQ), f32),
                        pltpu.VMEM((s, TQ), jnp.bfloat16),
                        pltpu.VMEM((LANES, TQ), f32),
                        pltpu.VMEM((8, TQ), f32),
                        pltpu.VMEM((8, TQ), f32),
                        pltpu.VMEM((TQ, TQ), f32),
                        pltpu.VMEM((TQ, TQ), f32),
                        pltpu.VMEM((N_KV, groups * TQ, HEAD_DIM), jnp.bfloat16),
                        pltpu.VMEM((N_KV, groups * TQ, TK2), jnp.bfloat16),
                        pltpu.VMEM((TQ, TK2), jnp.bfloat16),
                        pltpu.VMEM((N_KV, groups * TQ, LANES), f32),
                        pltpu.VMEM((N_KV, groups * TQ, 2 * HEAD_DIM), f32)],
        compiler_params=_cparams(("parallel",)),
        name="mixer_a",
    )(q, ix, iw, gate, ix, k, v)


def _mixer_b_kernel(q_ref, gate_ref, k_ref, v_ref, o_ref, qn_ref, c_ref, acc_ref):
    tq = q_ref.shape[0]
    tk = tq
    groups = q_ref.shape[1] // HEAD_DIM // N_KV
    m_rows = groups * tq
    i = pl.program_id(0)
    f32, bf16 = jnp.float32, jnp.bfloat16

    row = lax.broadcasted_iota(jnp.int32, (tq, tk), 0)
    col = lax.broadcasted_iota(jnp.int32, (tq, tk), 1)
    later = jnp.where(row > col, 1.0, 0.0).astype(bf16)
    later2 = jnp.concatenate([later, later], axis=0)
    strict = jnp.concatenate([col < row] * groups, axis=0)

    for n in range(N_KV):
        qn_ref[n] = _stack_heads(q_ref, n, groups)

    def step(kb, diagonal):
        koff = pl.multiple_of(kb * tk, tk)
        c_min = None
        for n in range(N_KV):
            kblk = k_ref[pl.ds(koff, tk), n * HEAD_DIM:(n + 1) * HEAD_DIM]
            vblk = v_ref[pl.ds(koff, tk), n * HEAD_DIM:(n + 1) * HEAD_DIM]
            z = lax.dot_general(qn_ref[n], kblk, (((1,), (1,)), ((), ())), preferred_element_type=f32)
            t = jnp.log(1.0 + jnp.exp2(jnp.abs(z) * (-LOG2_E)))
            sp = jnp.maximum(z, 0.0) + t
            if diagonal:
                sp = jnp.where(strict, sp, 0.0)
            sp_hi = sp.astype(bf16)
            sp_lo = (sp - sp_hi.astype(f32)).astype(bf16)
            suffix = jnp.dot(jnp.concatenate([sp_hi, sp_lo], axis=1), later2, preferred_element_type=f32)
            c_old = c_ref[n]
            total = suffix + jnp.concatenate([c_old] * (tk // LANES), axis=1)
            a = jnp.exp((jnp.minimum(z, 0.0) - t) - total)
            if diagonal:
                a = jnp.where(strict, a, 0.0)
            acc_ref[n] += jnp.dot(a.astype(bf16), vblk, preferred_element_type=f32)
            c_new = c_old + jnp.broadcast_to(jnp.sum(sp, axis=1, keepdims=True), (m_rows, LANES))
            c_ref[n] = c_new
            c_min = c_new if c_min is None else jnp.minimum(c_min, c_new)
        return (jnp.min(c_min) <= EXP_ZERO_ABOVE).astype(jnp.int32)

    c_ref[...] = jnp.zeros((N_KV, m_rows, LANES), f32)
    acc_ref[...] = jnp.zeros((N_KV, m_rows, HEAD_DIM), f32)
    go = step(i, True)
    lax.while_loop(lambda st: jnp.logical_and(st[0] >= 0, st[1] > 0),
                   lambda st: (st[0] - 1, step(st[0], False)), (i - 1, go))
    for n in range(N_KV):
        _store_gated(o_ref, gate_ref, acc_ref[n], n, groups)


def _mixer_b(q, gate, kv):
    s, width = q.shape
    groups = width // HEAD_DIM // N_KV
    kvw = kv.shape[1] // 2
    half_spec = lambda c: pl.BlockSpec((s, kvw), lambda i: (0, c), pipeline_mode=pl.Buffered(1))
    return pl.pallas_call(
        _mixer_b_kernel,
        grid=(s // TQ,),
        in_specs=[pl.BlockSpec((TQ, width), lambda i: (i, 0)),
                  pl.BlockSpec((TQ, width), lambda i: (i, 0)),
                  half_spec(0), half_spec(1)],
        out_specs=pl.BlockSpec((TQ, width), lambda i: (i, 0)),
        out_shape=jax.ShapeDtypeStruct((s, width), jnp.bfloat16),
        scratch_shapes=[pltpu.VMEM((N_KV, groups * TQ, HEAD_DIM), jnp.bfloat16),
                        pltpu.VMEM((N_KV, groups * TQ, LANES), jnp.float32),
                        pltpu.VMEM((N_KV, groups * TQ, HEAD_DIM), jnp.float32)],
        compiler_params=_cparams(("parallel",)),
        name="mixer_b",
    )(q, gate, kv, kv)


def _post_kernel(og_ref, x_ref, p_ref, wo_ref, wg_ref, wp_ref, lng_ref, lnb_ref, *refs, alpha, n_side):
    f32 = jnp.float32
    o_ref = refs[n_side]
    _cast_sides(refs[:n_side], refs[n_side + 1:])
    h = jnp.dot(og_ref[...], wo_ref[...], preferred_element_type=f32)
    y = alpha * x_ref[...] + h
    mu = jnp.mean(y, axis=-1, keepdims=True)
    d = y - mu
    var = jnp.mean(d * d, axis=-1, keepdims=True)
    yn = d * lax.rsqrt(var + LN_EPS) * lng_ref[...] + lnb_ref[...]
    gl = jnp.dot(yn.astype(jnp.bfloat16), wg_ref[...], preferred_element_type=f32)
    pe = jnp.dot(p_ref[...].astype(jnp.bfloat16), wp_ref[...], preferred_element_type=f32)
    o_ref[...] = yn + pe * jax.nn.sigmoid(gl)


def _resident_slab(arr, idx):
    return pl.BlockSpec((None,) + arr.shape[1:], lambda *_: (idx,) + (0,) * (arr.ndim - 1),
                        pipeline_mode=pl.Buffered(1))


def _post(og, x, p, layer, batch, wo, wg, wp, lng, lnb, alpha, sides=()):
    s, d = x.shape
    tm = TM_PROJ
    side_in, side_out, side_shapes = _side_io(sides, s // tm)
    res = pl.pallas_call(
        functools.partial(_post_kernel, alpha=alpha, n_side=len(sides)),
        grid=(s // tm,),
        in_specs=[pl.BlockSpec((tm, og.shape[1]), lambda i: (i, 0)),
                  pl.BlockSpec((tm, d), lambda i: (i, 0)),
                  pl.BlockSpec((None, None, tm, p.shape[3]), lambda i: (layer, batch, i, 0)),
                  _resident(wo.shape), _resident(wg.shape), _resident_slab(wp, layer),
                  _resident_slab(lng, layer), _resident_slab(lnb, layer)] + side_in,
        out_specs=[pl.BlockSpec((tm, d), lambda i: (i, 0))] + side_out,
        out_shape=[jax.ShapeDtypeStruct((s, d), jnp.float32)] + side_shapes,
        compiler_params=_cparams(("parallel",)),
        name="post",
    )(og, x, p, wo, wg, wp, lng, lnb, *(sd.arr for sd in sides))
    return res[0], res[1:]


def _now_bf16(sd):
    return (sd.arr if sd.layer < 0 else sd.arr[sd.layer]).astype(jnp.bfloat16)


def kernel(x, p, positions, w_in_a, w_o_a, w_kv_b, w_in_b, w_o_b, ln_g, ln_b, w_ple, w_ple_gate):
    bsz, s, d = x.shape
    n_a, n_b = w_in_a.shape[0], w_in_b.shape[0]
    depth = n_a + n_b
    alpha = (2.0 * depth) ** 0.25
    width = w_o_a.shape[1]
    kvw = N_KV * HEAD_DIM
    idxw = IDX_HEADS * IDX_DIM
    n_sel = min(TOPK_MAX, s // 4)
    bf16, f32 = jnp.bfloat16, jnp.float32
    sm_scale = HEAD_DIM ** -0.5
    assert s % TK2 == 0 and s // COUNT_ROWS <= 256
    n_steps = s // TM_PROJ
    wp_all = w_ple.astype(bf16)
    lng, lnb = ln_g[:, None, :], ln_b[:, None, :]

    def proj_sources(layer):
        if layer < n_a:
            return [Side(jnp.swapaxes(w_in_a[layer], 0, 1), -1)]
        first_b = [Side(w_kv_b, -1)] if layer == n_a else []
        return first_b + [Side(w_in_b, layer - n_a)]

    def riders(srcs):
        return srcs if all(_side_ok(sd.arr, n_steps) for sd in srcs) else []

    proj_w, post_w = {}, {}
    outs = []
    for b in range(bsz):
        xb = x[b]
        ropes = (positions[b].astype(f32)[:, None],
                 jnp.stack([_rope_freq_row(HEAD_DIM, ROT_DIM), _rope_freq_row(IDX_DIM, IDX_ROT)]),
                 ((HEAD_DIM, ROT_DIM // 2), (IDX_DIM, IDX_ROT // 2)))
        kv_shared = None
        for i in range(depth):
            if i not in proj_w:
                proj_w[i] = [_now_bf16(sd) for sd in proj_sources(i)]
            post_src = [Side(w_o_a, i) if i < n_a else Side(w_o_b, i - n_a), Side(w_ple_gate, i)]
            post_ride = riders(post_src) if i not in post_w else []
            if i < n_a:
                plan = [(0, width, width, 0, 0, LOG2_E * sm_scale, False), (0, kvw, kvw, 1, 0, 1.0, False),
                        (0, kvw, kvw, 2, -1, 1.0, False), (0, width, width, 3, -1, 1.0, True),
                        (0, idxw, idxw, 4, 1, 1.0, False), (0, IDX_DIM, LANES, 4, 1, 1.0, False),
                        (0, IDX_HEADS, IDX_HEADS, 5, -1, 1.0, False)]
                (q, k, v, gate, ix, iw), cast = _fused_project(
                    xb, proj_w[i], [True], plan, [bf16, bf16, bf16, f32, bf16, f32], ropes, post_ride)
                og = _mixer_a(q, ix, iw, gate, k, v, n_sel)
            elif kv_shared is None:
                plan = [(0, 2 * kvw, 2 * kvw, 0, -1, 1.0, False), (1, width, width, 1, -1, sm_scale, False),
                        (1, width, width, 2, -1, 1.0, True)]
                (kv_shared, q, gate), cast = _fused_project(xb, proj_w[i], [False, False], plan, [bf16, bf16, f32],
                                                            None, post_ride)
                og = _mixer_b(q, gate, kv_shared)
            else:
                plan = [(0, width, width, 0, -1, sm_scale, False), (0, width, width, 1, -1, 1.0, True)]
                (q, gate), cast = _fused_project(xb, proj_w[i], [False], plan, [bf16, f32], None, post_ride)
                og = _mixer_b(q, gate, kv_shared)
            if i not in post_w:
                post_w[i] = tuple(cast) if post_ride else tuple(_now_bf16(sd) for sd in post_src)
            next_ride = riders(proj_sources(i + 1)) if i + 1 < depth and i + 1 not in proj_w else []
            xb, cast = _post(og, xb, p, i, b, post_w[i][0], post_w[i][1], wp_all, lng, lnb, alpha, next_ride)
            if next_ride:
                proj_w[i + 1] = list(cast)
        outs.append(xb)
    return jnp.stack(outs, axis=0)
```

```python
import functools
from typing import NamedTuple

import jax
import jax.numpy as jnp
from jax import lax
from jax.experimental import pallas as pl
from jax.experimental.pallas import tpu as pltpu

HEAD_DIM = 128
N_KV = 4
ROT_DIM = HEAD_DIM // 4
IDX_HEADS = 16
IDX_DIM = 64
IDX_ROT = IDX_DIM // 4
TOPK_MAX = 256
ROPE_THETA = 500000.0
LN_EPS = 1e-5
LOG2_E = 1.4426950408889634

LANES = 128
BF16_SUBLANES = 16
V7X_VMEM_BYTES = 64 * 1024 * 1024
VMEM_LIMIT = V7X_VMEM_BYTES - 8 * 1024 * 1024

TQ = 256
TK2 = 2 * TQ
TM_PROJ = 256
SEG_MAX = 1024
NEG_BIG = -1e30
EXP_ZERO_ABOVE = 110.0
MAX_BISECT = 256
BISECT_STEPS_PER_CHECK = 2
UNCHECKED_STEPS = 4
COARSE_STEPS = 9


def _cparams(sem):
    return pltpu.CompilerParams(dimension_semantics=sem, vmem_limit_bytes=VMEM_LIMIT)


def _resident(shape):
    nd = len(shape)
    return pl.BlockSpec(shape, lambda *_: (0,) * nd, pipeline_mode=pl.Buffered(1))


class Seg(NamedTuple):
    w: int
    col: int
    width: int
    stored: int
    out: int
    out_col: int
    rope: int
    scale: float
    silu: bool


def _rope_slab(y, half, c, s_up, s_dn):
    up = pltpu.roll(y, LANES - half, axis=1)
    dn = pltpu.roll(y, half, axis=1)
    return y * c + up * s_up + dn * s_dn


class Side(NamedTuple):
    arr: jax.Array
    layer: int


def _side_ok(arr, n_steps):
    rows = arr.shape[-2]
    return rows % (n_steps * BF16_SUBLANES) == 0


def _side_io(sides, n_steps):
    in_specs, out_specs, out_shapes = [], [], []
    for sd in sides:
        r, c = sd.arr.shape[-2:]
        rows = r // n_steps
        if sd.layer < 0:
            in_specs.append(pl.BlockSpec((rows, c), lambda i: (i, 0)))
        else:
            in_specs.append(pl.BlockSpec((None, rows, c), lambda i, l=sd.layer: (l, i, 0)))
        out_specs.append(pl.BlockSpec((rows, c), lambda i: (i, 0)))
        out_shapes.append(jax.ShapeDtypeStruct((r, c), jnp.bfloat16))
    return in_specs, out_specs, out_shapes


def _cast_sides(side_in, side_out):
    for src, dst in zip(side_in, side_out):
        dst[...] = src[...].astype(dst.dtype)


def _rope_slab_tables(pos, freq, head_dim, half):
    ang = pos * freq
    c, sn = jnp.cos(ang), jnp.sin(ang)
    lane = lax.broadcasted_iota(jnp.int32, ang.shape, 1) & (head_dim - 1)
    s_up = jnp.where(lane < half, -sn, 0.0)
    s_dn = jnp.where(lane < half, 0.0, jnp.where(lane < 2 * half, sn, 0.0))
    return c, s_up, s_dn


def _fused_proj_kernel(x_ref, *refs, segs, rope_dims, w_transposed, n_side):
    n_w, n_rope = len(w_transposed), 2 if rope_dims else 0
    ws = refs[:n_w]
    side_in = refs[n_w + n_rope:n_w + n_rope + n_side]
    outs = refs[n_w + n_rope + n_side:len(refs) - n_side]
    _cast_sides(side_in, refs[len(refs) - n_side:])
    if rope_dims:
        pos_ref, freq_ref = refs[n_w:n_w + 2]
        tabs = [_rope_slab_tables(pos_ref[...], freq_ref[r:r + 1, :], hd, half)
                for r, (hd, half) in enumerate(rope_dims)]
    x = x_ref[...].astype(jnp.bfloat16)
    for sg in segs:
        if w_transposed[sg.w]:
            y = lax.dot_general(x, ws[sg.w][sg.col:sg.col + sg.width, :], (((1,), (1,)), ((), ())),
                                preferred_element_type=jnp.float32)
        else:
            y = jnp.dot(x, ws[sg.w][:, sg.col:sg.col + sg.width], preferred_element_type=jnp.float32)
        if sg.scale != 1.0:
            y = y * sg.scale
        if sg.silu:
            y = y * jax.nn.sigmoid(y)
        if sg.stored > sg.width:
            y = jnp.concatenate([y, jnp.zeros((y.shape[0], sg.stored - sg.width), y.dtype)], axis=1)
        if sg.rope >= 0:
            c, s_up, s_dn = tabs[sg.rope]
            slabs = [_rope_slab(y[:, j * LANES:(j + 1) * LANES], rope_dims[sg.rope][1], c, s_up, s_dn)
                     for j in range(sg.stored // LANES)]
            y = jnp.concatenate(slabs, axis=1) if len(slabs) > 1 else slabs[0]
        o_ref = outs[sg.out]
        o_ref[:, sg.out_col:sg.out_col + sg.stored] = y.astype(o_ref.dtype)


def _fused_project(x, ws, w_transposed, plan, out_dtypes, ropes, sides=()):
    s, kdim = x.shape
    segs, cols, out_cols = [], [0] * len(ws), [0] * len(out_dtypes)
    for wi, width, stored, out, rope, scale, silu in plan:
        assert stored == width or width <= SEG_MAX
        for c0 in range(0, width, SEG_MAX):
            wd = min(SEG_MAX, width - c0)
            segs.append(Seg(wi, cols[wi] + c0, wd, wd if stored == width else stored, out, out_cols[out] + c0,
                            rope, scale, silu))
        cols[wi] += width
        out_cols[out] += stored
    assert all(c == w.shape[0 if t else 1] for c, w, t in zip(cols, ws, w_transposed))
    tm = TM_PROJ
    rope_in, rope_specs = [], []
    if ropes is not None:
        pos, freqs, _ = ropes
        rope_in = [pos, freqs]
        rope_specs = [pl.BlockSpec((tm, 1), lambda i: (i, 0)), _resident(freqs.shape)]
    side_in, side_out, side_shapes = _side_io(sides, s // tm)
    res = pl.pallas_call(
        functools.partial(_fused_proj_kernel, segs=tuple(segs), rope_dims=ropes[2] if ropes is not None else (),
                          w_transposed=tuple(w_transposed), n_side=len(sides)),
        grid=(s // tm,),
        in_specs=[pl.BlockSpec((tm, kdim), lambda i: (i, 0))] + [_resident(w.shape) for w in ws]
                 + rope_specs + side_in,
        out_specs=[pl.BlockSpec((tm, n), lambda i: (i, 0)) for n in out_cols] + side_out,
        out_shape=[jax.ShapeDtypeStruct((s, n), dt) for n, dt in zip(out_cols, out_dtypes)] + side_shapes,
        compiler_params=_cparams(("parallel",)),
        name="fused_proj",
    )(x, *ws, *rope_in, *(sd.arr for sd in sides))
    return res[:len(out_dtypes)], res[len(out_dtypes):]


def _rope_freq_row(head_dim, rot_dim):
    half = rot_dim // 2
    inv = 1.0 / (ROPE_THETA ** (jnp.arange(half, dtype=jnp.float32) / half))
    row = jnp.concatenate([inv, inv, jnp.zeros((head_dim - rot_dim,), jnp.float32)])
    return jnp.tile(row, LANES // head_dim)


def _stack_heads(q_ref, n, groups):
    return jnp.concatenate(
        [q_ref[:, (n * groups + g) * HEAD_DIM:(n * groups + g + 1) * HEAD_DIM] for g in range(groups)], axis=0)


def _store_gated(o_ref, gate_ref, out, n, groups):
    tq = o_ref.shape[0]
    for g in range(groups):
        cols = slice((n * groups + g) * HEAD_DIM, (n * groups + g + 1) * HEAD_DIM)
        o_ref[:, cols] = (out[g * tq:(g + 1) * tq, :] * gate_ref[:, cols]).astype(o_ref.dtype)


COUNT_ROWS = 64


def _mixer_a_kernel(q_ref, iq_ref, iw_ref, gate_ref, ik_ref, k_ref, v_ref, o_ref,
                     score_ref, sb_ref, wt_ref, lo_ref, hi_ref, mn_ref, mx_ref,
                     qn_ref, z_ref, bias_ref, m_ref, acc_ref, *, n_sel, idx_scale):
    tq = q_ref.shape[0]
    groups = q_ref.shape[1] // HEAD_DIM // N_KV
    i = pl.program_id(0)
    nkb2 = (i + 2) // 2
    f32, bf16 = jnp.float32, jnp.bfloat16
    nt = (((1,), (1,)), ((), ()))

    key_in_blk = lax.broadcasted_iota(jnp.int32, (TK2, tq), 0)
    q_pos = lax.broadcasted_iota(jnp.int32, (TK2, tq), 1) + i * tq

    iw = iw_ref[...] * idx_scale
    wt_ref[...] = jnp.concatenate([iw, jnp.zeros((tq, LANES - IDX_HEADS), f32)], axis=1).T

    mn_ref[...] = jnp.full((tq, tq), jnp.inf, f32)
    mx_ref[...] = jnp.full((tq, tq), -jnp.inf, f32)

    def score_body(kb2, carry):
        koff = pl.multiple_of(kb2 * TK2, TK2)
        ik = ik_ref[pl.ds(koff, TK2), :IDX_DIM]
        acc = jnp.zeros((TK2, tq), f32)
        for h in range(IDX_HEADS):
            d = lax.dot_general(ik, iq_ref[:, h * IDX_DIM:(h + 1) * IDX_DIM], nt, preferred_element_type=f32)
            acc = acc + wt_ref[h:h + 1, :] * jnp.maximum(d, 0.0)
        causal = (key_in_blk + kb2 * TK2) <= q_pos
        sc = jnp.where(causal, acc, -jnp.inf)
        score_ref[pl.ds(koff, TK2), :] = sc
        sb_ref[pl.ds(koff, TK2), :] = sc.astype(bf16)
        lo_part = jnp.where(causal, acc, jnp.inf)
        mn_ref[...] = jnp.minimum(mn_ref[...], jnp.minimum(lo_part[:tq], lo_part[tq:]))
        mx_ref[...] = jnp.maximum(mx_ref[...], jnp.maximum(sc[:tq], sc[tq:]))
        return carry

    lax.fori_loop(0, nkb2, score_body, 0)

    t1 = (lax.broadcasted_iota(jnp.int32, (1, tq), 1) + i * tq + 1).astype(f32)
    k_t = jnp.minimum(t1, float(n_sel))
    row_min = jnp.min(mn_ref[...], axis=0, keepdims=True)
    row_max = jnp.max(mx_ref[...], axis=0, keepdims=True)
    if tq >= n_sel:
        lo0 = jnp.where(t1 >= float(tq), jnp.min(mx_ref[...], axis=0, keepdims=True), row_min)
    else:
        lo0 = row_min
    def to_b(x):
        return x.astype(bf16).astype(f32)

    def coarse_step(_, bracket):
        lo_b, hi_b = bracket
        mid = to_b(0.5 * lo_b + 0.5 * hi_b)
        mid16 = mid.astype(bf16)

        def count_body(kb2, acc):
            koff = pl.multiple_of(kb2 * TK2, TK2)
            for c in range(TK2 // COUNT_ROWS):
                blk = sb_ref[pl.ds(koff + c * COUNT_ROWS, COUNT_ROWS), :]
                acc = acc + jnp.where(blk >= mid16, jnp.ones((), bf16), jnp.zeros((), bf16))
            return acc

        acc = lax.fori_loop(0, nkb2, count_body, jnp.zeros((COUNT_ROWS, tq), bf16))
        ge = jnp.sum(acc.astype(f32), axis=0, keepdims=True) >= k_t
        return jnp.where(ge, mid, lo_b), jnp.where(ge, hi_b, mid)

    hi_b0 = to_b(row_max + jnp.abs(row_max) * 2.0 ** -6 + 1e-30)
    lo_b, hi_b = lax.fori_loop(0, COARSE_STEPS, coarse_step, (to_b(lo0), hi_b0))
    lo1 = jnp.maximum(lo0, lo_b - jnp.abs(lo_b) * 2.0 ** -7 - 1e-30)
    hi1 = hi_b

    settled = t1 <= float(n_sel)
    lo_ref[...] = jnp.broadcast_to(jnp.where(settled, lo0, lo1), lo_ref.shape)
    hi_ref[...] = jnp.broadcast_to(jnp.where(settled, row_min, hi1), hi_ref.shape)

    def unresolved():
        lo, hi = lo_ref[0:1, :], hi_ref[0:1, :]
        mid = 0.5 * lo + 0.5 * hi
        open_ = jnp.where(mid > lo, 1.0, 0.0) * jnp.where(mid < hi, 1.0, 0.0)
        return (jnp.max(open_) > 0.0).astype(jnp.int32)

    def bisect_step():
        lo, hi = lo_ref[0:1, :], hi_ref[0:1, :]
        mid = 0.5 * lo + 0.5 * hi

        def count_body(kb2, acc):
            koff = pl.multiple_of(kb2 * TK2, TK2)
            for c in range(TK2 // COUNT_ROWS):
                blk = score_ref[pl.ds(koff + c * COUNT_ROWS, COUNT_ROWS), :]
                acc = acc + jnp.where(blk >= mid, 1.0, 0.0)
            return acc

        acc = lax.fori_loop(0, nkb2, count_body, jnp.zeros((COUNT_ROWS, tq), f32))
        cnt = jnp.sum(acc, axis=0, keepdims=True)
        ge = cnt >= k_t
        lo_ref[...] = jnp.broadcast_to(jnp.where(ge, mid, lo), lo_ref.shape)
        hi_ref[...] = jnp.broadcast_to(jnp.where(cnt == k_t, mid, jnp.where(ge, hi, mid)), hi_ref.shape)

    def bisect_body(state):
        for _ in range(BISECT_STEPS_PER_CHECK):
            bisect_step()
        return state[0] + BISECT_STEPS_PER_CHECK, unresolved()

    lax.fori_loop(0, UNCHECKED_STEPS, lambda _, c: (bisect_step(), c)[1], 0)
    lax.while_loop(lambda st: jnp.logical_and(st[1] > 0, st[0] < MAX_BISECT), bisect_body,
                   (jnp.int32(0), unresolved()))

    @pl.when(jnp.max(jnp.where(hi_ref[0:1, :] != lo_ref[0:1, :], 1.0, 0.0)) > 0.0)
    def _():
        thr_row = lo_ref[0:1, :]

        def above_body(kb2, acc):
            blk = score_ref[pl.ds(pl.multiple_of(kb2 * TK2, TK2), TK2), :]
            return acc + jnp.sum(jnp.where(blk > thr_row, 1.0, 0.0), axis=0, keepdims=True)

        quota = k_t - lax.fori_loop(0, nkb2, above_body, jnp.zeros((1, tq), f32))
        r_idx = lax.broadcasted_iota(jnp.int32, (TK2, TK2), 0)
        c_idx = lax.broadcasted_iota(jnp.int32, (TK2, TK2), 1)
        up_to = jnp.where(c_idx <= r_idx, 1.0, 0.0).astype(bf16)

        def strike_body(kb2, seen):
            koff = pl.multiple_of(kb2 * TK2, TK2)
            blk = score_ref[pl.ds(koff, TK2), :]
            tie = jnp.where(blk == thr_row, 1.0, 0.0)
            rank = seen + jnp.dot(up_to, tie.astype(bf16), preferred_element_type=f32)
            score_ref[pl.ds(koff, TK2), :] = jnp.where(tie * jnp.where(rank > quota, 1.0, 0.0) > 0.0, -jnp.inf, blk)
            return seen + jnp.sum(tie, axis=0, keepdims=True)

        lax.fori_loop(0, nkb2, strike_body, jnp.zeros((1, tq), f32))

    thr = lo_ref[0:1, :]
    m_rows = groups * tq
    for n in range(N_KV):
        qn_ref[n] = _stack_heads(q_ref, n, groups)
    m_ref[...] = jnp.full(m_ref.shape, NEG_BIG, f32)
    acc_ref[...] = jnp.zeros(acc_ref.shape, f32)

    def logits(kb2, n):
        kblk = k_ref[pl.ds(pl.multiple_of(kb2 * TK2, TK2), TK2), n * HEAD_DIM:(n + 1) * HEAD_DIM]
        return lax.dot_general(qn_ref[n], kblk, nt, preferred_element_type=f32).astype(bf16)

    def mask_bias(kb2):
        blk = score_ref[pl.ds(pl.multiple_of(kb2 * TK2, TK2), TK2), :]
        return jnp.where(blk >= thr, 0.0, NEG_BIG).astype(bf16).T

    for n in range(N_KV):
        z_ref[n] = logits(0, n)
    bias_ref[...] = mask_bias(0)

    def attend(kb2, prefetch):
        koff = pl.multiple_of(kb2 * TK2, TK2)
        bias = jnp.concatenate([bias_ref[...]] * groups, axis=0)
        if prefetch:
            bias_ref[...] = mask_bias(kb2 + 1)
        for n in range(N_KV):
            vblk = v_ref[pl.ds(koff, TK2), n * HEAD_DIM:(n + 1) * HEAD_DIM]
            vext = jnp.concatenate([vblk, jnp.ones_like(vblk)], axis=1)
            z = z_ref[n] + bias
            if prefetch:
                z_ref[n] = logits(kb2 + 1, n)
            m_old = m_ref[n]
            m_new = jnp.maximum(m_old, jnp.max(z, axis=1, keepdims=True).astype(f32))
            p = jnp.exp2(z - jnp.concatenate([m_new.astype(bf16)] * (TK2 // LANES), axis=1))
            alpha = jnp.exp2(m_old - m_new)
            acc_ref[n] = (jnp.concatenate([alpha, alpha], axis=1) * acc_ref[n]
                          + jnp.dot(p, vext, preferred_element_type=f32))
            m_ref[n] = m_new

    def attn_body(kb2, carry):
        attend(kb2, True)
        return carry

    lax.fori_loop(0, nkb2 - 1, attn_body, 0)
    attend(nkb2 - 1, False)
    for n in range(N_KV):
        acc = acc_ref[n]
        _store_gated(o_ref, gate_ref, acc[:, :HEAD_DIM] / acc[:, HEAD_DIM:], n, groups)


def _mixer_a(q, ix, iw, gate, k, v, n_sel):
    s, width = q.shape
    groups = width // HEAD_DIM // N_KV
    f32 = jnp.float32
    idxw = IDX_HEADS * IDX_DIM
    ik_spec = pl.BlockSpec((s, LANES), lambda i: (0, idxw // LANES), pipeline_mode=pl.Buffered(1))
    kern = functools.partial(_mixer_a_kernel, n_sel=n_sel, idx_scale=(IDX_DIM ** -0.5) * (IDX_HEADS ** -0.5))
    return pl.pallas_call(
        kern,
        grid=(s // TQ,),
        in_specs=[pl.BlockSpec((TQ, width), lambda i: (i, 0)),
                  pl.BlockSpec((TQ, IDX_HEADS * IDX_DIM), lambda i: (i, 0)),
                  pl.BlockSpec((TQ, IDX_HEADS), lambda i: (i, 0)),
                  pl.BlockSpec((TQ, width), lambda i: (i, 0)),
                  ik_spec, _resident(k.shape), _resident(v.shape)],
        out_specs=pl.BlockSpec((TQ, width), lambda i: (i, 0)),
        out_shape=jax.ShapeDtypeStruct((s, width), jnp.bfloat16),
        scratch_shapes=[pltpu.VMEM((s, TQ), f32),
                        pltpu.VMEM((s, TQ), jnp.bfloat16),
                        pltpu.VMEM((LANES, TQ), f32),
                        pltpu.VMEM((8, TQ), f32),
                        pltpu.VMEM((8, TQ), f32),
                        pltpu.VMEM((TQ, TQ), f32),
                        pltpu.VMEM((TQ, TQ), f32),
                        pltpu.VMEM((N_KV, groups * TQ, HEAD_DIM), jnp.bfloat16),
                        pltpu.VMEM((N_KV, groups * TQ, TK2), jnp.bfloat16),
                        pltpu.VMEM((TQ, TK2), jnp.bfloat16),
                        pltpu.VMEM((N_KV, groups * TQ, LANES), f32),
                        pltpu.VMEM((N_KV, groups * TQ, 2 * HEAD_DIM), f32)],
        compiler_params=_cparams(("parallel",)),
        name="mixer_a",
    )(q, ix, iw, gate, ix, k, v)


def _mixer_b_kernel(q_ref, gate_ref, k_ref, v_ref, o_ref, qn_ref, c_ref, acc_ref):
    tq = q_ref.shape[0]
    tk = tq
    groups = q_ref.shape[1] // HEAD_DIM // N_KV
    m_rows = groups * tq
    i = pl.program_id(0)
    f32, bf16 = jnp.float32, jnp.bfloat16

    row = lax.broadcasted_iota(jnp.int32, (tq, tk), 0)
    col = lax.broadcasted_iota(jnp.int32, (tq, tk), 1)
    later = jnp.where(row > col, 1.0, 0.0).astype(bf16)
    later2 = jnp.concatenate([later, later], axis=0)
    strict = jnp.concatenate([col < row] * groups, axis=0)

    for n in range(N_KV):
        qn_ref[n] = _stack_heads(q_ref, n, groups)

    def step(kb, diagonal):
        koff = pl.multiple_of(kb * tk, tk)
        c_min = None
        for n in range(N_KV):
            kblk = k_ref[pl.ds(koff, tk), n * HEAD_DIM:(n + 1) * HEAD_DIM]
            vblk = v_ref[pl.ds(koff, tk), n * HEAD_DIM:(n + 1) * HEAD_DIM]
            z = lax.dot_general(qn_ref[n], kblk, (((1,), (1,)), ((), ())), preferred_element_type=f32)
            t = jnp.log(1.0 + jnp.exp2(jnp.abs(z) * (-LOG2_E)))
            sp = jnp.maximum(z, 0.0) + t
            if diagonal:
                sp = jnp.where(strict, sp, 0.0)
            sp_hi = sp.astype(bf16)
            sp_lo = (sp - sp_hi.astype(f32)).astype(bf16)
            suffix = jnp.dot(jnp.concatenate([sp_hi, sp_lo], axis=1), later2, preferred_element_type=f32)
            c_old = c_ref[n]
            total = suffix + jnp.concatenate([c_old] * (tk // LANES), axis=1)
            a = jnp.exp((jnp.minimum(z, 0.0) - t) - total)
            if diagonal:
                a = jnp.where(strict, a, 0.0)
            acc_ref[n] += jnp.dot(a.astype(bf16), vblk, preferred_element_type=f32)
            c_new = c_old + jnp.broadcast_to(jnp.sum(sp, axis=1, keepdims=True), (m_rows, LANES))
            c_ref[n] = c_new
            c_min = c_new if c_min is None else jnp.minimum(c_min, c_new)
        return (jnp.min(c_min) <= EXP_ZERO_ABOVE).astype(jnp.int32)

    c_ref[...] = jnp.zeros((N_KV, m_rows, LANES), f32)
    acc_ref[...] = jnp.zeros((N_KV, m_rows, HEAD_DIM), f32)
    go = step(i, True)
    lax.while_loop(lambda st: jnp.logical_and(st[0] >= 0, st[1] > 0),
                   lambda st: (st[0] - 1, step(st[0], False)), (i - 1, go))
    for n in range(N_KV):
        _store_gated(o_ref, gate_ref, acc_ref[n], n, groups)


def _mixer_b(q, gate, kv):
    s, width = q.shape
    groups = width // HEAD_DIM // N_KV
    kvw = kv.shape[1] // 2
    half_spec = lambda c: pl.BlockSpec((s, kvw), lambda i: (0, c), pipeline_mode=pl.Buffered(1))
    return pl.pallas_call(
        _mixer_b_kernel,
        grid=(s // TQ,),
        in_specs=[pl.BlockSpec((TQ, width), lambda i: (i, 0)),
                  pl.BlockSpec((TQ, width), lambda i: (i, 0)),
                  half_spec(0), half_spec(1)],
        out_specs=pl.BlockSpec((TQ, width), lambda i: (i, 0)),
        out_shape=jax.ShapeDtypeStruct((s, width), jnp.bfloat16),
        scratch_shapes=[pltpu.VMEM((N_KV, groups * TQ, HEAD_DIM), jnp.bfloat16),
                        pltpu.VMEM((N_KV, groups * TQ, LANES), jnp.float32),
                        pltpu.VMEM((N_KV, groups * TQ, HEAD_DIM), jnp.float32)],
        compiler_params=_cparams(("parallel",)),
        name="mixer_b",
    )(q, gate, kv, kv)


def _post_kernel(og_ref, x_ref, p_ref, wo_ref, wg_ref, wp_ref, lng_ref, lnb_ref, *refs, alpha, n_side):
    f32 = jnp.float32
    o_ref = refs[n_side]
    _cast_sides(refs[:n_side], refs[n_side + 1:])
    h = jnp.dot(og_ref[...], wo_ref[...], preferred_element_type=f32)
    y = alpha * x_ref[...] + h
    mu = jnp.mean(y, axis=-1, keepdims=True)
    d = y - mu
    var = jnp.mean(d * d, axis=-1, keepdims=True)
    yn = d * lax.rsqrt(var + LN_EPS) * lng_ref[...] + lnb_ref[...]
    gl = jnp.dot(yn.astype(jnp.bfloat16), wg_ref[...], preferred_element_type=f32)
    pe = jnp.dot(p_ref[...].astype(jnp.bfloat16), wp_ref[...], preferred_element_type=f32)
    o_ref[...] = yn + pe * jax.nn.sigmoid(gl)


def _resident_slab(arr, idx):
    return pl.BlockSpec((None,) + arr.shape[1:], lambda *_: (idx,) + (0,) * (arr.ndim - 1),
                        pipeline_mode=pl.Buffered(1))


def _post(og, x, p, layer, batch, wo, wg, wp, lng, lnb, alpha, sides=()):
    s, d = x.shape
    tm = TM_PROJ
    side_in, side_out, side_shapes = _side_io(sides, s // tm)
    res = pl.pallas_call(
        functools.partial(_post_kernel, alpha=alpha, n_side=len(sides)),
        grid=(s // tm,),
        in_specs=[pl.BlockSpec((tm, og.shape[1]), lambda i: (i, 0)),
                  pl.BlockSpec((tm, d), lambda i: (i, 0)),
                  pl.BlockSpec((None, None, tm, p.shape[3]), lambda i: (layer, batch, i, 0)),
                  _resident(wo.shape), _resident(wg.shape), _resident_slab(wp, layer),
                  _resident_slab(lng, layer), _resident_slab(lnb, layer)] + side_in,
        out_specs=[pl.BlockSpec((tm, d), lambda i: (i, 0))] + side_out,
        out_shape=[jax.ShapeDtypeStruct((s, d), jnp.float32)] + side_shapes,
        compiler_params=_cparams(("parallel",)),
        name="post",
    )(og, x, p, wo, wg, wp, lng, lnb, *(sd.arr for sd in sides))
    return res[0], res[1:]


def _now_bf16(sd):
    return (sd.arr if sd.layer < 0 else sd.arr[sd.layer]).astype(jnp.bfloat16)


def kernel(x, p, positions, w_in_a, w_o_a, w_kv_b, w_in_b, w_o_b, ln_g, ln_b, w_ple, w_ple_gate):
    bsz, s, d = x.shape
    n_a, n_b = w_in_a.shape[0], w_in_b.shape[0]
    depth = n_a + n_b
    alpha = (2.0 * depth) ** 0.25
    width = w_o_a.shape[1]
    kvw = N_KV * HEAD_DIM
    idxw = IDX_HEADS * IDX_DIM
    n_sel = min(TOPK_MAX, s // 4)
    bf16, f32 = jnp.bfloat16, jnp.float32
    sm_scale = HEAD_DIM ** -0.5
    assert s % TK2 == 0 and s // COUNT_ROWS <= 256
    n_steps = s // TM_PROJ
    wp_all = w_ple.astype(bf16)
    lng, lnb = ln_g[:, None, :], ln_b[:, None, :]

    def proj_sources(layer):
        if layer < n_a:
            return [Side(jnp.swapaxes(w_in_a[layer], 0, 1), -1)]
        first_b = [Side(w_kv_b, -1)] if layer == n_a else []
        return first_b + [Side(w_in_b, layer - n_a)]

    def riders(srcs):
        return srcs if all(_side_ok(sd.arr, n_steps) for sd in srcs) else []

    proj_w, post_w = {}, {}
    outs = []
    for b in range(bsz):
        xb = x[b]
        ropes = (positions[b].astype(f32)[:, None],
                 jnp.stack([_rope_freq_row(HEAD_DIM, ROT_DIM), _rope_freq_row(IDX_DIM, IDX_ROT)]),
                 ((HEAD_DIM, ROT_DIM // 2), (IDX_DIM, IDX_ROT // 2)))
        kv_shared = None
        for i in range(depth):
            if i not in proj_w:
                proj_w[i] = [_now_bf16(sd) for sd in proj_sources(i)]
            post_src = [Side(w_o_a, i) if i < n_a else Side(w_o_b, i - n_a), Side(w_ple_gate, i)]
            post_ride = riders(post_src) if i not in post_w else []
            if i < n_a:
                plan = [(0, width, width, 0, 0, LOG2_E * sm_scale, False), (0, kvw, kvw, 1, 0, 1.0, False),
                        (0, kvw, kvw, 2, -1, 1.0, False), (0, width, width, 3, -1, 1.0, True),
                        (0, idxw, idxw, 4, 1, 1.0, False), (0, IDX_DIM, LANES, 4, 1, 1.0, False),
                        (0, IDX_HEADS, IDX_HEADS, 5, -1, 1.0, False)]
                (q, k, v, gate, ix, iw), cast = _fused_project(
                    xb, proj_w[i], [True], plan, [bf16, bf16, bf16, f32, bf16, f32], ropes, post_ride)
                og = _mixer_a(q, ix, iw, gate, k, v, n_sel)
            elif kv_shared is None:
                plan = [(0, 2 * kvw, 2 * kvw, 0, -1, 1.0, False), (1, width, width, 1, -1, sm_scale, False),
                        (1, width, width, 2, -1, 1.0, True)]
                (kv_shared, q, gate), cast = _fused_project(xb, proj_w[i], [False, False], plan, [bf16, bf16, f32],
                                                            None, post_ride)
                og = _mixer_b(q, gate, kv_shared)
            else:
                plan = [(0, width, width, 0, -1, sm_scale, False), (0, width, width, 1, -1, 1.0, True)]
                (q, gate), cast = _fused_project(xb, proj_w[i], [False], plan, [bf16, f32], None, post_ride)
                og = _mixer_b(q, gate, kv_shared)
            if i not in post_w:
                post_w[i] = tuple(cast) if post_ride else tuple(_now_bf16(sd) for sd in post_src)
            next_ride = riders(proj_sources(i + 1)) if i + 1 < depth and i + 1 not in proj_w else []
            xb, cast = _post(og, xb, p, i, b, post_w[i][0], post_w[i][1], wp_all, lng, lnb, alpha, next_ride)
            if next_ride:
                proj_w[i + 1] = list(cast)
        outs.append(xb)
    return jnp.stack(outs, axis=0)
```
